```python
import math
import jax, jax.numpy as jnp
from jax import lax
import numpy as np

D_MODEL = 1024
BATCH = 8
SEQ = 4096
DEPTH = 1

CHUNK = 64
MEM_TOKENS = 256

SSM_WIDTH = D_MODEL // 2
SSM_GROUP_CH = 16
SSM_GROUPS = SSM_WIDTH // SSM_GROUP_CH
SSM_STATE = 64
FOX_WIDTH = D_MODEL - SSM_WIDTH
FOX_HEAD_DIM = 64
FOX_HEADS = FOX_WIDTH // FOX_HEAD_DIM
Q_BLOCK = 128
IN_PROJ_COLS = SSM_WIDTH + 3 * FOX_WIDTH + FOX_HEADS

XA_HEADS = 4
XA_HEAD_DIM = D_MODEL // XA_HEADS

FFN_HIDDEN = int(math.ceil(8 * D_MODEL / 3 / 256) * 256)

RMS_EPS = 1e-6
NEG_INF = -1e30
DT_MIN = 1e-3
DT_MAX = 1e-1

kernel_name = "hybrid_s5_fox_memxattn_block"


def rms_norm(x, g):
    xf = x.astype(jnp.float32)
    y = xf * lax.rsqrt(jnp.mean(xf * xf, axis=-1, keepdims=True) + RMS_EPS)
    return (y * g.astype(jnp.float32)).astype(x.dtype)


def s5_mixer(u, a_re, a_im, log_dt, b_re, b_im, c_re, c_im, d_skip, glu_w, glu_b):
    bsz, seq, _ = u.shape
    uf = u.astype(jnp.float32).reshape(bsz, seq, SSM_GROUPS, SSM_GROUP_CH)
    a = lax.complex(a_re.astype(jnp.float32), a_im.astype(jnp.float32))
    dt = jnp.exp(log_dt.astype(jnp.float32))[:, None]
    a_bar = jnp.exp(a * dt)
    b = lax.complex(b_re.astype(jnp.float32), b_im.astype(jnp.float32))
    b_bar = ((a_bar - 1.0) / a)[..., None] * b
    bu = jnp.einsum('bsgi,gni->bsgn', uf.astype(jnp.complex64), b_bar)
    a_seq = jnp.broadcast_to(a_bar, bu.shape)

    def combine(left, right):
        a_l, x_l = left
        a_r, x_r = right
        return a_l * a_r, a_r * x_l + x_r

    _, states = lax.associative_scan(combine, (a_seq, bu), axis=1)
    c = lax.complex(c_re.astype(jnp.float32), c_im.astype(jnp.float32))
    y = jnp.real(jnp.einsum('bsgn,gin->bsgi', states, c))
    y = y + d_skip.astype(jnp.float32).reshape(SSM_GROUPS, SSM_GROUP_CH) * uf
    y = y.reshape(bsz, seq, SSM_WIDTH)
    g = jax.nn.gelu(y)
    out = g * jax.nn.sigmoid(g @ glu_w.astype(jnp.float32) + glu_b.astype(jnp.float32))
    return out.astype(u.dtype)


def forgetting_attention(q, k, v, f_logit):
    bsz, seq, heads, hd = q.shape
    scale = 1.0 / math.sqrt(hd)
    cum_log_f = jnp.cumsum(jax.nn.log_sigmoid(f_logit.astype(jnp.float32)), axis=1)
    cum_log_f = jnp.transpose(cum_log_f, (0, 2, 1))
    outs = []
    for blk in range(seq // Q_BLOCK):
        q0 = blk * Q_BLOCK
        k_end = q0 + Q_BLOCK
        qb = q[:, q0:k_end]
        kb = k[:, :k_end]
        vb = v[:, :k_end]
        s = jnp.einsum('bqhd,bkhd->bhqk', qb, kb).astype(jnp.float32) * scale
        bias = cum_log_f[:, :, q0:k_end, None] - cum_log_f[:, :, None, :k_end]
        q_pos = q0 + jnp.arange(Q_BLOCK)
        k_pos = jnp.arange(k_end)
        mask = q_pos[:, None] >= k_pos[None, :]
        s = jnp.where(mask, s + bias, NEG_INF)
        p = jax.nn.softmax(s, axis=-1)
        outs.append(jnp.einsum('bhqk,bkhd->bqhd', p.astype(v.dtype), vb))
    return jnp.concatenate(outs, axis=1)


def memory_cross_attention(h, mem_n, wq, wkv, wo):
    bsz, seq, _ = h.shape
    m = mem_n.shape[1]
    q = (h @ wq).reshape(bsz, seq, XA_HEADS, XA_HEAD_DIM)
    kv = mem_n @ wkv
    k = kv[..., :D_MODEL].reshape(bsz, m, XA_HEADS, XA_HEAD_DIM)
    v = kv[..., D_MODEL:].reshape(bsz, m, XA_HEADS, XA_HEAD_DIM)
    s = jnp.einsum('bqhd,bmhd->bhqm', q, k).astype(jnp.float32) / math.sqrt(XA_HEAD_DIM)
    p = jax.nn.softmax(s, axis=-1)
    o = jnp.einsum('bhqm,bmhd->bqhd', p.astype(v.dtype), v).reshape(bsz, seq, D_MODEL)
    return o @ wo


def swiglu(h, w_gate, w_up, w_down):
    return (jax.nn.silu(h @ w_gate) * (h @ w_up)) @ w_down


def setup_inputs(seed: int = 0) -> dict:
    key = jax.random.key(seed)
    ks = jax.random.split(key, 40)
    f32 = jnp.float32

    def dense(k, fan_in, shape):
        return jax.random.normal(k, shape, f32) * fan_in ** -0.5

    def gain(k, n):
        return jnp.ones((n,), f32) + 0.02 * jax.random.normal(k, (n,), f32)

    n_idx = jnp.arange(SSM_STATE, dtype=f32)
    a_re = -0.5 + 0.01 * jax.random.normal(ks[4], (SSM_GROUPS, SSM_STATE), f32)
    a_im = math.pi * n_idx[None, :] + 0.01 * jax.random.normal(ks[5], (SSM_GROUPS, SSM_STATE), f32)
    log_dt = jax.random.uniform(ks[6], (SSM_GROUPS,), f32, math.log(DT_MIN), math.log(DT_MAX))
    b_scale = (2.0 * SSM_GROUP_CH) ** -0.5
    c_scale = (2.0 * SSM_STATE) ** -0.5
    return {
        "x": jax.random.normal(ks[0], (BATCH, SEQ, D_MODEL), f32),
        "mem": jax.random.normal(ks[1], (BATCH, MEM_TOKENS, D_MODEL), f32),
        "mix_pre_g": gain(ks[2], D_MODEL),
        "w_in": dense(ks[3], D_MODEL, (D_MODEL, IN_PROJ_COLS)),
        "ssm_a_re": a_re,
        "ssm_a_im": a_im,
        "ssm_log_dt": log_dt,
        "ssm_b_re": jax.random.normal(ks[7], (SSM_GROUPS, SSM_STATE, SSM_GROUP_CH), f32) * b_scale,
        "ssm_b_im": jax.random.normal(ks[8], (SSM_GROUPS, SSM_STATE, SSM_GROUP_CH), f32) * b_scale,
        "ssm_c_re": jax.random.normal(ks[9], (SSM_GROUPS, SSM_GROUP_CH, SSM_STATE), f32) * c_scale,
        "ssm_c_im": jax.random.normal(ks[10], (SSM_GROUPS, SSM_GROUP_CH, SSM_STATE), f32) * c_scale,
        "ssm_d": jax.random.normal(ks[11], (SSM_WIDTH,), f32),
        "ssm_glu_w": dense(ks[12], SSM_WIDTH, (SSM_WIDTH, SSM_WIDTH)),
        "ssm_glu_b": 0.01 * jax.random.normal(ks[13], (SSM_WIDTH,), f32),
        "fox_f_bias": jax.random.uniform(ks[14], (FOX_HEADS,), f32, 1.0, 4.0),
        "ssm_out_g": gain(ks[15], SSM_WIDTH),
        "fox_out_g": gain(ks[16], FOX_WIDTH),
        "w_out": dense(ks[17], D_MODEL, (D_MODEL, D_MODEL)),
        "mix_post_g": gain(ks[18], D_MODEL),
        "xa_pre_g": gain(ks[19], D_MODEL),
        "mem_g": gain(ks[20], D_MODEL),
        "xa_wq": dense(ks[21], D_MODEL, (D_MODEL, D_MODEL)),
        "xa_wkv": dense(ks[22], D_MODEL, (D_MODEL, 2 * D_MODEL)),
        "xa_wo": dense(ks[23], D_MODEL, (D_MODEL, D_MODEL)),
        "xa_post_g": gain(ks[24], D_MODEL),
        "ffn_pre_g": gain(ks[25], D_MODEL),
        "w_gate": dense(ks[26], D_MODEL, (D_MODEL, FFN_HIDDEN)),
        "w_up": dense(ks[27], D_MODEL, (D_MODEL, FFN_HIDDEN)),
        "w_down": dense(ks[28], FFN_HIDDEN, (FFN_HIDDEN, D_MODEL)),
        "ffn_post_g": gain(ks[29], D_MODEL),
    }


def reference(x, mem, mix_pre_g, w_in, ssm_a_re, ssm_a_im, ssm_log_dt, ssm_b_re, ssm_b_im,
              ssm_c_re, ssm_c_im, ssm_d, ssm_glu_w, ssm_glu_b, fox_f_bias, ssm_out_g, fox_out_g,
              w_out, mix_post_g, xa_pre_g, mem_g, xa_wq, xa_wkv, xa_wo, xa_post_g,
              ffn_pre_g, w_gate, w_up, w_down, ffn_post_g):
    bsz, seq, _ = x.shape
    mem_n = rms_norm(mem, mem_g)
    for _layer in range(DEPTH):
        h = rms_norm(x, mix_pre_g)
        proj = h @ w_in
        o0 = SSM_WIDTH
        u = proj[..., :o0]
        q = proj[..., o0:o0 + FOX_WIDTH].reshape(bsz, seq, FOX_HEADS, FOX_HEAD_DIM)
        k = proj[..., o0 + FOX_WIDTH:o0 + 2 * FOX_WIDTH].reshape(bsz, seq, FOX_HEADS, FOX_HEAD_DIM)
        v = proj[..., o0 + 2 * FOX_WIDTH:o0 + 3 * FOX_WIDTH].reshape(bsz, seq, FOX_HEADS, FOX_HEAD_DIM)
        f_logit = proj[..., o0 + 3 * FOX_WIDTH:] + fox_f_bias

        y_ssm = s5_mixer(u, ssm_a_re, ssm_a_im, ssm_log_dt, ssm_b_re, ssm_b_im,
                         ssm_c_re, ssm_c_im, ssm_d, ssm_glu_w, ssm_glu_b)
        y_fox = forgetting_attention(q, k, v, f_logit).reshape(bsz, seq, FOX_WIDTH)
        y_mix = jnp.concatenate([rms_norm(y_ssm, ssm_out_g), rms_norm(y_fox, fox_out_g)], axis=-1)
        x = x + rms_norm(y_mix @ w_out, mix_post_g)

        h = rms_norm(x, xa_pre_g)
        x = x + rms_norm(memory_cross_attention(h, mem_n, xa_wq, xa_wkv, xa_wo), xa_post_g)

        h = rms_norm(x, ffn_pre_g)
        x = x + rms_norm(swiglu(h, w_gate, w_up, w_down), ffn_post_g)
    return x
```

```python
import functools
import math

import jax
import jax.numpy as jnp
from jax import lax
from jax.experimental import pallas as pl
from jax.experimental.pallas import tpu as pltpu

F32 = jnp.float32
BF16 = jnp.bfloat16

LANES = 128
SUBLANES = 8
VMEM_LIMIT_BYTES = 56 * 1024 * 1024

D_MODEL = 1024
SSM_WIDTH = 512
SSM_GROUP_CH = 16
SSM_GROUPS = 32
SSM_STATE = 64
FOX_WIDTH = 512
FOX_HEAD_DIM = 64
FOX_HEADS = 8
FOX_PAIRS = FOX_HEADS * FOX_HEAD_DIM // LANES
XA_HEADS = 4
XA_HEAD_DIM = 256
RMS_EPS = 1e-6
NEG_INF = -1e30

STATE_TILES = 2 * SSM_GROUPS * SSM_STATE // LANES
RE_TILES = STATE_TILES // 2
U_TILES = SSM_WIDTH // LANES


def _rms(x, g):
    return x * lax.rsqrt(jnp.mean(x * x, axis=-1, keepdims=True) + RMS_EPS) * g


def _dot(a, b):
    return jnp.dot(a, b, preferred_element_type=F32)


def _dot_nt(a, b):
    return lax.dot_general(a, b, (((1,), (1,)), ((), ())), preferred_element_type=F32)


def _sigmoid(z):
    return 1.0 / (1.0 + jnp.exp(-z))


def _memkv_kernel(mem_ref, g_ref, w_ref, k_ref, v_ref):
    mn = _rms(mem_ref[...], g_ref[...]).astype(BF16)
    kv = _dot(mn, w_ref[...])
    k_ref[...] = (kv[:, :D_MODEL] * (1.0 / math.sqrt(XA_HEAD_DIM))).astype(BF16)
    v_ref[...] = kv[:, D_MODEL:].astype(BF16)


def _memkv(mem, mem_g, wkv):
    b, m, d = mem.shape
    return pl.pallas_call(
        _memkv_kernel,
        grid=(b,),
        in_specs=[
            pl.BlockSpec((None, m, d), lambda i: (i, 0, 0)),
            pl.BlockSpec((1, d), lambda i: (0, 0)),
            pl.BlockSpec((d, 2 * d), lambda i: (0, 0)),
        ],
        out_specs=[
            pl.BlockSpec((None, m, d), lambda i: (i, 0, 0)),
            pl.BlockSpec((None, m, d), lambda i: (i, 0, 0)),
        ],
        out_shape=[jax.ShapeDtypeStruct((b, m, d), BF16)] * 2,
        compiler_params=pltpu.CompilerParams(
            dimension_semantics=("arbitrary",), vmem_limit_bytes=VMEM_LIMIT_BYTES),
        name="memkv",
    )(mem, mem_g.reshape(1, d), wkv)


def _inproj_kernel(x_ref, g_ref, w_ref, wf_ref, fb_ref, tri_ref,
                   u_ref, q_ref, k_ref, v_ref, f_ref, carry_ref):
    @pl.when(pl.program_id(1) == 0)
    def _():
        carry_ref[...] = jnp.zeros_like(carry_ref)

    hb = _rms(x_ref[...], g_ref[...]).astype(BF16)
    proj = _dot(hb, w_ref[...])
    u_ref[...] = proj[:, :SSM_WIDTH].astype(BF16)
    o = SSM_WIDTH
    q_ref[...] = (proj[:, o:o + FOX_WIDTH] * (1.0 / math.sqrt(FOX_HEAD_DIM))).astype(BF16)
    k_ref[...] = proj[:, o + FOX_WIDTH:o + 2 * FOX_WIDTH].astype(BF16)
    v_ref[...] = proj[:, o + 2 * FOX_WIDTH:].astype(BF16)

    z = _dot_nt(wf_ref[...], hb) + fb_ref[...]
    logf = jnp.minimum(z, 0.0) - jnp.log1p(jnp.exp(-jnp.abs(z)))
    p1 = logf.astype(BF16)
    r1 = logf - p1.astype(F32)
    p2 = r1.astype(BF16)
    p3 = (r1 - p2.astype(F32)).astype(BF16)
    tri = tri_ref[...]
    csum = _dot(p1, tri) + _dot(p2, tri) + _dot(p3, tri)
    f = csum + carry_ref[...][:, :1]
    f_ref[...] = f
    carry_ref[...] = jnp.broadcast_to(f[:, -1:], carry_ref.shape)


def _inproj(x, mix_pre_g, w_main, wf_t, f_bias, tm):
    b, s, d = x.shape
    nt = s // tm
    tri = jnp.triu(jnp.ones((tm, tm), F32)).astype(BF16)
    row = lambda i, j: (i, j, 0)
    const2 = lambda i, j: (0, 0)
    return pl.pallas_call(
        _inproj_kernel,
        grid=(b, nt),
        in_specs=[
            pl.BlockSpec((None, tm, d), row),
            pl.BlockSpec((1, d), const2),
            pl.BlockSpec(w_main.shape, const2),
            pl.BlockSpec(wf_t.shape, const2),
            pl.BlockSpec((FOX_HEADS, 1), const2),
            pl.BlockSpec((tm, tm), const2),
        ],
        out_specs=[
            pl.BlockSpec((None, tm, SSM_WIDTH), row),
            pl.BlockSpec((None, tm, FOX_WIDTH), row),
            pl.BlockSpec((None, tm, FOX_WIDTH), row),
            pl.BlockSpec((None, tm, FOX_WIDTH), row),
            pl.BlockSpec((None, FOX_HEADS, tm), lambda i, j: (i, 0, j)),
        ],
        out_shape=[
            jax.ShapeDtypeStruct((b, s, SSM_WIDTH), BF16),
            jax.ShapeDtypeStruct((b, s, FOX_WIDTH), BF16),
            jax.ShapeDtypeStruct((b, s, FOX_WIDTH), BF16),
            jax.ShapeDtypeStruct((b, s, FOX_WIDTH), BF16),
            jax.ShapeDtypeStruct((b, FOX_HEADS, s), F32),
        ],
        scratch_shapes=[pltpu.VMEM((FOX_HEADS, LANES), F32)],
        compiler_params=pltpu.CompilerParams(
            dimension_semantics=("arbitrary", "arbitrary"), vmem_limit_bytes=VMEM_LIMIT_BYTES),
        name="inproj",
    )(x, mix_pre_g.reshape(1, d), w_main, wf_t, f_bias.reshape(FOX_HEADS, 1), tri)


def _gelu_tanh(x):
    c = math.sqrt(2.0 / math.pi)
    return 0.5 * x * (1.0 + jnp.tanh(c * (x + 0.044715 * (x * x * x))))


def _ssm_kernel(u_ref, bblk_ref, cblk_ref, are_ref, aim_ref, d_ref, gw_ref, gb_ref, g_ref,
                o_ref, s_ref, xc_ref, *, nb, tm, pitch):
    @pl.when(pl.program_id(0) == 0)
    def _():
        xc_ref[...] = jnp.zeros_like(xc_ref)

    def drive(b, _):
        for ut in range(U_TILES):
            ub = u_ref[b, :, ut * LANES:(ut + 1) * LANES]
            res = _dot(ub, bblk_ref[ut])
            for j in range(2 * U_TILES):
                tile = U_TILES * ut + j if j < U_TILES else RE_TILES + U_TILES * ut + (j - U_TILES)
                r0 = pl.multiple_of((b * STATE_TILES + tile) * pitch, SUBLANES)
                s_ref[pl.ds(r0, tm), :] = res[:, j * LANES:(j + 1) * LANES]
        return 0
    lax.fori_loop(0, nb, drive, 0)

    ar = are_ref[...]
    ai = aim_ref[...]

    def step(t, carry):
        new = []
        for b in range(nb):
            xr, xi = carry[2 * b], carry[2 * b + 1]
            ire = pl.ds(b * STATE_TILES * pitch + t, RE_TILES, stride=pitch)
            iim = pl.ds((b * STATE_TILES + RE_TILES) * pitch + t, RE_TILES, stride=pitch)
            nr = ar * xr - ai * xi + s_ref[ire, :]
            ni = ar * xi + ai * xr + s_ref[iim, :]
            s_ref[ire, :] = nr
            s_ref[iim, :] = ni
            new += [nr, ni]
        return tuple(new)

    init = []
    for b in range(nb):
        init.append(xc_ref[b * STATE_TILES:b * STATE_TILES + RE_TILES, :])
        init.append(xc_ref[b * STATE_TILES + RE_TILES:(b + 1) * STATE_TILES, :])
    fin = lax.fori_loop(0, tm, step, tuple(init))
    for b in range(nb):
        xc_ref[b * STATE_TILES:b * STATE_TILES + RE_TILES, :] = fin[2 * b]
        xc_ref[b * STATE_TILES + RE_TILES:(b + 1) * STATE_TILES, :] = fin[2 * b + 1]

    def readout(b, _):
        ys = []
        for ot in range(U_TILES):
            tiles = []
            for j in range(2 * U_TILES):
                tile = U_TILES * ot + j if j < U_TILES else RE_TILES + U_TILES * ot + (j - U_TILES)
                r0 = pl.multiple_of((b * STATE_TILES + tile) * pitch, SUBLANES)
                tiles.append(s_ref[pl.ds(r0, tm), :].astype(BF16))
            xs = jnp.concatenate(tiles, axis=1)
            ys.append(_dot(xs, cblk_ref[ot]))
        y = jnp.concatenate(ys, axis=1) + d_ref[...] * u_ref[b].astype(F32)
        g = _gelu_tanh(y)
        out = g * _sigmoid(_dot(g.astype(BF16), gw_ref[...]) + gb_ref[...])
        o_ref[b] = _rms(out, g_ref[...]).astype(BF16)
        return 0
    lax.fori_loop(0, nb, readout, 0)


def _ssm(u, bblk, cblk, a_re, a_im, d_skip, glu_w, glu_b, out_g, tm):
    b, s, w = u.shape
    pitch = tm + SUBLANES
    kern = functools.partial(_ssm_kernel, nb=b, tm=tm, pitch=pitch)
    c2 = lambda t: (0, 0)
    c3 = lambda t: (0, 0, 0)
    return pl.pallas_call(
        kern,
        grid=(s // tm,),
        in_specs=[
            pl.BlockSpec((b, tm, w), lambda t: (0, t, 0)),
            pl.BlockSpec(bblk.shape, c3),
            pl.BlockSpec(cblk.shape, c3),
            pl.BlockSpec(a_re.shape, c2),
            pl.BlockSpec(a_im.shape, c2),
            pl.BlockSpec((1, w), c2),
            pl.BlockSpec(glu_w.shape, c2),
            pl.BlockSpec((1, w), c2),
            pl.BlockSpec((1, w), c2),
        ],
        out_specs=pl.BlockSpec((b, tm, w), lambda t: (0, t, 0)),
        out_shape=jax.ShapeDtypeStruct((b, s, w), BF16),
        scratch_shapes=[
            pltpu.VMEM((b * STATE_TILES * pitch, LANES), F32),
            pltpu.VMEM((b * STATE_TILES, LANES), F32),
        ],
        compiler_params=pltpu.CompilerParams(
            dimension_semantics=("arbitrary",), vmem_limit_bytes=VMEM_LIMIT_BYTES),
        name="ssm",
    )(u, bblk, cblk, a_re, a_im, d_skip.reshape(1, w), glu_w, glu_b.reshape(1, w),
      out_g.reshape(1, w))


def _ssm_params(a_re, a_im, log_dt, b_re, b_im, c_re, c_im):
    a = lax.complex(a_re.astype(F32), a_im.astype(F32))
    dt = jnp.exp(log_dt.astype(F32))[:, None]
    a_bar = jnp.exp(a * dt)
    b_bar = ((a_bar - 1.0) / a)[..., None] * lax.complex(b_re.astype(F32), b_im.astype(F32))
    gpt = LANES // SSM_GROUP_CH
    eye = jnp.eye(gpt, dtype=F32)

    def bmat(part):
        p = part.reshape(U_TILES, gpt, SSM_STATE, SSM_GROUP_CH)
        m = jnp.einsum('ugni,gh->ugihn', p, eye)
        return m.reshape(U_TILES, LANES, gpt * SSM_STATE)

    def cmat(part):
        p = part.reshape(U_TILES, gpt, SSM_GROUP_CH, SSM_STATE)
        m = jnp.einsum('ugin,gh->ugnhi', p, eye)
        return m.reshape(U_TILES, gpt * SSM_STATE, LANES)

    bblk = jnp.concatenate([bmat(jnp.real(b_bar)), bmat(jnp.imag(b_bar))], axis=2).astype(BF16)
    cblk = jnp.concatenate([cmat(c_re.astype(F32)), cmat(-c_im.astype(F32))], axis=1).astype(BF16)
    are = jnp.real(a_bar).reshape(RE_TILES, LANES)
    aim = jnp.imag(a_bar).reshape(RE_TILES, LANES)
    return bblk, cblk, are, aim


def _fox_kernel(q_ref, k_ref, v_ref, frow_ref, fcol_ref, o_ref, m_ref, l_ref, acc_ref, *, tq):
    i = pl.program_id(2)
    q = q_ref[...]
    lane = lax.broadcasted_iota(jnp.int32, q.shape, 1)
    first = lane < FOX_HEAD_DIM
    zero = jnp.zeros_like(q)
    qh = (jnp.where(first, q, zero), jnp.where(first, zero, q))
    fq = (fcol_ref[:, 0:1], fcol_ref[:, 1:2])

    m_ref[...] = jnp.full_like(m_ref, NEG_INF)
    l_ref[...] = jnp.zeros_like(l_ref)
    acc_ref[...] = jnp.zeros_like(acc_ref)

    def block(j, masked):
        k0 = pl.multiple_of(j * tq, tq)
        ks = k_ref[pl.ds(k0, tq), :]
        vs = v_ref[pl.ds(k0, tq), :]
        fk = frow_ref[:, pl.ds(k0, tq)]
        for h in range(2):
            s = _dot_nt(qh[h], ks) + (fq[h] - fk[h:h + 1, :])
            if masked:
                r = lax.broadcasted_iota(jnp.int32, s.shape, 0)
                c = lax.broadcasted_iota(jnp.int32, s.shape, 1)
                s = jnp.where(r >= c, s, NEG_INF)
            m_old = m_ref[h]
            m_new = jnp.maximum(m_old, jnp.max(s, axis=-1, keepdims=True))
            alpha = jnp.exp(m_old - m_new)
            p = jnp.exp(s - m_new)
            l_ref[h] = alpha * l_ref[h] + jnp.sum(p, axis=-1, keepdims=True)
            acc_ref[h] = alpha * acc_ref[h] + _dot(p.astype(BF16), vs)
            m_ref[h] = m_new

    def off_diag(j, _):
        block(j, False)
        return 0
    lax.fori_loop(0, i, off_diag, 0)
    block(i, True)

    o0 = acc_ref[0] / l_ref[0]
    o1 = acc_ref[1] / l_ref[1]
    o_ref[...] = jnp.where(first, o0, o1).astype(BF16)


def _fox(q, k, v, frow, fcol, tq):
    b, s, w = q.shape
    kern = functools.partial(_fox_kernel, tq=tq)
    return pl.pallas_call(
        kern,
        grid=(b, FOX_PAIRS, s // tq),
        in_specs=[
            pl.BlockSpec((None, tq, LANES), lambda bi, p, i: (bi, i, p)),
            pl.BlockSpec((None, s, LANES), lambda bi, p, i: (bi, 0, p)),
            pl.BlockSpec((None, s, LANES), lambda bi, p, i: (bi, 0, p)),
            pl.BlockSpec((None, None, 2, s), lambda bi, p, i: (bi, p, 0, 0)),
            pl.BlockSpec((None, None, tq, 2), lambda bi, p, i: (bi, p, i, 0)),
        ],
        out_specs=pl.BlockSpec((None, tq, LANES), lambda bi, p, i: (bi, i, p)),
        out_shape=jax.ShapeDtypeStruct((b, s, w), BF16),
        scratch_shapes=[
            pltpu.VMEM((2, tq, 1), F32),
            pltpu.VMEM((2, tq, 1), F32),
            pltpu.VMEM((2, tq, LANES), F32),
        ],
        compiler_params=pltpu.CompilerParams(
            dimension_semantics=("arbitrary", "arbitrary", "arbitrary"),
            vmem_limit_bytes=VMEM_LIMIT_BYTES),
        name="fox",
    )(q, k, v, frow, fcol)


def _tail_kernel(x_ref, ys_ref, yf_ref, kx_ref, vx_ref,
                 fox_g_ref, w_out_ref, mix_post_ref, xa_pre_ref, wq_ref, wo_ref, xa_post_ref,
                 ffn_pre_ref, wg_ref, wu_ref, wd_ref, ffn_post_ref, o_ref):
    x = x_ref[...]
    yf = _rms(yf_ref[...].astype(F32), fox_g_ref[...]).astype(BF16)
    mix = _dot(ys_ref[...], w_out_ref[:SSM_WIDTH, :]) + _dot(yf, w_out_ref[SSM_WIDTH:, :])
    x = x + _rms(mix, mix_post_ref[...])

    h = _rms(x, xa_pre_ref[...]).astype(BF16)
    q = _dot(h, wq_ref[...]).astype(BF16)
    heads = []
    for hd in range(XA_HEADS):
        sl = slice(hd * XA_HEAD_DIM, (hd + 1) * XA_HEAD_DIM)
        s = _dot_nt(q[:, sl], kx_ref[:, sl])
        p = jnp.exp(s - jnp.max(s, axis=-1, keepdims=True))
        p = p / jnp.sum(p, axis=-1, keepdims=True)
        heads.append(_dot(p.astype(BF16), vx_ref[:, sl]).astype(BF16))
    o = jnp.concatenate(heads, axis=1)
    x = x + _rms(_dot(o, wo_ref[...]), xa_post_ref[...])

    h = _rms(x, ffn_pre_ref[...]).astype(BF16)
    gate = _dot(h, wg_ref[...])
    up = _dot(h, wu_ref[...])
    act = (gate * _sigmoid(gate) * up).astype(BF16)
    x = x + _rms(_dot(act, wd_ref[...]), ffn_post_ref[...])
    o_ref[...] = x


def _tail(x, ys, yf, kx, vx, fox_g, w_out, mix_post_g, xa_pre_g, wq, wo, xa_post_g,
          ffn_pre_g, wg, wu, wd, ffn_post_g, tm):
    b, s, d = x.shape
    m = kx.shape[1]
    row = lambda i, j: (i, j, 0)
    per_b = lambda i, j: (i, 0, 0)
    const = lambda i, j: (0, 0)

    def resident(a):
        return pl.BlockSpec(a.shape, const, pipeline_mode=pl.Buffered(1))

    gains = [g.reshape(1, -1) for g in (fox_g, mix_post_g, xa_pre_g, xa_post_g, ffn_pre_g, ffn_post_g)]
    fox_g, mix_post_g, xa_pre_g, xa_post_g, ffn_pre_g, ffn_post_g = gains
    args = (x, ys, yf, kx, vx, fox_g, w_out, mix_post_g, xa_pre_g, wq, wo, xa_post_g,
            ffn_pre_g, wg, wu, wd, ffn_post_g)
    in_specs = [
        pl.BlockSpec((None, tm, d), row),
        pl.BlockSpec((None, tm, SSM_WIDTH), row),
        pl.BlockSpec((None, tm, FOX_WIDTH), row),
        pl.BlockSpec((None, m, d), per_b),
        pl.BlockSpec((None, m, d), per_b),
    ] + [resident(a) for a in args[5:]]
    return pl.pallas_call(
        _tail_kernel,
        grid=(b, s // tm),
        in_specs=in_specs,
        out_specs=pl.BlockSpec((None, tm, d), row),
        out_shape=jax.ShapeDtypeStruct((b, s, d), F32),
        compiler_params=pltpu.CompilerParams(
            dimension_semantics=("arbitrary", "arbitrary"), vmem_limit_bytes=VMEM_LIMIT_BYTES),
        name="tail",
    )(*args)


def _pick(n, pref):
    t = min(n, pref)
    assert n % t == 0, (n, t)
    return t


def kernel(x, mem, mix_pre_g, w_in, ssm_a_re, ssm_a_im, ssm_log_dt, ssm_b_re, ssm_b_im, ssm_c_re, ssm_c_im, ssm_d, ssm_glu_w, ssm_glu_b, fox_f_bias, ssm_out_g, fox_out_g, w_out, mix_post_g, xa_pre_g, mem_g, xa_wq, xa_wkv, xa_wo, xa_post_g, ffn_pre_g, w_gate, w_up, w_down, ffn_post_g):
    b, s, d = x.shape
    assert d == D_MODEL and s % LANES == 0
    n_main = SSM_WIDTH + 3 * FOX_WIDTH

    kx, vx = _memkv(mem, mem_g, xa_wkv.astype(BF16))

    u, q, k, v, fcum = _inproj(
        x, mix_pre_g, w_in[:, :n_main].astype(BF16), w_in[:, n_main:].T.astype(BF16),
        fox_f_bias, tm=_pick(s, 512))

    bblk, cblk, are, aim = _ssm_params(ssm_a_re, ssm_a_im, ssm_log_dt, ssm_b_re, ssm_b_im,
                                       ssm_c_re, ssm_c_im)
    y_ssm = _ssm(u, bblk, cblk, are, aim, ssm_d, ssm_glu_w.astype(BF16), ssm_glu_b, ssm_out_g,
                 tm=_pick(s, 128))

    frow = fcum.reshape(b, FOX_PAIRS, 2, s)
    fcol = jnp.swapaxes(frow, 2, 3)
    y_fox = _fox(q, k, v, frow, fcol, tq=_pick(s, 256))

    return _tail(x, y_ssm, y_fox, kx, vx, fox_out_g, w_out.astype(BF16), mix_post_g, xa_pre_g,
                 xa_wq.astype(BF16), xa_wo.astype(BF16), xa_post_g, ffn_pre_g,
                 w_gate.astype(BF16), w_up.astype(BF16), w_down.astype(BF16), ffn_post_g,
                 tm=_pick(s, 256))
```

```python
import functools
import math

import jax
import jax.numpy as jnp
from jax import lax
from jax.experimental import pallas as pl
from jax.experimental.pallas import tpu as pltpu

F32 = jnp.float32
BF16 = jnp.bfloat16

LANES = 128
SUBLANES = 8
VMEM_LIMIT_BYTES = 56 * 1024 * 1024

D_MODEL = 1024
SSM_WIDTH = 512
SSM_GROUP_CH = 16
SSM_GROUPS = 32
SSM_STATE = 64
FOX_WIDTH = 512
FOX_HEAD_DIM = 64
FOX_HEADS = 8
FOX_PAIRS = FOX_HEADS * FOX_HEAD_DIM // LANES
XA_HEADS = 4
XA_HEAD_DIM = 256
RMS_EPS = 1e-6
NEG_INF = -1e30

STATE_TILES = 2 * SSM_GROUPS * SSM_STATE // LANES
RE_TILES = STATE_TILES // 2
U_TILES = SSM_WIDTH // LANES


def _rms(x, g):
    return x * lax.rsqrt(jnp.mean(x * x, axis=-1, keepdims=True) + RMS_EPS) * g


def _dot(a, b):
    return jnp.dot(a, b, preferred_element_type=F32)


def _dot_nt(a, b):
    return lax.dot_general(a, b, (((1,), (1,)), ((), ())), preferred_element_type=F32)


def _sigmoid(z):
    return 1.0 / (1.0 + jnp.exp(-z))


def _memkv_kernel(mem_ref, g_ref, w_ref, k_ref, v_ref):
    mn = _rms(mem_ref[...], g_ref[...]).astype(BF16)
    kv = _dot(mn, w_ref[...])
    k_ref[...] = (kv[:, :D_MODEL] * (1.0 / math.sqrt(XA_HEAD_DIM))).astype(BF16)
    v_ref[...] = kv[:, D_MODEL:].astype(BF16)


def _memkv(mem, mem_g, wkv):
    b, m, d = mem.shape
    return pl.pallas_call(
        _memkv_kernel,
        grid=(b,),
        in_specs=[
            pl.BlockSpec((None, m, d), lambda i: (i, 0, 0)),
            pl.BlockSpec((1, d), lambda i: (0, 0)),
            pl.BlockSpec((d, 2 * d), lambda i: (0, 0)),
        ],
        out_specs=[
            pl.BlockSpec((None, m, d), lambda i: (i, 0, 0)),
            pl.BlockSpec((None, m, d), lambda i: (i, 0, 0)),
        ],
        out_shape=[jax.ShapeDtypeStruct((b, m, d), BF16)] * 2,
        compiler_params=pltpu.CompilerParams(
            dimension_semantics=("arbitrary",), vmem_limit_bytes=VMEM_LIMIT_BYTES),
        name="memkv",
    )(mem, mem_g.reshape(1, d), wkv)


def _inproj_kernel(x_ref, g_ref, w_ref, wf_ref, fb_ref, tri_ref,
                   u_ref, q_ref, k_ref, v_ref, f_ref, carry_ref):
    @pl.when(pl.program_id(1) == 0)
    def _():
        carry_ref[...] = jnp.zeros_like(carry_ref)

    hb = _rms(x_ref[...], g_ref[...]).astype(BF16)
    proj = _dot(hb, w_ref[...])
    u_ref[...] = proj[:, :SSM_WIDTH].astype(BF16)
    o = SSM_WIDTH
    q_ref[...] = (proj[:, o:o + FOX_WIDTH] * (1.0 / math.sqrt(FOX_HEAD_DIM))).astype(BF16)
    k_ref[...] = proj[:, o + FOX_WIDTH:o + 2 * FOX_WIDTH].astype(BF16)
    v_ref[...] = proj[:, o + 2 * FOX_WIDTH:].astype(BF16)

    z = _dot_nt(wf_ref[...], hb) + fb_ref[...]
    logf = jnp.minimum(z, 0.0) - jnp.log1p(jnp.exp(-jnp.abs(z)))
    p1 = logf.astype(BF16)
    r1 = logf - p1.astype(F32)
    p2 = r1.astype(BF16)
    p3 = (r1 - p2.astype(F32)).astype(BF16)
    tri = tri_ref[...]
    csum = _dot(p1, tri) + _dot(p2, tri) + _dot(p3, tri)
    f = csum + carry_ref[...][:, :1]
    f_ref[...] = f
    carry_ref[...] = jnp.broadcast_to(f[:, -1:], carry_ref.shape)


def _inproj(x, mix_pre_g, w_main, wf_t, f_bias, tm):
    b, s, d = x.shape
    nt = s // tm
    tri = jnp.triu(jnp.ones((tm, tm), F32)).astype(BF16)
    row = lambda i, j: (i, j, 0)
    const2 = lambda i, j: (0, 0)
    return pl.pallas_call(
        _inproj_kernel,
        grid=(b, nt),
        in_specs=[
            pl.BlockSpec((None, tm, d), row),
            pl.BlockSpec((1, d), const2),
            pl.BlockSpec(w_main.shape, const2),
            pl.BlockSpec(wf_t.shape, const2),
            pl.BlockSpec((FOX_HEADS, 1), const2),
            pl.BlockSpec((tm, tm), const2),
        ],
        out_specs=[
            pl.BlockSpec((None, tm, SSM_WIDTH), row),
            pl.BlockSpec((None, tm, FOX_WIDTH), row),
            pl.BlockSpec((None, tm, FOX_WIDTH), row),
            pl.BlockSpec((None, tm, FOX_WIDTH), row),
            pl.BlockSpec((None, FOX_HEADS, tm), lambda i, j: (i, 0, j)),
        ],
        out_shape=[
            jax.ShapeDtypeStruct((b, s, SSM_WIDTH), BF16),
            jax.ShapeDtypeStruct((b, s, FOX_WIDTH), BF16),
            jax.ShapeDtypeStruct((b, s, FOX_WIDTH), BF16),
            jax.ShapeDtypeStruct((b, s, FOX_WIDTH), BF16),
            jax.ShapeDtypeStruct((b, FOX_HEADS, s), F32),
        ],
        scratch_shapes=[pltpu.VMEM((FOX_HEADS, LANES), F32)],
        compiler_params=pltpu.CompilerParams(
            dimension_semantics=("arbitrary", "arbitrary"), vmem_limit_bytes=VMEM_LIMIT_BYTES),
        name="inproj",
    )(x, mix_pre_g.reshape(1, d), w_main, wf_t, f_bias.reshape(FOX_HEADS, 1), tri)


def _gelu_tanh(x):
    c = math.sqrt(2.0 / math.pi)
    return 0.5 * x * (1.0 + jnp.tanh(c * (x + 0.044715 * (x * x * x))))


def _ssm_kernel(u_ref, bblk_ref, cblk_ref, are_ref, aim_ref, d_ref, gw_ref, gb_ref, g_ref,
                o_ref, s_ref, xc_ref, *, nb, tm, pitch):
    @pl.when(pl.program_id(0) == 0)
    def _():
        xc_ref[...] = jnp.zeros_like(xc_ref)

    def drive(b, _):
        for ut in range(U_TILES):
            ub = u_ref[b, :, ut * LANES:(ut + 1) * LANES]
            res = _dot(ub, bblk_ref[ut])
            for j in range(2 * U_TILES):
                tile = U_TILES * ut + j if j < U_TILES else RE_TILES + U_TILES * ut + (j - U_TILES)
                r0 = pl.multiple_of((b * STATE_TILES + tile) * pitch, SUBLANES)
                s_ref[pl.ds(r0, tm), :] = res[:, j * LANES:(j + 1) * LANES]
        return 0
    lax.fori_loop(0, nb, drive, 0)

    ar = are_ref[...]
    ai = aim_ref[...]

    def step(t, carry):
        new = []
        for b in range(nb):
            xr, xi = carry[2 * b], carry[2 * b + 1]
            ire = pl.ds(b * STATE_TILES * pitch + t, RE_TILES, stride=pitch)
            iim = pl.ds((b * STATE_TILES + RE_TILES) * pitch + t, RE_TILES, stride=pitch)
            nr = ar * xr - ai * xi + s_ref[ire, :]
            ni = ar * xi + ai * xr + s_ref[iim, :]
            s_ref[ire, :] = nr
            s_ref[iim, :] = ni
            new += [nr, ni]
        return tuple(new)

    init = []
    for b in range(nb):
        init.append(xc_ref[b * STATE_TILES:b * STATE_TILES + RE_TILES, :])
        init.append(xc_ref[b * STATE_TILES + RE_TILES:(b + 1) * STATE_TILES, :])
    fin = lax.fori_loop(0, tm, step, tuple(init))
    for b in range(nb):
        xc_ref[b * STATE_TILES:b * STATE_TILES + RE_TILES, :] = fin[2 * b]
        xc_ref[b * STATE_TILES + RE_TILES:(b + 1) * STATE_TILES, :] = fin[2 * b + 1]

    def readout(b, _):
        ys = []
        for ot in range(U_TILES):
            tiles = []
            for j in range(2 * U_TILES):
                tile = U_TILES * ot + j if j < U_TILES else RE_TILES + U_TILES * ot + (j - U_TILES)
                r0 = pl.multiple_of((b * STATE_TILES + tile) * pitch, SUBLANES)
                tiles.append(s_ref[pl.ds(r0, tm), :].astype(BF16))
            xs = jnp.concatenate(tiles, axis=1)
            ys.append(_dot(xs, cblk_ref[ot]))
        y = jnp.concatenate(ys, axis=1) + d_ref[...] * u_ref[b].astype(F32)
        g = _gelu_tanh(y)
        out = g * _sigmoid(_dot(g.astype(BF16), gw_ref[...]) + gb_ref[...])
        o_ref[b] = _rms(out, g_ref[...]).astype(BF16)
        return 0
    lax.fori_loop(0, nb, readout, 0)


def _ssm(u, bblk, cblk, a_re, a_im, d_skip, glu_w, glu_b, out_g, tm):
    b, s, w = u.shape
    pitch = tm + SUBLANES
    kern = functools.partial(_ssm_kernel, nb=b, tm=tm, pitch=pitch)
    c2 = lambda t: (0, 0)
    c3 = lambda t: (0, 0, 0)
    return pl.pallas_call(
        kern,
        grid=(s // tm,),
        in_specs=[
            pl.BlockSpec((b, tm, w), lambda t: (0, t, 0)),
            pl.BlockSpec(bblk.shape, c3),
            pl.BlockSpec(cblk.shape, c3),
            pl.BlockSpec(a_re.shape, c2),
            pl.BlockSpec(a_im.shape, c2),
            pl.BlockSpec((1, w), c2),
            pl.BlockSpec(glu_w.shape, c2),
            pl.BlockSpec((1, w), c2),
            pl.BlockSpec((1, w), c2),
        ],
        out_specs=pl.BlockSpec((b, tm, w), lambda t: (0, t, 0)),
        out_shape=jax.ShapeDtypeStruct((b, s, w), BF16),
        scratch_shapes=[
            pltpu.VMEM((b * STATE_TILES * pitch, LANES), F32),
            pltpu.VMEM((b * STATE_TILES, LANES), F32),
        ],
        compiler_params=pltpu.CompilerParams(
            dimension_semantics=("arbitrary",), vmem_limit_bytes=VMEM_LIMIT_BYTES),
        name="ssm",
    )(u, bblk, cblk, a_re, a_im, d_skip.reshape(1, w), glu_w, glu_b.reshape(1, w),
      out_g.reshape(1, w))


def _ssm_params(a_re, a_im, log_dt, b_re, b_im, c_re, c_im):
    a = lax.complex(a_re.astype(F32), a_im.astype(F32))
    dt = jnp.exp(log_dt.astype(F32))[:, None]
    a_bar = jnp.exp(a * dt)
    b_bar = ((a_bar - 1.0) / a)[..., None] * lax.complex(b_re.astype(F32), b_im.astype(F32))
    gpt = LANES // SSM_GROUP_CH
    eye = jnp.eye(gpt, dtype=F32)

    def bmat(part):
        p = part.reshape(U_TILES, gpt, SSM_STATE, SSM_GROUP_CH)
        m = jnp.einsum('ugni,gh->ugihn', p, eye)
        return m.reshape(U_TILES, LANES, gpt * SSM_STATE)

    def cmat(part):
        p = part.reshape(U_TILES, gpt, SSM_GROUP_CH, SSM_STATE)
        m = jnp.einsum('ugin,gh->ugnhi', p, eye)
        return m.reshape(U_TILES, gpt * SSM_STATE, LANES)

    bblk = jnp.concatenate([bmat(jnp.real(b_bar)), bmat(jnp.imag(b_bar))], axis=2).astype(BF16)
    cblk = jnp.concatenate([cmat(c_re.astype(F32)), cmat(-c_im.astype(F32))], axis=1).astype(BF16)
    are = jnp.real(a_bar).reshape(RE_TILES, LANES)
    aim = jnp.imag(a_bar).reshape(RE_TILES, LANES)
    return bblk, cblk, are, aim


BIAS_LANES = 8
KAUG_CHUNK = 512


def _split3(f):
    p1 = f.astype(BF16).astype(F32)
    r = f - p1
    p2 = r.astype(BF16).astype(F32)
    p3 = (r - p2).astype(BF16).astype(F32)
    return p1, p2, p3


def _bias_cols(f, base, key_side):
    rows = f.shape[0]
    lane = lax.broadcasted_iota(jnp.int32, (rows, LANES), 1)
    pieces = _split3(jnp.broadcast_to(f, (rows, LANES)))
    one = jnp.ones((rows, LANES), F32)
    vals = (one, one, one) + tuple(-p for p in pieces) if key_side else pieces + (one, one, one)
    out = jnp.zeros((rows, LANES), F32)
    for c, val in enumerate(vals):
        out = jnp.where(lane == base + c, val, out)
    return out


def _fox_kernel(q_ref, k_ref, v_ref, fcol_ref, o_ref, kaug_ref, m_ref, acc_ref, *, tq, tk):
    i = pl.program_id(2)
    seq = k_ref.shape[0]

    @pl.when(i == 0)
    def _():
        def chunk(c, _):
            r0 = pl.multiple_of(c * KAUG_CHUNK, KAUG_CHUNK)
            f = fcol_ref[pl.ds(r0, KAUG_CHUNK), :]
            cols = _bias_cols(f[:, 0:1], 0, True) + _bias_cols(f[:, 1:2], BIAS_LANES, True)
            kaug_ref[pl.ds(r0, KAUG_CHUNK), :] = cols.astype(BF16)
            return 0
        lax.fori_loop(0, seq // KAUG_CHUNK, chunk, 0)

    q = q_ref[...]
    lane = lax.broadcasted_iota(jnp.int32, q.shape, 1)
    first = lane < FOX_HEAD_DIM
    zero = jnp.zeros_like(q)
    q0 = pl.multiple_of(i * tq, tq)
    fq = fcol_ref[pl.ds(q0, tq), :]
    lhs = (
        jnp.concatenate([jnp.where(first, q, zero),
                         _bias_cols(fq[:, 0:1], 0, False).astype(BF16)], axis=1),
        jnp.concatenate([jnp.where(first, zero, q),
                         _bias_cols(fq[:, 1:2], BIAS_LANES, False).astype(BF16)], axis=1),
    )

    m_ref[...] = jnp.full_like(m_ref, NEG_INF)
    acc_ref[...] = jnp.zeros_like(acc_ref)
    ones = jnp.ones((tk, LANES), BF16)

    def block(j, diag):
        k0 = pl.multiple_of(j * tk, tk)
        rk = jnp.concatenate([k_ref[pl.ds(k0, tk), :], kaug_ref[pl.ds(k0, tk), :]], axis=1)
        rv = jnp.concatenate([v_ref[pl.ds(k0, tk), :], ones], axis=1)
        for h in range(2):
            s = _dot_nt(lhs[h], rk)
            if diag is not None:
                r = lax.broadcasted_iota(jnp.int32, s.shape, 0)
                c = lax.broadcasted_iota(jnp.int32, s.shape, 1)
                s = jnp.where(r >= c + diag * tk, s, NEG_INF)
            m_old = m_ref[h]
            m_new = jnp.maximum(m_old, jnp.max(s, axis=-1, keepdims=True))
            alpha = jnp.exp(m_old - m_new)
            p = jnp.exp(s - jnp.concatenate([m_new] * (tk // LANES), axis=1))
            acc_ref[h] = jnp.concatenate([alpha, alpha], axis=1) * acc_ref[h] + _dot(p.astype(BF16), rv)
            m_ref[h] = m_new

    n_off = i * (tq // tk)

    def off_diag(j, _):
        block(j, None)
        return 0
    lax.fori_loop(0, n_off, off_diag, 0)
    for d in range(tq // tk):
        block(n_off + d, d)

    a0 = acc_ref[0]
    a1 = acc_ref[1]
    o_ref[...] = jnp.where(first, a0[:, :LANES] / a0[:, LANES:], a1[:, :LANES] / a1[:, LANES:]).astype(BF16)


def _fox(q, k, v, fcol, tq, tk):
    b, s, w = q.shape
    assert tq % tk == 0 and s % KAUG_CHUNK == 0
    kern = functools.partial(_fox_kernel, tq=tq, tk=tk)
    return pl.pallas_call(
        kern,
        grid=(b, FOX_PAIRS, s // tq),
        in_specs=[
            pl.BlockSpec((None, tq, LANES), lambda bi, p, i: (bi, i, p)),
            pl.BlockSpec((None, s, LANES), lambda bi, p, i: (bi, 0, p)),
            pl.BlockSpec((None, s, LANES), lambda bi, p, i: (bi, 0, p)),
            pl.BlockSpec((None, None, s, 2), lambda bi, p, i: (bi, p, 0, 0)),
        ],
        out_specs=pl.BlockSpec((None, tq, LANES), lambda bi, p, i: (bi, i, p)),
        out_shape=jax.ShapeDtypeStruct((b, s, w), BF16),
        scratch_shapes=[
            pltpu.VMEM((s, LANES), BF16),
            pltpu.VMEM((2, tq, LANES), F32),
            pltpu.VMEM((2, tq, 2 * LANES), F32),
        ],
        compiler_params=pltpu.CompilerParams(
            dimension_semantics=("arbitrary", "arbitrary", "arbitrary"),
            vmem_limit_bytes=VMEM_LIMIT_BYTES),
        name="fox",
    )(q, k, v, fcol)


def _tail_kernel(x_ref, ys_ref, yf_ref, kx_ref, vx_ref,
                 fox_g_ref, w_out_ref, mix_post_ref, xa_pre_ref, wq_ref, wo_ref, xa_post_ref,
                 ffn_pre_ref, wg_ref, wu_ref, wd_ref, ffn_post_ref, o_ref):
    x = x_ref[...]
    yf = _rms(yf_ref[...].astype(F32), fox_g_ref[...]).astype(BF16)
    mix = _dot(ys_ref[...], w_out_ref[:SSM_WIDTH, :]) + _dot(yf, w_out_ref[SSM_WIDTH:, :])
    x = x + _rms(mix, mix_post_ref[...])

    h = _rms(x, xa_pre_ref[...]).astype(BF16)
    q = _dot(h, wq_ref[...]).astype(BF16)
    heads = []
    for hd in range(XA_HEADS):
        sl = slice(hd * XA_HEAD_DIM, (hd + 1) * XA_HEAD_DIM)
        s = _dot_nt(q[:, sl], kx_ref[:, sl])
        p = jnp.exp(s - jnp.max(s, axis=-1, keepdims=True))
        p = p / jnp.sum(p, axis=-1, keepdims=True)
        heads.append(_dot(p.astype(BF16), vx_ref[:, sl]).astype(BF16))
    o = jnp.concatenate(heads, axis=1)
    x = x + _rms(_dot(o, wo_ref[...]), xa_post_ref[...])

    h = _rms(x, ffn_pre_ref[...]).astype(BF16)
    gate = _dot(h, wg_ref[...])
    up = _dot(h, wu_ref[...])
    act = (gate * _sigmoid(gate) * up).astype(BF16)
    x = x + _rms(_dot(act, wd_ref[...]), ffn_post_ref[...])
    o_ref[...] = x


def _tail(x, ys, yf, kx, vx, fox_g, w_out, mix_post_g, xa_pre_g, wq, wo, xa_post_g,
          ffn_pre_g, wg, wu, wd, ffn_post_g, tm):
    b, s, d = x.shape
    m = kx.shape[1]
    row = lambda i, j: (i, j, 0)
    per_b = lambda i, j: (i, 0, 0)
    const = lambda i, j: (0, 0)

    def resident(a):
        return pl.BlockSpec(a.shape, const, pipeline_mode=pl.Buffered(1))

    gains = [g.reshape(1, -1) for g in (fox_g, mix_post_g, xa_pre_g, xa_post_g, ffn_pre_g, ffn_post_g)]
    fox_g, mix_post_g, xa_pre_g, xa_post_g, ffn_pre_g, ffn_post_g = gains
    args = (x, ys, yf, kx, vx, fox_g, w_out, mix_post_g, xa_pre_g, wq, wo, xa_post_g,
            ffn_pre_g, wg, wu, wd, ffn_post_g)
    in_specs = [
        pl.BlockSpec((None, tm, d), row),
        pl.BlockSpec((None, tm, SSM_WIDTH), row),
        pl.BlockSpec((None, tm, FOX_WIDTH), row),
        pl.BlockSpec((None, m, d), per_b),
        pl.BlockSpec((None, m, d), per_b),
    ] + [resident(a) for a in args[5:]]
    return pl.pallas_call(
        _tail_kernel,
        grid=(b, s // tm),
        in_specs=in_specs,
        out_specs=pl.BlockSpec((None, tm, d), row),
        out_shape=jax.ShapeDtypeStruct((b, s, d), F32),
        compiler_params=pltpu.CompilerParams(
            dimension_semantics=("arbitrary", "arbitrary"), vmem_limit_bytes=VMEM_LIMIT_BYTES),
        name="tail",
    )(*args)


def _pick(n, pref):
    t = min(n, pref)
    assert n % t == 0, (n, t)
    return t


def kernel(x, mem, mix_pre_g, w_in, ssm_a_re, ssm_a_im, ssm_log_dt, ssm_b_re, ssm_b_im, ssm_c_re, ssm_c_im, ssm_d, ssm_glu_w, ssm_glu_b, fox_f_bias, ssm_out_g, fox_out_g, w_out, mix_post_g, xa_pre_g, mem_g, xa_wq, xa_wkv, xa_wo, xa_post_g, ffn_pre_g, w_gate, w_up, w_down, ffn_post_g):
    b, s, d = x.shape
    assert d == D_MODEL and s % LANES == 0
    n_main = SSM_WIDTH + 3 * FOX_WIDTH

    kx, vx = _memkv(mem, mem_g, xa_wkv.astype(BF16))

    u, q, k, v, fcum = _inproj(
        x, mix_pre_g, w_in[:, :n_main].astype(BF16), w_in[:, n_main:].T.astype(BF16),
        fox_f_bias, tm=_pick(s, 512))

    bblk, cblk, are, aim = _ssm_params(ssm_a_re, ssm_a_im, ssm_log_dt, ssm_b_re, ssm_b_im,
                                       ssm_c_re, ssm_c_im)
    y_ssm = _ssm(u, bblk, cblk, are, aim, ssm_d, ssm_glu_w.astype(BF16), ssm_glu_b, ssm_out_g,
                 tm=_pick(s, 128))

    fcol = jnp.swapaxes(fcum.reshape(b, FOX_PAIRS, 2, s), 2, 3)
    y_fox = _fox(q, k, v, fcol, tq=_pick(s, 512), tk=_pick(s, 512))

    return _tail(x, y_ssm, y_fox, kx, vx, fox_out_g, w_out.astype(BF16), mix_post_g, xa_pre_g,
                 xa_wq.astype(BF16), xa_wo.astype(BF16), xa_post_g, ffn_pre_g,
                 w_gate.astype(BF16), w_up.astype(BF16), w_down.astype(BF16), ffn_post_g,
                 tm=_pick(s, 256))
```

```python
import functools
import math

import jax
import jax.numpy as jnp
from jax import lax
from jax.experimental import pallas as pl
from jax.experimental.pallas import tpu as pltpu

F32 = jnp.float32
BF16 = jnp.bfloat16

LANES = 128
SUBLANES = 8
VMEM_LIMIT_BYTES = 56 * 1024 * 1024

D_MODEL = 1024
SSM_WIDTH = 512
SSM_GROUP_CH = 16
SSM_GROUPS = 32
SSM_STATE = 64
FOX_WIDTH = 512
FOX_HEAD_DIM = 64
FOX_HEADS = 8
FOX_PAIRS = FOX_HEADS * FOX_HEAD_DIM // LANES
XA_HEADS = 4
XA_HEAD_DIM = 256
RMS_EPS = 1e-6
NEG_INF = -1e30
LOG2E = math.log2(math.e)

STATE_TILES = 2 * SSM_GROUPS * SSM_STATE // LANES
RE_TILES = STATE_TILES // 2
U_TILES = SSM_WIDTH // LANES


def _rms(x, g):
    return x * lax.rsqrt(jnp.mean(x * x, axis=-1, keepdims=True) + RMS_EPS) * g


def _dot(a, b):
    return jnp.dot(a, b, preferred_element_type=F32)


def _dot_nt(a, b):
    return lax.dot_general(a, b, (((1,), (1,)), ((), ())), preferred_element_type=F32)


def _sigmoid(z):
    return 1.0 / (1.0 + jnp.exp(-z))


def _memkv_kernel(mem_ref, g_ref, w_ref, k_ref, v_ref):
    mn = _rms(mem_ref[...], g_ref[...]).astype(BF16)
    kv = _dot(mn, w_ref[...])
    k_ref[...] = (kv[:, :D_MODEL] * (1.0 / math.sqrt(XA_HEAD_DIM))).astype(BF16)
    v_ref[...] = kv[:, D_MODEL:].astype(BF16)


def _memkv(mem, mem_g, wkv):
    b, m, d = mem.shape
    return pl.pallas_call(
        _memkv_kernel,
        grid=(b,),
        in_specs=[
            pl.BlockSpec((None, m, d), lambda i: (i, 0, 0)),
            pl.BlockSpec((1, d), lambda i: (0, 0)),
            pl.BlockSpec((d, 2 * d), lambda i: (0, 0)),
        ],
        out_specs=[
            pl.BlockSpec((None, m, d), lambda i: (i, 0, 0)),
            pl.BlockSpec((None, m, d), lambda i: (i, 0, 0)),
        ],
        out_shape=[jax.ShapeDtypeStruct((b, m, d), BF16)] * 2,
        compiler_params=pltpu.CompilerParams(
            dimension_semantics=("arbitrary",), vmem_limit_bytes=VMEM_LIMIT_BYTES),
        name="memkv",
    )(mem, mem_g.reshape(1, d), wkv)


def _inproj_kernel(x_ref, g_ref, w_ref, wf_ref, fb_ref, tri_ref,
                   u_ref, q_ref, k_ref, v_ref, f_ref, carry_ref):
    @pl.when(pl.program_id(1) == 0)
    def _():
        carry_ref[...] = jnp.zeros_like(carry_ref)

    hb = _rms(x_ref[...], g_ref[...]).astype(BF16)
    proj = _dot(hb, w_ref[...])
    u_ref[...] = proj[:, :SSM_WIDTH].astype(BF16)
    o = SSM_WIDTH
    q_ref[...] = (proj[:, o:o + FOX_WIDTH] * (LOG2E / math.sqrt(FOX_HEAD_DIM))).astype(BF16)
    k_ref[...] = proj[:, o + FOX_WIDTH:o + 2 * FOX_WIDTH].astype(BF16)
    v_ref[...] = proj[:, o + 2 * FOX_WIDTH:].astype(BF16)

    z = _dot_nt(wf_ref[...], hb) + fb_ref[...]
    logf = jnp.minimum(z, 0.0) - jnp.log1p(jnp.exp(-jnp.abs(z)))
    p1 = logf.astype(BF16)
    r1 = logf - p1.astype(F32)
    p2 = r1.astype(BF16)
    p3 = (r1 - p2.astype(F32)).astype(BF16)
    tri = tri_ref[...]
    csum = _dot(p1, tri) + _dot(p2, tri) + _dot(p3, tri)
    f = csum + carry_ref[...][:, :1]
    f_ref[...] = f * LOG2E
    carry_ref[...] = jnp.broadcast_to(f[:, -1:], carry_ref.shape)


def _inproj(x, mix_pre_g, w_main, wf_t, f_bias, tm):
    b, s, d = x.shape
    nt = s // tm
    tri = jnp.triu(jnp.ones((tm, tm), F32)).astype(BF16)
    row = lambda i, j: (i, j, 0)
    const2 = lambda i, j: (0, 0)
    return pl.pallas_call(
        _inproj_kernel,
        grid=(b, nt),
        in_specs=[
            pl.BlockSpec((None, tm, d), row),
            pl.BlockSpec((1, d), const2),
            pl.BlockSpec(w_main.shape, const2),
            pl.BlockSpec(wf_t.shape, const2),
            pl.BlockSpec((FOX_HEADS, 1), const2),
            pl.BlockSpec((tm, tm), const2),
        ],
        out_specs=[
            pl.BlockSpec((None, tm, SSM_WIDTH), row),
            pl.BlockSpec((None, tm, FOX_WIDTH), row),
            pl.BlockSpec((None, tm, FOX_WIDTH), row),
            pl.BlockSpec((None, tm, FOX_WIDTH), row),
            pl.BlockSpec((None, FOX_HEADS, tm), lambda i, j: (i, 0, j)),
        ],
        out_shape=[
            jax.ShapeDtypeStruct((b, s, SSM_WIDTH), BF16),
            jax.ShapeDtypeStruct((b, s, FOX_WIDTH), BF16),
            jax.ShapeDtypeStruct((b, s, FOX_WIDTH), BF16),
            jax.ShapeDtypeStruct((b, s, FOX_WIDTH), BF16),
            jax.ShapeDtypeStruct((b, FOX_HEADS, s), F32),
        ],
        scratch_shapes=[pltpu.VMEM((FOX_HEADS, LANES), F32)],
        compiler_params=pltpu.CompilerParams(
            dimension_semantics=("arbitrary", "arbitrary"), vmem_limit_bytes=VMEM_LIMIT_BYTES),
        name="inproj",
    )(x, mix_pre_g.reshape(1, d), w_main, wf_t, f_bias.reshape(FOX_HEADS, 1), tri)


def _gelu_tanh(x):
    c = math.sqrt(2.0 / math.pi)
    return 0.5 * x * (1.0 + jnp.tanh(c * (x + 0.044715 * (x * x * x))))


def _ssm_kernel(u_ref, bblk_ref, cblk_ref, are_ref, aim_ref, d_ref, gw_ref, gb_ref, g_ref,
                o_ref, s_ref, xc_ref, *, nb, tm, pitch):
    @pl.when(pl.program_id(0) == 0)
    def _():
        xc_ref[...] = jnp.zeros_like(xc_ref)

    def drive(b, _):
        for ut in range(U_TILES):
            ub = u_ref[b, :, ut * LANES:(ut + 1) * LANES]
            res = _dot(ub, bblk_ref[ut])
            for j in range(2 * U_TILES):
                tile = U_TILES * ut + j if j < U_TILES else RE_TILES + U_TILES * ut + (j - U_TILES)
                r0 = pl.multiple_of((b * STATE_TILES + tile) * pitch, SUBLANES)
                s_ref[pl.ds(r0, tm), :] = res[:, j * LANES:(j + 1) * LANES]
        return 0
    lax.fori_loop(0, nb, drive, 0)

    ar = are_ref[...]
    ai = aim_ref[...]

    def step(t, carry):
        new = []
        for b in range(nb):
            xr, xi = carry[2 * b], carry[2 * b + 1]
            ire = pl.ds(b * STATE_TILES * pitch + t, RE_TILES, stride=pitch)
            iim = pl.ds((b * STATE_TILES + RE_TILES) * pitch + t, RE_TILES, stride=pitch)
            nr = ar * xr - ai * xi + s_ref[ire, :]
            ni = ar * xi + ai * xr + s_ref[iim, :]
            s_ref[ire, :] = nr
            s_ref[iim, :] = ni
            new += [nr, ni]
        return tuple(new)

    init = []
    for b in range(nb):
        init.append(xc_ref[b * STATE_TILES:b * STATE_TILES + RE_TILES, :])
        init.append(xc_ref[b * STATE_TILES + RE_TILES:(b + 1) * STATE_TILES, :])
    fin = lax.fori_loop(0, tm, step, tuple(init))
    for b in range(nb):
        xc_ref[b * STATE_TILES:b * STATE_TILES + RE_TILES, :] = fin[2 * b]
        xc_ref[b * STATE_TILES + RE_TILES:(b + 1) * STATE_TILES, :] = fin[2 * b + 1]

    def readout(b, _):
        ys = []
        for ot in range(U_TILES):
            tiles = []
            for j in range(2 * U_TILES):
                tile = U_TILES * ot + j if j < U_TILES else RE_TILES + U_TILES * ot + (j - U_TILES)
                r0 = pl.multiple_of((b * STATE_TILES + tile) * pitch, SUBLANES)
                tiles.append(s_ref[pl.ds(r0, tm), :].astype(BF16))
            xs = jnp.concatenate(tiles, axis=1)
            ys.append(_dot(xs, cblk_ref[ot]))
        y = jnp.concatenate(ys, axis=1) + d_ref[...] * u_ref[b].astype(F32)
        g = _gelu_tanh(y)
        out = g * _sigmoid(_dot(g.astype(BF16), gw_ref[...]) + gb_ref[...])
        o_ref[b] = _rms(out, g_ref[...]).astype(BF16)
        return 0
    lax.fori_loop(0, nb, readout, 0)


def _ssm(u, bblk, cblk, a_re, a_im, d_skip, glu_w, glu_b, out_g, tm):
    b, s, w = u.shape
    pitch = tm + SUBLANES
    kern = functools.partial(_ssm_kernel, nb=b, tm=tm, pitch=pitch)
    c2 = lambda t: (0, 0)
    c3 = lambda t: (0, 0, 0)
    return pl.pallas_call(
        kern,
        grid=(s // tm,),
        in_specs=[
            pl.BlockSpec((b, tm, w), lambda t: (0, t, 0)),
            pl.BlockSpec(bblk.shape, c3),
            pl.BlockSpec(cblk.shape, c3),
            pl.BlockSpec(a_re.shape, c2),
            pl.BlockSpec(a_im.shape, c2),
            pl.BlockSpec((1, w), c2),
            pl.BlockSpec(glu_w.shape, c2),
            pl.BlockSpec((1, w), c2),
            pl.BlockSpec((1, w), c2),
        ],
        out_specs=pl.BlockSpec((b, tm, w), lambda t: (0, t, 0)),
        out_shape=jax.ShapeDtypeStruct((b, s, w), BF16),
        scratch_shapes=[
            pltpu.VMEM((b * STATE_TILES * pitch, LANES), F32),
            pltpu.VMEM((b * STATE_TILES, LANES), F32),
        ],
        compiler_params=pltpu.CompilerParams(
            dimension_semantics=("arbitrary",), vmem_limit_bytes=VMEM_LIMIT_BYTES),
        name="ssm",
    )(u, bblk, cblk, a_re, a_im, d_skip.reshape(1, w), glu_w, glu_b.reshape(1, w),
      out_g.reshape(1, w))


def _ssm_params(a_re, a_im, log_dt, b_re, b_im, c_re, c_im):
    a = lax.complex(a_re.astype(F32), a_im.astype(F32))
    dt = jnp.exp(log_dt.astype(F32))[:, None]
    a_bar = jnp.exp(a * dt)
    b_bar = ((a_bar - 1.0) / a)[..., None] * lax.complex(b_re.astype(F32), b_im.astype(F32))
    gpt = LANES // SSM_GROUP_CH
    eye = jnp.eye(gpt, dtype=F32)

    def bmat(part):
        p = part.reshape(U_TILES, gpt, SSM_STATE, SSM_GROUP_CH)
        m = jnp.einsum('ugni,gh->ugihn', p, eye)
        return m.reshape(U_TILES, LANES, gpt * SSM_STATE)

    def cmat(part):
        p = part.reshape(U_TILES, gpt, SSM_GROUP_CH, SSM_STATE)
        m = jnp.einsum('ugin,gh->ugnhi', p, eye)
        return m.reshape(U_TILES, gpt * SSM_STATE, LANES)

    bblk = jnp.concatenate([bmat(jnp.real(b_bar)), bmat(jnp.imag(b_bar))], axis=2).astype(BF16)
    cblk = jnp.concatenate([cmat(c_re.astype(F32)), cmat(-c_im.astype(F32))], axis=1).astype(BF16)
    are = jnp.real(a_bar).reshape(RE_TILES, LANES)
    aim = jnp.imag(a_bar).reshape(RE_TILES, LANES)
    return bblk, cblk, are, aim


BIAS_LANES = 8
KAUG_CHUNK = 512


def _split3(f):
    p1 = f.astype(BF16).astype(F32)
    r = f - p1
    p2 = r.astype(BF16).astype(F32)
    p3 = (r - p2).astype(BF16).astype(F32)
    return p1, p2, p3


def _bias_cols(f, base, key_side):
    rows = f.shape[0]
    lane = lax.broadcasted_iota(jnp.int32, (rows, LANES), 1)
    pieces = _split3(jnp.broadcast_to(f, (rows, LANES)))
    one = jnp.ones((rows, LANES), F32)
    vals = (one, one, one) + tuple(-p for p in pieces) if key_side else pieces + (one, one, one)
    out = jnp.zeros((rows, LANES), F32)
    for c, val in enumerate(vals):
        out = jnp.where(lane == base + c, val, out)
    return out


def _fox_kernel(q_ref, k_ref, v_ref, fcol_ref, o_ref, kaug_ref, s_ref, m_ref, acc_ref, *, t):
    i = pl.program_id(2)
    nq = pl.num_programs(2)
    seq = k_ref.shape[0]
    lane = lax.broadcasted_iota(jnp.int32, (t, LANES), 1)
    first = lane < FOX_HEAD_DIM

    def make_lhs(qi):
        r0 = pl.multiple_of(qi * t, t)
        q = q_ref[pl.ds(r0, t), :]
        fq = fcol_ref[pl.ds(r0, t), :]
        zero = jnp.zeros_like(q)
        return (
            jnp.concatenate([jnp.where(first, q, zero),
                             _bias_cols(fq[:, 0:1], 0, False).astype(BF16)], axis=1),
            jnp.concatenate([jnp.where(first, zero, q),
                             _bias_cols(fq[:, 1:2], BIAS_LANES, False).astype(BF16)], axis=1),
        )

    def produce(lhs, j, causal):
        k0 = pl.multiple_of(j * t, t)
        rk = jnp.concatenate([k_ref[pl.ds(k0, t), :], kaug_ref[pl.ds(k0, t), :]], axis=1)
        for h in range(2):
            s = _dot_nt(lhs[h], rk)
            if causal:
                r = lax.broadcasted_iota(jnp.int32, s.shape, 0)
                c = lax.broadcasted_iota(jnp.int32, s.shape, 1)
                s = jnp.where(r >= c, s, NEG_INF)
            s_ref[h] = s

    def consume(j):
        k0 = pl.multiple_of(j * t, t)
        rv = jnp.concatenate([v_ref[pl.ds(k0, t), :], jnp.ones((t, LANES), BF16)], axis=1)
        for h in range(2):
            s = s_ref[h]
            m_old = m_ref[h]
            m_new = jnp.maximum(m_old, jnp.max(s, axis=-1, keepdims=True))
            alpha = jnp.exp2(m_old - m_new)
            p = jnp.exp2(s - jnp.concatenate([m_new] * (t // LANES), axis=1))
            acc_ref[h] = jnp.concatenate([alpha, alpha], axis=1) * acc_ref[h] + _dot(p.astype(BF16), rv)
            m_ref[h] = m_new

    @pl.when(i == 0)
    def _():
        def chunk(c, _):
            r0 = pl.multiple_of(c * KAUG_CHUNK, KAUG_CHUNK)
            f = fcol_ref[pl.ds(r0, KAUG_CHUNK), :]
            cols = _bias_cols(f[:, 0:1], 0, True) + _bias_cols(f[:, 1:2], BIAS_LANES, True)
            kaug_ref[pl.ds(r0, KAUG_CHUNK), :] = cols.astype(BF16)
            return 0
        lax.fori_loop(0, seq // KAUG_CHUNK, chunk, 0)
        produce(make_lhs(0), 0, True)

    m_ref[...] = jnp.full_like(m_ref, NEG_INF)
    acc_ref[...] = jnp.zeros_like(acc_ref)
    lhs = make_lhs(i)

    n_steady = jnp.maximum(i - 1, 0)

    def steady_pair(jj, _):
        j = 2 * jj
        consume(j)
        produce(lhs, j + 1, False)
        consume(j + 1)
        produce(lhs, j + 2, False)
        return 0
    lax.fori_loop(0, n_steady >> 1, steady_pair, 0)

    @pl.when((n_steady & 1) == 1)
    def _():
        consume(n_steady - 1)
        produce(lhs, n_steady, False)

    @pl.when(i > 0)
    def _():
        consume(i - 1)
        produce(lhs, i, True)

    consume(i)
    produce(make_lhs(jnp.minimum(i + 1, nq - 1)), 0, False)

    a0 = acc_ref[0]
    a1 = acc_ref[1]
    o_ref[...] = jnp.where(first, a0[:, :LANES] / a0[:, LANES:], a1[:, :LANES] / a1[:, LANES:]).astype(BF16)


def _fox(q, k, v, fcol, t):
    b, s, w = q.shape
    assert s % t == 0 and s % KAUG_CHUNK == 0
    kern = functools.partial(_fox_kernel, t=t)
    whole = lambda bi, p, i: (bi, 0, p)
    return pl.pallas_call(
        kern,
        grid=(b, FOX_PAIRS, s // t),
        in_specs=[
            pl.BlockSpec((None, s, LANES), whole),
            pl.BlockSpec((None, s, LANES), whole),
            pl.BlockSpec((None, s, LANES), whole),
            pl.BlockSpec((None, None, s, 2), lambda bi, p, i: (bi, p, 0, 0)),
        ],
        out_specs=pl.BlockSpec((None, t, LANES), lambda bi, p, i: (bi, i, p)),
        out_shape=jax.ShapeDtypeStruct((b, s, w), BF16),
        scratch_shapes=[
            pltpu.VMEM((s, LANES), BF16),
            pltpu.VMEM((2, t, t), F32),
            pltpu.VMEM((2, t, LANES), F32),
            pltpu.VMEM((2, t, 2 * LANES), F32),
        ],
        compiler_params=pltpu.CompilerParams(
            dimension_semantics=("arbitrary", "arbitrary", "arbitrary"),
            vmem_limit_bytes=VMEM_LIMIT_BYTES),
        name="fox",
    )(q, k, v, fcol)


def _tail_kernel(x_ref, ys_ref, yf_ref, kx_ref, vx_ref,
                 fox_g_ref, w_out_ref, mix_post_ref, xa_pre_ref, wq_ref, wo_ref, xa_post_ref,
                 ffn_pre_ref, wg_ref, wu_ref, wd_ref, ffn_post_ref, o_ref):
    x = x_ref[...]
    yf = _rms(yf_ref[...].astype(F32), fox_g_ref[...]).astype(BF16)
    mix = _dot(ys_ref[...], w_out_ref[:SSM_WIDTH, :]) + _dot(yf, w_out_ref[SSM_WIDTH:, :])
    x = x + _rms(mix, mix_post_ref[...])

    h = _rms(x, xa_pre_ref[...]).astype(BF16)
    q = _dot(h, wq_ref[...]).astype(BF16)
    heads = []
    for hd in range(XA_HEADS):
        sl = slice(hd * XA_HEAD_DIM, (hd + 1) * XA_HEAD_DIM)
        s = _dot_nt(q[:, sl], kx_ref[:, sl])
        p = jnp.exp(s - jnp.max(s, axis=-1, keepdims=True))
        p = p / jnp.sum(p, axis=-1, keepdims=True)
        heads.append(_dot(p.astype(BF16), vx_ref[:, sl]).astype(BF16))
    o = jnp.concatenate(heads, axis=1)
    x = x + _rms(_dot(o, wo_ref[...]), xa_post_ref[...])

    h = _rms(x, ffn_pre_ref[...]).astype(BF16)
    gate = _dot(h, wg_ref[...])
    up = _dot(h, wu_ref[...])
    act = (gate * _sigmoid(gate) * up).astype(BF16)
    x = x + _rms(_dot(act, wd_ref[...]), ffn_post_ref[...])
    o_ref[...] = x


def _tail(x, ys, yf, kx, vx, fox_g, w_out, mix_post_g, xa_pre_g, wq, wo, xa_post_g,
          ffn_pre_g, wg, wu, wd, ffn_post_g, tm):
    b, s, d = x.shape
    m = kx.shape[1]
    row = lambda i, j: (i, j, 0)
    per_b = lambda i, j: (i, 0, 0)
    const = lambda i, j: (0, 0)

    def resident(a):
        return pl.BlockSpec(a.shape, const, pipeline_mode=pl.Buffered(1))

    gains = [g.reshape(1, -1) for g in (fox_g, mix_post_g, xa_pre_g, xa_post_g, ffn_pre_g, ffn_post_g)]
    fox_g, mix_post_g, xa_pre_g, xa_post_g, ffn_pre_g, ffn_post_g = gains
    args = (x, ys, yf, kx, vx, fox_g, w_out, mix_post_g, xa_pre_g, wq, wo, xa_post_g,
            ffn_pre_g, wg, wu, wd, ffn_post_g)
    in_specs = [
        pl.BlockSpec((None, tm, d), row),
        pl.BlockSpec((None, tm, SSM_WIDTH), row),
        pl.BlockSpec((None, tm, FOX_WIDTH), row),
        pl.BlockSpec((None, m, d), per_b),
        pl.BlockSpec((None, m, d), per_b),
    ] + [resident(a) for a in args[5:]]
    return pl.pallas_call(
        _tail_kernel,
        grid=(b, s // tm),
        in_specs=in_specs,
        out_specs=pl.BlockSpec((None, tm, d), row),
        out_shape=jax.ShapeDtypeStruct((b, s, d), F32),
        compiler_params=pltpu.CompilerParams(
            dimension_semantics=("arbitrary", "arbitrary"), vmem_limit_bytes=VMEM_LIMIT_BYTES),
        name="tail",
    )(*args)


def _pick(n, pref):
    t = min(n, pref)
    assert n % t == 0, (n, t)
    return t


def kernel(x, mem, mix_pre_g, w_in, ssm_a_re, ssm_a_im, ssm_log_dt, ssm_b_re, ssm_b_im, ssm_c_re, ssm_c_im, ssm_d, ssm_glu_w, ssm_glu_b, fox_f_bias, ssm_out_g, fox_out_g, w_out, mix_post_g, xa_pre_g, mem_g, xa_wq, xa_wkv, xa_wo, xa_post_g, ffn_pre_g, w_gate, w_up, w_down, ffn_post_g):
    b, s, d = x.shape
    assert d == D_MODEL and s % LANES == 0
    n_main = SSM_WIDTH + 3 * FOX_WIDTH

    kx, vx = _memkv(mem, mem_g, xa_wkv.astype(BF16))

    u, q, k, v, fcum = _inproj(
        x, mix_pre_g, w_in[:, :n_main].astype(BF16), w_in[:, n_main:].T.astype(BF16),
        fox_f_bias, tm=_pick(s, 512))

    bblk, cblk, are, aim = _ssm_params(ssm_a_re, ssm_a_im, ssm_log_dt, ssm_b_re, ssm_b_im,
                                       ssm_c_re, ssm_c_im)
    y_ssm = _ssm(u, bblk, cblk, are, aim, ssm_d, ssm_glu_w.astype(BF16), ssm_glu_b, ssm_out_g,
                 tm=_pick(s, 128))

    fcol = jnp.swapaxes(fcum.reshape(b, FOX_PAIRS, 2, s), 2, 3)
    y_fox = _fox(q, k, v, fcol, t=_pick(s, 512))

    return _tail(x, y_ssm, y_fox, kx, vx, fox_out_g, w_out.astype(BF16), mix_post_g, xa_pre_g,
                 xa_wq.astype(BF16), xa_wo.astype(BF16), xa_post_g, ffn_pre_g,
                 w_gate.astype(BF16), w_up.astype(BF16), w_down.astype(BF16), ffn_post_g,
                 tm=_pick(s, 256))
```

```python
import functools
import math

import jax
import jax.numpy as jnp
from jax import lax
from jax.experimental import pallas as pl
from jax.experimental.pallas import tpu as pltpu

F32 = jnp.float32
BF16 = jnp.bfloat16

LANES = 128
SUBLANES = 8
VMEM_LIMIT_BYTES = 56 * 1024 * 1024

D_MODEL = 1024
SSM_WIDTH = 512
SSM_GROUP_CH = 16
SSM_GROUPS = 32
SSM_STATE = 64
FOX_WIDTH = 512
FOX_HEAD_DIM = 64
FOX_HEADS = 8
FOX_PAIRS = FOX_HEADS * FOX_HEAD_DIM // LANES
XA_HEADS = 4
XA_HEAD_DIM = 256
RMS_EPS = 1e-6
NEG_INF = -1e30
LOG2E = math.log2(math.e)

STATE_TILES = 2 * SSM_GROUPS * SSM_STATE // LANES
RE_TILES = STATE_TILES // 2
U_TILES = SSM_WIDTH // LANES


def _rms(x, g):
    return x * lax.rsqrt(jnp.mean(x * x, axis=-1, keepdims=True) + RMS_EPS) * g


def _dot(a, b):
    return jnp.dot(a, b, preferred_element_type=F32)


def _dot_nt(a, b):
    return lax.dot_general(a, b, (((1,), (1,)), ((), ())), preferred_element_type=F32)


def _sigmoid(z):
    return 1.0 / (1.0 + jnp.exp(-z))


def _memkv_kernel(mem_ref, g_ref, w_ref, k_ref, v_ref):
    mn = _rms(mem_ref[...], g_ref[...]).astype(BF16)
    kv = _dot(mn, w_ref[...])
    k_ref[...] = (kv[:, :D_MODEL] * (1.0 / math.sqrt(XA_HEAD_DIM))).astype(BF16)
    v_ref[...] = kv[:, D_MODEL:].astype(BF16)


def _memkv(mem, mem_g, wkv):
    b, m, d = mem.shape
    return pl.pallas_call(
        _memkv_kernel,
        grid=(b,),
        in_specs=[
            pl.BlockSpec((None, m, d), lambda i: (i, 0, 0)),
            pl.BlockSpec((1, d), lambda i: (0, 0)),
            pl.BlockSpec((d, 2 * d), lambda i: (0, 0)),
        ],
        out_specs=[
            pl.BlockSpec((None, m, d), lambda i: (i, 0, 0)),
            pl.BlockSpec((None, m, d), lambda i: (i, 0, 0)),
        ],
        out_shape=[jax.ShapeDtypeStruct((b, m, d), BF16)] * 2,
        compiler_params=pltpu.CompilerParams(
            dimension_semantics=("arbitrary",), vmem_limit_bytes=VMEM_LIMIT_BYTES),
        name="memkv",
    )(mem, mem_g.reshape(1, d), wkv)


def _inproj_kernel(x_ref, g_ref, w_ref, wf_ref, fb_ref, tri_ref,
                   u_ref, q_ref, k_ref, v_ref, f_ref, carry_ref):
    @pl.when(pl.program_id(1) == 0)
    def _():
        carry_ref[...] = jnp.zeros_like(carry_ref)

    hb = _rms(x_ref[...], g_ref[...]).astype(BF16)
    proj = _dot(hb, w_ref[...])
    u_ref[...] = proj[:, :SSM_WIDTH].astype(BF16)
    o = SSM_WIDTH
    q_ref[...] = (proj[:, o:o + FOX_WIDTH] * (LOG2E / math.sqrt(FOX_HEAD_DIM))).astype(BF16)
    k_ref[...] = proj[:, o + FOX_WIDTH:o + 2 * FOX_WIDTH].astype(BF16)
    v_ref[...] = proj[:, o + 2 * FOX_WIDTH:].astype(BF16)

    z = _dot_nt(wf_ref[...], hb) + fb_ref[...]
    logf = jnp.minimum(z, 0.0) - jnp.log1p(jnp.exp(-jnp.abs(z)))
    p1 = logf.astype(BF16)
    r1 = logf - p1.astype(F32)
    p2 = r1.astype(BF16)
    p3 = (r1 - p2.astype(F32)).astype(BF16)
    tri = tri_ref[...]
    csum = _dot(p1, tri) + _dot(p2, tri) + _dot(p3, tri)
    f = csum + carry_ref[...][:, :1]
    f_ref[...] = f * LOG2E
    carry_ref[...] = jnp.broadcast_to(f[:, -1:], carry_ref.shape)


def _inproj(x, mix_pre_g, w_main, wf_t, f_bias, tm):
    b, s, d = x.shape
    nt = s // tm
    tri = jnp.triu(jnp.ones((tm, tm), F32)).astype(BF16)
    row = lambda i, j: (i, j, 0)
    const2 = lambda i, j: (0, 0)
    return pl.pallas_call(
        _inproj_kernel,
        grid=(b, nt),
        in_specs=[
            pl.BlockSpec((None, tm, d), row),
            pl.BlockSpec((1, d), const2),
            pl.BlockSpec(w_main.shape, const2),
            pl.BlockSpec(wf_t.shape, const2),
            pl.BlockSpec((FOX_HEADS, 1), const2),
            pl.BlockSpec((tm, tm), const2),
        ],
        out_specs=[
            pl.BlockSpec((None, tm, SSM_WIDTH), row),
            pl.BlockSpec((None, tm, FOX_WIDTH), row),
            pl.BlockSpec((None, tm, FOX_WIDTH), row),
            pl.BlockSpec((None, tm, FOX_WIDTH), row),
            pl.BlockSpec((None, FOX_HEADS, tm), lambda i, j: (i, 0, j)),
        ],
        out_shape=[
            jax.ShapeDtypeStruct((b, s, SSM_WIDTH), BF16),
            jax.ShapeDtypeStruct((b, s, FOX_WIDTH), BF16),
            jax.ShapeDtypeStruct((b, s, FOX_WIDTH), BF16),
            jax.ShapeDtypeStruct((b, s, FOX_WIDTH), BF16),
            jax.ShapeDtypeStruct((b, FOX_HEADS, s), F32),
        ],
        scratch_shapes=[pltpu.VMEM((FOX_HEADS, LANES), F32)],
        compiler_params=pltpu.CompilerParams(
            dimension_semantics=("arbitrary", "arbitrary"), vmem_limit_bytes=VMEM_LIMIT_BYTES),
        name="inproj",
    )(x, mix_pre_g.reshape(1, d), w_main, wf_t, f_bias.reshape(FOX_HEADS, 1), tri)


def _gelu_tanh(x):
    c = math.sqrt(2.0 / math.pi)
    return 0.5 * x * (1.0 + jnp.tanh(c * (x + 0.044715 * (x * x * x))))


TILE_PITCH = 20
TIME_PITCH = 324


def _ssm_kernel(u_ref, bblk_ref, cblk_ref, are_ref, aim_ref, d_ref, gw_ref, gb_ref, g_ref,
                o_ref, s_ref, xc_ref, *, nb, tm):
    @pl.when(pl.program_id(0) == 0)
    def _():
        xc_ref[...] = jnp.zeros_like(xc_ref)

    def rows_of_tile(b, part, k):
        return pl.ds(TILE_PITCH * k + 2 * b + part, tm, stride=TIME_PITCH)

    for ut in range(U_TILES):
        ub = u_ref[:, :, ut * LANES:(ut + 1) * LANES].reshape(nb * tm, LANES)
        res = _dot(ub, bblk_ref[ut])
        for b in range(nb):
            for j in range(2 * U_TILES):
                part, k = divmod(j, U_TILES)
                s_ref[rows_of_tile(b, part, U_TILES * ut + k), :] = (
                    res[b * tm:(b + 1) * tm, j * LANES:(j + 1) * LANES])

    ar = are_ref[...]
    ai = aim_ref[...]

    def step(t, carry):
        new = []
        for b in range(nb):
            xr, xi = carry[2 * b], carry[2 * b + 1]
            ire = pl.ds(TIME_PITCH * t + 2 * b, RE_TILES, stride=TILE_PITCH)
            iim = pl.ds(TIME_PITCH * t + 2 * b + 1, RE_TILES, stride=TILE_PITCH)
            nr = ar * xr - ai * xi + s_ref[ire, :]
            ni = ar * xi + ai * xr + s_ref[iim, :]
            s_ref[ire, :] = nr
            s_ref[iim, :] = ni
            new += [nr, ni]
        return tuple(new)

    init = []
    for b in range(nb):
        init.append(xc_ref[b * STATE_TILES:b * STATE_TILES + RE_TILES, :])
        init.append(xc_ref[b * STATE_TILES + RE_TILES:(b + 1) * STATE_TILES, :])
    fin = lax.fori_loop(0, tm, step, tuple(init))
    for b in range(nb):
        xc_ref[b * STATE_TILES:b * STATE_TILES + RE_TILES, :] = fin[2 * b]
        xc_ref[b * STATE_TILES + RE_TILES:(b + 1) * STATE_TILES, :] = fin[2 * b + 1]

    ys = []
    for ot in range(U_TILES):
        per_seq = []
        for b in range(nb):
            tiles = [s_ref[rows_of_tile(b, part, U_TILES * ot + k), :].astype(BF16)
                     for part in range(2) for k in range(U_TILES)]
            per_seq.append(jnp.concatenate(tiles, axis=1))
        ys.append(_dot(jnp.concatenate(per_seq, axis=0), cblk_ref[ot]))
    u_all = u_ref[...].reshape(nb * tm, SSM_WIDTH).astype(F32)
    y = jnp.concatenate(ys, axis=1) + d_ref[...] * u_all
    g = _gelu_tanh(y)
    out = g * _sigmoid(_dot(g.astype(BF16), gw_ref[...]) + gb_ref[...])
    o_ref[...] = _rms(out, g_ref[...]).astype(BF16).reshape(nb, tm, SSM_WIDTH)


def _ssm(u, bblk, cblk, a_re, a_im, d_skip, glu_w, glu_b, out_g, tm):
    b, s, w = u.shape
    assert 2 * b <= TILE_PITCH - 4 and TIME_PITCH >= TILE_PITCH * (RE_TILES - 1) + 2 * b
    kern = functools.partial(_ssm_kernel, nb=b, tm=tm)
    c2 = lambda t: (0, 0)
    c3 = lambda t: (0, 0, 0)
    return pl.pallas_call(
        kern,
        grid=(s // tm,),
        in_specs=[
            pl.BlockSpec((b, tm, w), lambda t: (0, t, 0)),
            pl.BlockSpec(bblk.shape, c3),
            pl.BlockSpec(cblk.shape, c3),
            pl.BlockSpec(a_re.shape, c2),
            pl.BlockSpec(a_im.shape, c2),
            pl.BlockSpec((1, w), c2),
            pl.BlockSpec(glu_w.shape, c2),
            pl.BlockSpec((1, w), c2),
            pl.BlockSpec((1, w), c2),
        ],
        out_specs=pl.BlockSpec((b, tm, w), lambda t: (0, t, 0)),
        out_shape=jax.ShapeDtypeStruct((b, s, w), BF16),
        scratch_shapes=[
            pltpu.VMEM((tm * TIME_PITCH, LANES), F32),
            pltpu.VMEM((b * STATE_TILES, LANES), F32),
        ],
        compiler_params=pltpu.CompilerParams(
            dimension_semantics=("arbitrary",), vmem_limit_bytes=VMEM_LIMIT_BYTES),
        name="ssm",
    )(u, bblk, cblk, a_re, a_im, d_skip.reshape(1, w), glu_w, glu_b.reshape(1, w),
      out_g.reshape(1, w))


def _ssm_params(a_re, a_im, log_dt, b_re, b_im, c_re, c_im):
    a = lax.complex(a_re.astype(F32), a_im.astype(F32))
    dt = jnp.exp(log_dt.astype(F32))[:, None]
    a_bar = jnp.exp(a * dt)
    b_bar = ((a_bar - 1.0) / a)[..., None] * lax.complex(b_re.astype(F32), b_im.astype(F32))
    gpt = LANES // SSM_GROUP_CH
    eye = jnp.eye(gpt, dtype=F32)

    def bmat(part):
        p = part.reshape(U_TILES, gpt, SSM_STATE, SSM_GROUP_CH)
        m = jnp.einsum('ugni,gh->ugihn', p, eye)
        return m.reshape(U_TILES, LANES, gpt * SSM_STATE)

    def cmat(part):
        p = part.reshape(U_TILES, gpt, SSM_GROUP_CH, SSM_STATE)
        m = jnp.einsum('ugin,gh->ugnhi', p, eye)
        return m.reshape(U_TILES, gpt * SSM_STATE, LANES)

    bblk = jnp.concatenate([bmat(jnp.real(b_bar)), bmat(jnp.imag(b_bar))], axis=2).astype(BF16)
    cblk = jnp.concatenate([cmat(c_re.astype(F32)), cmat(-c_im.astype(F32))], axis=1).astype(BF16)
    are = jnp.real(a_bar).reshape(RE_TILES, LANES)
    aim = jnp.imag(a_bar).reshape(RE_TILES, LANES)
    return bblk, cblk, are, aim


BIAS_LANES = 8
KAUG_CHUNK = 512


def _split3(f):
    p1 = f.astype(BF16).astype(F32)
    r = f - p1
    p2 = r.astype(BF16).astype(F32)
    p3 = (r - p2).astype(BF16).astype(F32)
    return p1, p2, p3


def _bias_cols(f, base, key_side):
    rows = f.shape[0]
    lane = lax.broadcasted_iota(jnp.int32, (rows, LANES), 1)
    pieces = _split3(jnp.broadcast_to(f, (rows, LANES)))
    one = jnp.ones((rows, LANES), F32)
    vals = (one, one, one) + tuple(-p for p in pieces) if key_side else pieces + (one, one, one)
    out = jnp.zeros((rows, LANES), F32)
    for c, val in enumerate(vals):
        out = jnp.where(lane == base + c, val, out)
    return out


def _fox_kernel(q_ref, k_ref, v_ref, fcol_ref, o_ref, kaug_ref, s_ref, m_ref, acc_ref, *, t):
    i = pl.program_id(2)
    nq = pl.num_programs(2)
    seq = k_ref.shape[0]
    lane = lax.broadcasted_iota(jnp.int32, (t, LANES), 1)
    first = lane < FOX_HEAD_DIM

    def make_lhs(qi):
        r0 = pl.multiple_of(qi * t, t)
        q = q_ref[pl.ds(r0, t), :]
        fq = fcol_ref[pl.ds(r0, t), :]
        zero = jnp.zeros_like(q)
        return (
            jnp.concatenate([jnp.where(first, q, zero),
                             _bias_cols(fq[:, 0:1], 0, False).astype(BF16)], axis=1),
            jnp.concatenate([jnp.where(first, zero, q),
                             _bias_cols(fq[:, 1:2], BIAS_LANES, False).astype(BF16)], axis=1),
        )

    def produce(lhs, j, causal):
        k0 = pl.multiple_of(j * t, t)
        rk = jnp.concatenate([k_ref[pl.ds(k0, t), :], kaug_ref[pl.ds(k0, t), :]], axis=1)
        for h in range(2):
            s = _dot_nt(lhs[h], rk)
            if causal:
                r = lax.broadcasted_iota(jnp.int32, s.shape, 0)
                c = lax.broadcasted_iota(jnp.int32, s.shape, 1)
                s = jnp.where(r >= c, s, NEG_INF)
            s_ref[h] = s

    def consume(j):
        k0 = pl.multiple_of(j * t, t)
        rv = jnp.concatenate([v_ref[pl.ds(k0, t), :], jnp.ones((t, LANES), BF16)], axis=1)
        for h in range(2):
            s = s_ref[h]
            m_old = m_ref[h]
            m_new = jnp.maximum(m_old, jnp.max(s, axis=-1, keepdims=True))
            alpha = jnp.exp2(m_old - m_new)
            p = jnp.exp2(s - jnp.concatenate([m_new] * (t // LANES), axis=1))
            acc_ref[h] = jnp.concatenate([alpha, alpha], axis=1) * acc_ref[h] + _dot(p.astype(BF16), rv)
            m_ref[h] = m_new

    @pl.when(i == 0)
    def _():
        def chunk(c, _):
            r0 = pl.multiple_of(c * KAUG_CHUNK, KAUG_CHUNK)
            f = fcol_ref[pl.ds(r0, KAUG_CHUNK), :]
            cols = _bias_cols(f[:, 0:1], 0, True) + _bias_cols(f[:, 1:2], BIAS_LANES, True)
            kaug_ref[pl.ds(r0, KAUG_CHUNK), :] = cols.astype(BF16)
            return 0
        lax.fori_loop(0, seq // KAUG_CHUNK, chunk, 0)
        produce(make_lhs(0), 0, True)

    m_ref[...] = jnp.full_like(m_ref, NEG_INF)
    acc_ref[...] = jnp.zeros_like(acc_ref)
    lhs = make_lhs(i)

    n_steady = jnp.maximum(i - 1, 0)

    def steady_pair(jj, _):
        j = 2 * jj
        consume(j)
        produce(lhs, j + 1, False)
        consume(j + 1)
        produce(lhs, j + 2, False)
        return 0
    lax.fori_loop(0, n_steady >> 1, steady_pair, 0)

    @pl.when((n_steady & 1) == 1)
    def _():
        consume(n_steady - 1)
        produce(lhs, n_steady, False)

    @pl.when(i > 0)
    def _():
        consume(i - 1)
        produce(lhs, i, True)

    consume(i)
    produce(make_lhs(jnp.minimum(i + 1, nq - 1)), 0, False)

    a0 = acc_ref[0]
    a1 = acc_ref[1]
    o_ref[...] = jnp.where(first, a0[:, :LANES] / a0[:, LANES:], a1[:, :LANES] / a1[:, LANES:]).astype(BF16)


def _fox(q, k, v, fcol, t):
    b, s, w = q.shape
    assert s % t == 0 and s % KAUG_CHUNK == 0
    kern = functools.partial(_fox_kernel, t=t)
    whole = lambda bi, p, i: (bi, 0, p)
    return pl.pallas_call(
        kern,
        grid=(b, FOX_PAIRS, s // t),
        in_specs=[
            pl.BlockSpec((None, s, LANES), whole),
            pl.BlockSpec((None, s, LANES), whole),
            pl.BlockSpec((None, s, LANES), whole),
            pl.BlockSpec((None, None, s, 2), lambda bi, p, i: (bi, p, 0, 0)),
        ],
        out_specs=pl.BlockSpec((None, t, LANES), lambda bi, p, i: (bi, i, p)),
        out_shape=jax.ShapeDtypeStruct((b, s, w), BF16),
        scratch_shapes=[
            pltpu.VMEM((s, LANES), BF16),
            pltpu.VMEM((2, t, t), F32),
            pltpu.VMEM((2, t, LANES), F32),
            pltpu.VMEM((2, t, 2 * LANES), F32),
        ],
        compiler_params=pltpu.CompilerParams(
            dimension_semantics=("arbitrary", "arbitrary", "arbitrary"),
            vmem_limit_bytes=VMEM_LIMIT_BYTES),
        name="fox",
    )(q, k, v, fcol)


TAIL_SUB_ROWS = 256
FFN_CHUNK = 1024


def _tail_kernel(x_ref, ys_ref, yf_ref, kx_ref, vx_ref,
                 fox_g_ref, w_out_ref, mix_post_ref, xa_pre_ref, wq_ref, wo_ref, xa_post_ref,
                 ffn_pre_ref, wg_ref, wu_ref, wd_ref, ffn_post_ref, o_ref, *, n_sub):
    sub = x_ref.shape[0] // n_sub
    rows = [slice(r * sub, (r + 1) * sub) for r in range(n_sub)]
    each = lambda f, *lists: [f(*a) for a in zip(*lists)]

    x = [x_ref[r, :] for r in rows]
    yf = [_rms(yf_ref[r, :].astype(F32), fox_g_ref[...]).astype(BF16) for r in rows]
    mix = [_dot(ys_ref[r, :], w_out_ref[:SSM_WIDTH, :]) + _dot(f, w_out_ref[SSM_WIDTH:, :])
           for r, f in zip(rows, yf)]
    x = each(lambda xi, mi: xi + _rms(mi, mix_post_ref[...]), x, mix)

    h = [_rms(xi, xa_pre_ref[...]).astype(BF16) for xi in x]
    q = [_dot(hi, wq_ref[...]).astype(BF16) for hi in h]
    heads = [[] for _ in rows]
    for hd in range(XA_HEADS):
        sl = slice(hd * XA_HEAD_DIM, (hd + 1) * XA_HEAD_DIM)
        for qi, out in zip(q, heads):
            s = _dot_nt(qi[:, sl], kx_ref[:, sl])
            p = jnp.exp(s - jnp.max(s, axis=-1, keepdims=True))
            p = p / jnp.sum(p, axis=-1, keepdims=True)
            out.append(_dot(p.astype(BF16), vx_ref[:, sl]).astype(BF16))
    o = [jnp.concatenate(hs, axis=1) for hs in heads]
    x = each(lambda xi, oi: xi + _rms(_dot(oi, wo_ref[...]), xa_post_ref[...]), x, o)

    h = [_rms(xi, ffn_pre_ref[...]).astype(BF16) for xi in x]
    hidden = wg_ref.shape[1]
    down = [None] * n_sub
    for c0 in range(0, hidden, FFN_CHUNK):
        c1 = min(c0 + FFN_CHUNK, hidden)
        for r, hi in enumerate(h):
            gate = _dot(hi, wg_ref[:, c0:c1])
            up = _dot(hi, wu_ref[:, c0:c1])
            act = (gate * _sigmoid(gate) * up).astype(BF16)
            part = _dot(act, wd_ref[c0:c1, :])
            down[r] = part if down[r] is None else down[r] + part
    for r, xi, di in zip(rows, x, down):
        o_ref[r, :] = xi + _rms(di, ffn_post_ref[...])


def _tail(x, ys, yf, kx, vx, fox_g, w_out, mix_post_g, xa_pre_g, wq, wo, xa_post_g,
          ffn_pre_g, wg, wu, wd, ffn_post_g, tm):
    b, s, d = x.shape
    m = kx.shape[1]
    row = lambda i, j: (i, j, 0)
    per_b = lambda i, j: (i, 0, 0)
    const = lambda i, j: (0, 0)

    def resident(a):
        return pl.BlockSpec(a.shape, const, pipeline_mode=pl.Buffered(1))

    gains = [g.reshape(1, -1) for g in (fox_g, mix_post_g, xa_pre_g, xa_post_g, ffn_pre_g, ffn_post_g)]
    fox_g, mix_post_g, xa_pre_g, xa_post_g, ffn_pre_g, ffn_post_g = gains
    args = (x, ys, yf, kx, vx, fox_g, w_out, mix_post_g, xa_pre_g, wq, wo, xa_post_g,
            ffn_pre_g, wg, wu, wd, ffn_post_g)
    in_specs = [
        pl.BlockSpec((None, tm, d), row),
        pl.BlockSpec((None, tm, SSM_WIDTH), row),
        pl.BlockSpec((None, tm, FOX_WIDTH), row),
        pl.BlockSpec((None, m, d), per_b),
        pl.BlockSpec((None, m, d), per_b),
    ] + [resident(a) for a in args[5:]]
    return pl.pallas_call(
        functools.partial(_tail_kernel, n_sub=tm // TAIL_SUB_ROWS),
        grid=(b, s // tm),
        in_specs=in_specs,
        out_specs=pl.BlockSpec((None, tm, d), row),
        out_shape=jax.ShapeDtypeStruct((b, s, d), F32),
        compiler_params=pltpu.CompilerParams(
            dimension_semantics=("arbitrary", "arbitrary"), vmem_limit_bytes=VMEM_LIMIT_BYTES),
        name="tail",
    )(*args)


def _pick(n, pref):
    t = min(n, pref)
    assert n % t == 0, (n, t)
    return t


def kernel(x, mem, mix_pre_g, w_in, ssm_a_re, ssm_a_im, ssm_log_dt, ssm_b_re, ssm_b_im, ssm_c_re, ssm_c_im, ssm_d, ssm_glu_w, ssm_glu_b, fox_f_bias, ssm_out_g, fox_out_g, w_out, mix_post_g, xa_pre_g, mem_g, xa_wq, xa_wkv, xa_wo, xa_post_g, ffn_pre_g, w_gate, w_up, w_down, ffn_post_g):
    b, s, d = x.shape
    assert d == D_MODEL and s % LANES == 0
    n_main = SSM_WIDTH + 3 * FOX_WIDTH

    kx, vx = _memkv(mem, mem_g, xa_wkv.astype(BF16))

    u, q, k, v, fcum = _inproj(
        x, mix_pre_g, w_in[:, :n_main].astype(BF16), w_in[:, n_main:].T.astype(BF16),
        fox_f_bias, tm=_pick(s, 512))

    bblk, cblk, are, aim = _ssm_params(ssm_a_re, ssm_a_im, ssm_log_dt, ssm_b_re, ssm_b_im,
                                       ssm_c_re, ssm_c_im)
    y_ssm = _ssm(u, bblk, cblk, are, aim, ssm_d, ssm_glu_w.astype(BF16), ssm_glu_b, ssm_out_g,
                 tm=_pick(s, 128))

    fcol = jnp.swapaxes(fcum.reshape(b, FOX_PAIRS, 2, s), 2, 3)
    y_fox = _fox(q, k, v, fcol, t=_pick(s, 512))

    return _tail(x, y_ssm, y_fox, kx, vx, fox_out_g, w_out.astype(BF16), mix_post_g, xa_pre_g,
                 xa_wq.astype(BF16), xa_wo.astype(BF16), xa_post_g, ffn_pre_g,
                 w_gate.astype(BF16), w_up.astype(BF16), w_down.astype(BF16), ffn_post_g,
                 tm=_pick(s, 2 * TAIL_SUB_ROWS))
```

```python
import functools
import math

import jax
import jax.numpy as jnp
from jax import lax
from jax.experimental import pallas as pl
from jax.experimental.pallas import tpu as pltpu

F32 = jnp.float32
BF16 = jnp.bfloat16

LANES = 128
SUBLANES = 8
VMEM_LIMIT_BYTES = 56 * 1024 * 1024

D_MODEL = 1024
SSM_WIDTH = 512
SSM_GROUP_CH = 16
SSM_GROUPS = 32
SSM_STATE = 64
FOX_WIDTH = 512
FOX_HEAD_DIM = 64
FOX_HEADS = 8
FOX_PAIRS = FOX_HEADS * FOX_HEAD_DIM // LANES
XA_HEADS = 4
XA_HEAD_DIM = 256
RMS_EPS = 1e-6
NEG_INF = -1e30
LOG2E = math.log2(math.e)

STATE_TILES = 2 * SSM_GROUPS * SSM_STATE // LANES
RE_TILES = STATE_TILES // 2
U_TILES = SSM_WIDTH // LANES


def _rms(x, g):
    return x * lax.rsqrt(jnp.mean(x * x, axis=-1, keepdims=True) + RMS_EPS) * g


def _dot(a, b):
    return jnp.dot(a, b, preferred_element_type=F32)


def _dot_nt(a, b):
    return lax.dot_general(a, b, (((1,), (1,)), ((), ())), preferred_element_type=F32)


def _sigmoid(z):
    return 1.0 / (1.0 + jnp.exp(-z))


def _memkv_kernel(mem_ref, g_ref, w_ref, k_ref, v_ref):
    mn = _rms(mem_ref[...], g_ref[...]).astype(BF16)
    kv = _dot(mn, w_ref[...])
    k_ref[...] = (kv[:, :D_MODEL] * (1.0 / math.sqrt(XA_HEAD_DIM))).astype(BF16)
    v_ref[...] = kv[:, D_MODEL:].astype(BF16)


def _memkv(mem, mem_g, wkv):
    b, m, d = mem.shape
    return pl.pallas_call(
        _memkv_kernel,
        grid=(b,),
        in_specs=[
            pl.BlockSpec((None, m, d), lambda i: (i, 0, 0)),
            pl.BlockSpec((1, d), lambda i: (0, 0)),
            pl.BlockSpec((d, 2 * d), lambda i: (0, 0)),
        ],
        out_specs=[
            pl.BlockSpec((None, m, d), lambda i: (i, 0, 0)),
            pl.BlockSpec((None, m, d), lambda i: (i, 0, 0)),
        ],
        out_shape=[jax.ShapeDtypeStruct((b, m, d), BF16)] * 2,
        compiler_params=pltpu.CompilerParams(
            dimension_semantics=("arbitrary",), vmem_limit_bytes=VMEM_LIMIT_BYTES),
        name="memkv",
    )(mem, mem_g.reshape(1, d), wkv)


def _inproj_kernel(x_ref, g_ref, w_ref, wf_ref, fb_ref, tri_ref,
                   u_ref, q_ref, k_ref, v_ref, f_ref, carry_ref):
    @pl.when(pl.program_id(1) == 0)
    def _():
        carry_ref[...] = jnp.zeros_like(carry_ref)

    hb = _rms(x_ref[...], g_ref[...]).astype(BF16)
    proj = _dot(hb, w_ref[...])
    u_ref[...] = proj[:, :SSM_WIDTH].astype(BF16)
    o = SSM_WIDTH
    q_ref[...] = (proj[:, o:o + FOX_WIDTH] * (LOG2E / math.sqrt(FOX_HEAD_DIM))).astype(BF16)
    k_ref[...] = proj[:, o + FOX_WIDTH:o + 2 * FOX_WIDTH].astype(BF16)
    v_ref[...] = proj[:, o + 2 * FOX_WIDTH:].astype(BF16)

    z = _dot_nt(wf_ref[...], hb) + fb_ref[...]
    logf = jnp.minimum(z, 0.0) - jnp.log1p(jnp.exp(-jnp.abs(z)))
    p1 = logf.astype(BF16)
    r1 = logf - p1.astype(F32)
    p2 = r1.astype(BF16)
    p3 = (r1 - p2.astype(F32)).astype(BF16)
    tri = tri_ref[...]
    csum = _dot(p1, tri) + _dot(p2, tri) + _dot(p3, tri)
    f = csum + carry_ref[...][:, :1]
    f_ref[...] = f * LOG2E
    carry_ref[...] = jnp.broadcast_to(f[:, -1:], carry_ref.shape)


def _inproj(x, mix_pre_g, w_main, wf_t, f_bias, tm):
    b, s, d = x.shape
    nt = s // tm
    tri = jnp.triu(jnp.ones((tm, tm), F32)).astype(BF16)
    row = lambda i, j: (i, j, 0)
    const2 = lambda i, j: (0, 0)
    return pl.pallas_call(
        _inproj_kernel,
        grid=(b, nt),
        in_specs=[
            pl.BlockSpec((None, tm, d), row),
            pl.BlockSpec((1, d), const2),
            pl.BlockSpec(w_main.shape, const2),
            pl.BlockSpec(wf_t.shape, const2),
            pl.BlockSpec((FOX_HEADS, 1), const2),
            pl.BlockSpec((tm, tm), const2),
        ],
        out_specs=[
            pl.BlockSpec((None, tm, SSM_WIDTH), row),
            pl.BlockSpec((None, tm, FOX_WIDTH), row),
            pl.BlockSpec((None, tm, FOX_WIDTH), row),
            pl.BlockSpec((None, tm, FOX_WIDTH), row),
            pl.BlockSpec((None, FOX_HEADS, tm), lambda i, j: (i, 0, j)),
        ],
        out_shape=[
            jax.ShapeDtypeStruct((b, s, SSM_WIDTH), BF16),
            jax.ShapeDtypeStruct((b, s, FOX_WIDTH), BF16),
            jax.ShapeDtypeStruct((b, s, FOX_WIDTH), BF16),
            jax.ShapeDtypeStruct((b, s, FOX_WIDTH), BF16),
            jax.ShapeDtypeStruct((b, FOX_HEADS, s), F32),
        ],
        scratch_shapes=[pltpu.VMEM((FOX_HEADS, LANES), F32)],
        compiler_params=pltpu.CompilerParams(
            dimension_semantics=("arbitrary", "arbitrary"), vmem_limit_bytes=VMEM_LIMIT_BYTES),
        name="inproj",
    )(x, mix_pre_g.reshape(1, d), w_main, wf_t, f_bias.reshape(FOX_HEADS, 1), tri)


def _gelu_tanh(x):
    c = math.sqrt(2.0 / math.pi)
    return 0.5 * x * (1.0 + jnp.tanh(c * (x + 0.044715 * (x * x * x))))


TILE_PITCH = 20
TIME_PITCH = 324


def _ssm_kernel(u_ref, bblk_ref, cblk_ref, are_ref, aim_ref, d_ref, gw_ref, gb_ref, g_ref,
                o_ref, s_ref, xc_ref, *, nb, tm):
    @pl.when(pl.program_id(0) == 0)
    def _():
        xc_ref[...] = jnp.zeros_like(xc_ref)

    def rows_of_tile(b, part, k):
        return pl.ds(TILE_PITCH * k + 2 * b + part, tm, stride=TIME_PITCH)

    for ut in range(U_TILES):
        ub = u_ref[:, :, ut * LANES:(ut + 1) * LANES].reshape(nb * tm, LANES)
        res = _dot(ub, bblk_ref[ut])
        for b in range(nb):
            for j in range(2 * U_TILES):
                part, k = divmod(j, U_TILES)
                s_ref[rows_of_tile(b, part, U_TILES * ut + k), :] = (
                    res[b * tm:(b + 1) * tm, j * LANES:(j + 1) * LANES])

    ar = are_ref[...]
    ai = aim_ref[...]

    def step(t, carry):
        new = []
        for b in range(nb):
            xr, xi = carry[2 * b], carry[2 * b + 1]
            ire = pl.ds(TIME_PITCH * t + 2 * b, RE_TILES, stride=TILE_PITCH)
            iim = pl.ds(TIME_PITCH * t + 2 * b + 1, RE_TILES, stride=TILE_PITCH)
            nr = ar * xr - ai * xi + s_ref[ire, :]
            ni = ar * xi + ai * xr + s_ref[iim, :]
            s_ref[ire, :] = nr
            s_ref[iim, :] = ni
            new += [nr, ni]
        return tuple(new)

    init = []
    for b in range(nb):
        init.append(xc_ref[b * STATE_TILES:b * STATE_TILES + RE_TILES, :])
        init.append(xc_ref[b * STATE_TILES + RE_TILES:(b + 1) * STATE_TILES, :])
    fin = lax.fori_loop(0, tm, step, tuple(init))
    for b in range(nb):
        xc_ref[b * STATE_TILES:b * STATE_TILES + RE_TILES, :] = fin[2 * b]
        xc_ref[b * STATE_TILES + RE_TILES:(b + 1) * STATE_TILES, :] = fin[2 * b + 1]

    ys = []
    for ot in range(U_TILES):
        per_seq = []
        for b in range(nb):
            tiles = [s_ref[rows_of_tile(b, part, U_TILES * ot + k), :].astype(BF16)
                     for part in range(2) for k in range(U_TILES)]
            per_seq.append(jnp.concatenate(tiles, axis=1))
        ys.append(_dot(jnp.concatenate(per_seq, axis=0), cblk_ref[ot]))
    u_all = u_ref[...].reshape(nb * tm, SSM_WIDTH).astype(F32)
    y = jnp.concatenate(ys, axis=1) + d_ref[...] * u_all
    g = _gelu_tanh(y)
    out = g * _sigmoid(_dot(g.astype(BF16), gw_ref[...]) + gb_ref[...])
    o_ref[...] = _rms(out, g_ref[...]).astype(BF16).reshape(nb, tm, SSM_WIDTH)


def _ssm(u, bblk, cblk, a_re, a_im, d_skip, glu_w, glu_b, out_g, tm):
    b, s, w = u.shape
    assert 2 * b <= TILE_PITCH - 4 and TIME_PITCH >= TILE_PITCH * (RE_TILES - 1) + 2 * b
    kern = functools.partial(_ssm_kernel, nb=b, tm=tm)
    c2 = lambda t: (0, 0)
    c3 = lambda t: (0, 0, 0)
    return pl.pallas_call(
        kern,
        grid=(s // tm,),
        in_specs=[
            pl.BlockSpec((b, tm, w), lambda t: (0, t, 0)),
            pl.BlockSpec(bblk.shape, c3),
            pl.BlockSpec(cblk.shape, c3),
            pl.BlockSpec(a_re.shape, c2),
            pl.BlockSpec(a_im.shape, c2),
            pl.BlockSpec((1, w), c2),
            pl.BlockSpec(glu_w.shape, c2),
            pl.BlockSpec((1, w), c2),
            pl.BlockSpec((1, w), c2),
        ],
        out_specs=pl.BlockSpec((b, tm, w), lambda t: (0, t, 0)),
        out_shape=jax.ShapeDtypeStruct((b, s, w), BF16),
        scratch_shapes=[
            pltpu.VMEM((tm * TIME_PITCH, LANES), F32),
            pltpu.VMEM((b * STATE_TILES, LANES), F32),
        ],
        compiler_params=pltpu.CompilerParams(
            dimension_semantics=("arbitrary",), vmem_limit_bytes=VMEM_LIMIT_BYTES),
        name="ssm",
    )(u, bblk, cblk, a_re, a_im, d_skip.reshape(1, w), glu_w, glu_b.reshape(1, w),
      out_g.reshape(1, w))


def _ssm_params(a_re, a_im, log_dt, b_re, b_im, c_re, c_im):
    a = lax.complex(a_re.astype(F32), a_im.astype(F32))
    dt = jnp.exp(log_dt.astype(F32))[:, None]
    a_bar = jnp.exp(a * dt)
    b_bar = ((a_bar - 1.0) / a)[..., None] * lax.complex(b_re.astype(F32), b_im.astype(F32))
    gpt = LANES // SSM_GROUP_CH
    eye = jnp.eye(gpt, dtype=F32)

    def bmat(part):
        p = part.reshape(U_TILES, gpt, SSM_STATE, SSM_GROUP_CH)
        m = jnp.einsum('ugni,gh->ugihn', p, eye)
        return m.reshape(U_TILES, LANES, gpt * SSM_STATE)

    def cmat(part):
        p = part.reshape(U_TILES, gpt, SSM_GROUP_CH, SSM_STATE)
        m = jnp.einsum('ugin,gh->ugnhi', p, eye)
        return m.reshape(U_TILES, gpt * SSM_STATE, LANES)

    bblk = jnp.concatenate([bmat(jnp.real(b_bar)), bmat(jnp.imag(b_bar))], axis=2).astype(BF16)
    cblk = jnp.concatenate([cmat(c_re.astype(F32)), cmat(-c_im.astype(F32))], axis=1).astype(BF16)
    are = jnp.real(a_bar).reshape(RE_TILES, LANES)
    aim = jnp.imag(a_bar).reshape(RE_TILES, LANES)
    return bblk, cblk, are, aim


BIAS_LANES = 8


def _split3(f):
    p1 = f.astype(BF16).astype(F32)
    r = f - p1
    p2 = r.astype(BF16).astype(F32)
    p3 = (r - p2).astype(BF16).astype(F32)
    return p1, p2, p3


def _bias_cols(f, key_side):
    n = f.shape[1]
    one = jnp.ones((1, n), F32)
    zero = jnp.zeros((2, n), F32)
    rows = []
    for h in range(2):
        pieces = [p[h:h + 1, :] for p in _split3(f)]
        rows += [one] * 3 + [-p for p in pieces] if key_side else pieces + [one] * 3
        rows.append(zero)
    pt = jnp.concatenate(rows, axis=0).astype(BF16)
    eye = (lax.broadcasted_iota(jnp.int32, (2 * BIAS_LANES, LANES), 0) ==
           lax.broadcasted_iota(jnp.int32, (2 * BIAS_LANES, LANES), 1)).astype(BF16)
    return lax.dot_general(pt, eye, (((0,), (0,)), ((), ())), preferred_element_type=F32)


def _fox_kernel(q_ref, k_ref, v_ref, frow_ref, o_ref, kaug_ref, lhs_ref, s_ref, m_ref, acc_ref, *, t):
    i = pl.program_id(2)
    nq = pl.num_programs(2)
    lane = lax.broadcasted_iota(jnp.int32, (t, LANES), 1)
    first = lane < FOX_HEAD_DIM

    def store_lhs(qi):
        r0 = pl.multiple_of(qi * t, t)
        q = q_ref[pl.ds(r0, t), :]
        bias = _bias_cols(frow_ref[:, pl.ds(r0, t)], False)
        zero = jnp.zeros_like(q)
        fzero = jnp.zeros_like(bias)
        lhs_ref[0] = jnp.concatenate(
            [jnp.where(first, q, zero), jnp.where(lane < BIAS_LANES, bias, fzero).astype(BF16)], axis=1)
        lhs_ref[1] = jnp.concatenate(
            [jnp.where(first, zero, q), jnp.where(lane < BIAS_LANES, fzero, bias).astype(BF16)], axis=1)

    def reset_stats():
        m_ref[...] = jnp.full_like(m_ref, NEG_INF)
        acc_ref[...] = jnp.zeros_like(acc_ref)

    def produce(j, causal):
        k0 = pl.multiple_of(j * t, t)
        rk = jnp.concatenate([k_ref[pl.ds(k0, t), :], kaug_ref[pl.ds(k0, t), :]], axis=1)
        for h in range(2):
            s = _dot_nt(lhs_ref[h], rk)
            if causal:
                r = lax.broadcasted_iota(jnp.int32, s.shape, 0)
                c = lax.broadcasted_iota(jnp.int32, s.shape, 1)
                s = jnp.where(r >= c, s, NEG_INF)
            s_ref[h] = s

    def consume(j):
        k0 = pl.multiple_of(j * t, t)
        rv = jnp.concatenate([v_ref[pl.ds(k0, t), :], jnp.ones((t, LANES), BF16)], axis=1)
        for h in range(2):
            s = s_ref[h]
            m_old = m_ref[h]
            m_new = jnp.maximum(m_old, jnp.max(s, axis=-1, keepdims=True))
            alpha = jnp.exp2(m_old - m_new)
            p = jnp.exp2(s - jnp.concatenate([m_new] * (t // LANES), axis=1))
            acc_ref[h] = jnp.concatenate([alpha, alpha], axis=1) * acc_ref[h] + _dot(p.astype(BF16), rv)
            m_ref[h] = m_new

    @pl.when(i == 0)
    def _():
        kaug_ref[...] = _bias_cols(frow_ref[...], True).astype(BF16)
        store_lhs(0)
        reset_stats()
        produce(0, True)

    n_steady = jnp.maximum(i - 1, 0)

    def steady_pair(jj, _):
        j = 2 * jj
        consume(j)
        produce(j + 1, False)
        consume(j + 1)
        produce(j + 2, False)
        return 0
    lax.fori_loop(0, n_steady >> 1, steady_pair, 0)

    @pl.when((n_steady & 1) == 1)
    def _():
        consume(n_steady - 1)
        produce(n_steady, False)

    @pl.when(i > 0)
    def _():
        consume(i - 1)
        produce(i, True)

    consume(i)
    store_lhs(jnp.minimum(i + 1, nq - 1))
    produce(0, False)

    a0 = acc_ref[0]
    a1 = acc_ref[1]
    o_ref[...] = jnp.where(first, a0[:, :LANES] / a0[:, LANES:], a1[:, :LANES] / a1[:, LANES:]).astype(BF16)
    reset_stats()


def _fox(q, k, v, frow, t):
    b, s, w = q.shape
    assert s % t == 0
    kern = functools.partial(_fox_kernel, t=t)
    whole = lambda bi, p, i: (bi, 0, p)
    return pl.pallas_call(
        kern,
        grid=(b, FOX_PAIRS, s // t),
        in_specs=[
            pl.BlockSpec((None, s, LANES), whole),
            pl.BlockSpec((None, s, LANES), whole),
            pl.BlockSpec((None, s, LANES), whole),
            pl.BlockSpec((None, None, 2, s), lambda bi, p, i: (bi, p, 0, 0)),
        ],
        out_specs=pl.BlockSpec((None, t, LANES), lambda bi, p, i: (bi, i, p)),
        out_shape=jax.ShapeDtypeStruct((b, s, w), BF16),
        scratch_shapes=[
            pltpu.VMEM((s, LANES), BF16),
            pltpu.VMEM((2, t, 2 * LANES), BF16),
            pltpu.VMEM((2, t, t), F32),
            pltpu.VMEM((2, t, LANES), F32),
            pltpu.VMEM((2, t, 2 * LANES), F32),
        ],
        compiler_params=pltpu.CompilerParams(
            dimension_semantics=("arbitrary", "arbitrary", "arbitrary"),
            vmem_limit_bytes=VMEM_LIMIT_BYTES),
        name="fox",
    )(q, k, v, frow)


TAIL_SUB_ROWS = 256
FFN_CHUNK = 1024


def _tail_kernel(x_ref, ys_ref, yf_ref, kx_ref, vx_ref,
                 fox_g_ref, w_out_ref, mix_post_ref, xa_pre_ref, wq_ref, wo_ref, xa_post_ref,
                 ffn_pre_ref, wg_ref, wu_ref, wd_ref, ffn_post_ref, o_ref, *, n_sub):
    sub = x_ref.shape[0] // n_sub
    rows = [slice(r * sub, (r + 1) * sub) for r in range(n_sub)]
    each = lambda f, *lists: [f(*a) for a in zip(*lists)]

    x = [x_ref[r, :] for r in rows]
    yf = [_rms(yf_ref[r, :].astype(F32), fox_g_ref[...]).astype(BF16) for r in rows]
    mix = [_dot(ys_ref[r, :], w_out_ref[:SSM_WIDTH, :]) + _dot(f, w_out_ref[SSM_WIDTH:, :])
           for r, f in zip(rows, yf)]
    x = each(lambda xi, mi: xi + _rms(mi, mix_post_ref[...]), x, mix)

    h = [_rms(xi, xa_pre_ref[...]).astype(BF16) for xi in x]
    q = [_dot(hi, wq_ref[...]).astype(BF16) for hi in h]
    heads = [[] for _ in rows]
    for hd in range(XA_HEADS):
        sl = slice(hd * XA_HEAD_DIM, (hd + 1) * XA_HEAD_DIM)
        for qi, out in zip(q, heads):
            s = _dot_nt(qi[:, sl], kx_ref[:, sl])
            p = jnp.exp(s - jnp.max(s, axis=-1, keepdims=True))
            p = p / jnp.sum(p, axis=-1, keepdims=True)
            out.append(_dot(p.astype(BF16), vx_ref[:, sl]).astype(BF16))
    o = [jnp.concatenate(hs, axis=1) for hs in heads]
    x = each(lambda xi, oi: xi + _rms(_dot(oi, wo_ref[...]), xa_post_ref[...]), x, o)

    h = [_rms(xi, ffn_pre_ref[...]).astype(BF16) for xi in x]
    hidden = wg_ref.shape[1]
    down = [None] * n_sub
    for c0 in range(0, hidden, FFN_CHUNK):
        c1 = min(c0 + FFN_CHUNK, hidden)
        for r, hi in enumerate(h):
            gate = _dot(hi, wg_ref[:, c0:c1])
            up = _dot(hi, wu_ref[:, c0:c1])
            act = (gate * _sigmoid(gate) * up).astype(BF16)
            part = _dot(act, wd_ref[c0:c1, :])
            down[r] = part if down[r] is None else down[r] + part
    for r, xi, di in zip(rows, x, down):
        o_ref[r, :] = xi + _rms(di, ffn_post_ref[...])


def _tail(x, ys, yf, kx, vx, fox_g, w_out, mix_post_g, xa_pre_g, wq, wo, xa_post_g,
          ffn_pre_g, wg, wu, wd, ffn_post_g, tm):
    b, s, d = x.shape
    m = kx.shape[1]
    row = lambda i, j: (i, j, 0)
    per_b = lambda i, j: (i, 0, 0)
    const = lambda i, j: (0, 0)

    def resident(a):
        return pl.BlockSpec(a.shape, const, pipeline_mode=pl.Buffered(1))

    gains = [g.reshape(1, -1) for g in (fox_g, mix_post_g, xa_pre_g, xa_post_g, ffn_pre_g, ffn_post_g)]
    fox_g, mix_post_g, xa_pre_g, xa_post_g, ffn_pre_g, ffn_post_g = gains
    args = (x, ys, yf, kx, vx, fox_g, w_out, mix_post_g, xa_pre_g, wq, wo, xa_post_g,
            ffn_pre_g, wg, wu, wd, ffn_post_g)
    in_specs = [
        pl.BlockSpec((None, tm, d), row),
        pl.BlockSpec((None, tm, SSM_WIDTH), row),
        pl.BlockSpec((None, tm, FOX_WIDTH), row),
        pl.BlockSpec((None, m, d), per_b),
        pl.BlockSpec((None, m, d), per_b),
    ] + [resident(a) for a in args[5:]]
    return pl.pallas_call(
        functools.partial(_tail_kernel, n_sub=tm // TAIL_SUB_ROWS),
        grid=(b, s // tm),
        in_specs=in_specs,
        out_specs=pl.BlockSpec((None, tm, d), row),
        out_shape=jax.ShapeDtypeStruct((b, s, d), F32),
        compiler_params=pltpu.CompilerParams(
            dimension_semantics=("arbitrary", "arbitrary"), vmem_limit_bytes=VMEM_LIMIT_BYTES),
        name="tail",
    )(*args)


def _pick(n, pref):
    t = min(n, pref)
    assert n % t == 0, (n, t)
    return t


def kernel(x, mem, mix_pre_g, w_in, ssm_a_re, ssm_a_im, ssm_log_dt, ssm_b_re, ssm_b_im, ssm_c_re, ssm_c_im, ssm_d, ssm_glu_w, ssm_glu_b, fox_f_bias, ssm_out_g, fox_out_g, w_out, mix_post_g, xa_pre_g, mem_g, xa_wq, xa_wkv, xa_wo, xa_post_g, ffn_pre_g, w_gate, w_up, w_down, ffn_post_g):
    b, s, d = x.shape
    assert d == D_MODEL and s % LANES == 0
    n_main = SSM_WIDTH + 3 * FOX_WIDTH

    kx, vx = _memkv(mem, mem_g, xa_wkv.astype(BF16))

    u, q, k, v, fcum = _inproj(
        x, mix_pre_g, w_in[:, :n_main].astype(BF16), w_in[:, n_main:].T.astype(BF16),
        fox_f_bias, tm=_pick(s, 512))

    bblk, cblk, are, aim = _ssm_params(ssm_a_re, ssm_a_im, ssm_log_dt, ssm_b_re, ssm_b_im,
                                       ssm_c_re, ssm_c_im)
    y_ssm = _ssm(u, bblk, cblk, are, aim, ssm_d, ssm_glu_w.astype(BF16), ssm_glu_b, ssm_out_g,
                 tm=_pick(s, 128))

    y_fox = _fox(q, k, v, fcum.reshape(b, FOX_PAIRS, 2, s), t=_pick(s, 512))

    return _tail(x, y_ssm, y_fox, kx, vx, fox_out_g, w_out.astype(BF16), mix_post_g, xa_pre_g,
                 xa_wq.astype(BF16), xa_wo.astype(BF16), xa_post_g, ffn_pre_g,
                 w_gate.astype(BF16), w_up.astype(BF16), w_down.astype(BF16), ffn_post_g,
                 tm=_pick(s, 2 * TAIL_SUB_ROWS))
```

```python
import functools
import math

import jax
import jax.numpy as jnp
from jax import lax
from jax.experimental import pallas as pl
from jax.experimental.pallas import tpu as pltpu

F32 = jnp.float32
BF16 = jnp.bfloat16

LANES = 128
SUBLANES = 8
VMEM_LIMIT_BYTES = 56 * 1024 * 1024

D_MODEL = 1024
SSM_WIDTH = 512
SSM_GROUP_CH = 16
SSM_GROUPS = 32
SSM_STATE = 64
FOX_WIDTH = 512
FOX_HEAD_DIM = 64
FOX_HEADS = 8
FOX_PAIRS = FOX_HEADS * FOX_HEAD_DIM // LANES
XA_HEADS = 4
XA_HEAD_DIM = 256
RMS_EPS = 1e-6
NEG_INF = -1e30
LOG2E = math.log2(math.e)

STATE_TILES = 2 * SSM_GROUPS * SSM_STATE // LANES
RE_TILES = STATE_TILES // 2
U_TILES = SSM_WIDTH // LANES


def _rms(x, g):
    return x * lax.rsqrt(jnp.mean(x * x, axis=-1, keepdims=True) + RMS_EPS) * g


def _dot(a, b):
    return jnp.dot(a, b, preferred_element_type=F32)


def _dot_nt(a, b):
    return lax.dot_general(a, b, (((1,), (1,)), ((), ())), preferred_element_type=F32)


def _sigmoid(z):
    return 1.0 / (1.0 + jnp.exp(-z))


def _memkv_kernel(mem_ref, g_ref, w_ref, k_ref, v_ref):
    mn = _rms(mem_ref[...], g_ref[...]).astype(BF16)
    kv = _dot(mn, w_ref[...])
    k_ref[...] = (kv[:, :D_MODEL] * (1.0 / math.sqrt(XA_HEAD_DIM))).astype(BF16)
    v_ref[...] = kv[:, D_MODEL:].astype(BF16)


def _memkv(mem, mem_g, wkv):
    b, m, d = mem.shape
    return pl.pallas_call(
        _memkv_kernel,
        grid=(b,),
        in_specs=[
            pl.BlockSpec((None, m, d), lambda i: (i, 0, 0)),
            pl.BlockSpec((1, d), lambda i: (0, 0)),
            pl.BlockSpec((d, 2 * d), lambda i: (0, 0)),
        ],
        out_specs=[
            pl.BlockSpec((None, m, d), lambda i: (i, 0, 0)),
            pl.BlockSpec((None, m, d), lambda i: (i, 0, 0)),
        ],
        out_shape=[jax.ShapeDtypeStruct((b, m, d), BF16)] * 2,
        compiler_params=pltpu.CompilerParams(
            dimension_semantics=("arbitrary",), vmem_limit_bytes=VMEM_LIMIT_BYTES),
        name="memkv",
    )(mem, mem_g.reshape(1, d), wkv)


def _inproj_kernel(x_ref, g_ref, w_ref, wf_ref, fb_ref, tri_ref,
                   u_ref, q_ref, k_ref, v_ref, f_ref, carry_ref):
    @pl.when(pl.program_id(1) == 0)
    def _():
        carry_ref[...] = jnp.zeros_like(carry_ref)

    hb = _rms(x_ref[...], g_ref[...]).astype(BF16)
    proj = _dot(hb, w_ref[...])
    u_ref[...] = proj[:, :SSM_WIDTH].astype(BF16)
    o = SSM_WIDTH
    q_ref[...] = (proj[:, o:o + FOX_WIDTH] * (LOG2E / math.sqrt(FOX_HEAD_DIM))).astype(BF16)
    k_ref[...] = proj[:, o + FOX_WIDTH:o + 2 * FOX_WIDTH].astype(BF16)
    v_ref[...] = proj[:, o + 2 * FOX_WIDTH:].astype(BF16)

    z = _dot_nt(wf_ref[...], hb) + fb_ref[...]
    logf = jnp.minimum(z, 0.0) - jnp.log1p(jnp.exp(-jnp.abs(z)))
    p1 = logf.astype(BF16)
    r1 = logf - p1.astype(F32)
    p2 = r1.astype(BF16)
    p3 = (r1 - p2.astype(F32)).astype(BF16)
    tri = tri_ref[...]
    csum = _dot(p1, tri) + _dot(p2, tri) + _dot(p3, tri)
    f = csum + carry_ref[...][:, :1]
    f_ref[...] = f * LOG2E
    carry_ref[...] = jnp.broadcast_to(f[:, -1:], carry_ref.shape)


def _inproj(x, mix_pre_g, w_main, wf_t, f_bias, tm):
    b, s, d = x.shape
    nt = s // tm
    tri = jnp.triu(jnp.ones((tm, tm), F32)).astype(BF16)
    row = lambda i, j: (i, j, 0)
    const2 = lambda i, j: (0, 0)
    return pl.pallas_call(
        _inproj_kernel,
        grid=(b, nt),
        in_specs=[
            pl.BlockSpec((None, tm, d), row),
            pl.BlockSpec((1, d), const2),
            pl.BlockSpec(w_main.shape, const2),
            pl.BlockSpec(wf_t.shape, const2),
            pl.BlockSpec((FOX_HEADS, 1), const2),
            pl.BlockSpec((tm, tm), const2),
        ],
        out_specs=[
            pl.BlockSpec((None, tm, SSM_WIDTH), row),
            pl.BlockSpec((None, tm, FOX_WIDTH), row),
            pl.BlockSpec((None, tm, FOX_WIDTH), row),
            pl.BlockSpec((None, tm, FOX_WIDTH), row),
            pl.BlockSpec((None, FOX_HEADS, tm), lambda i, j: (i, 0, j)),
        ],
        out_shape=[
            jax.ShapeDtypeStruct((b, s, SSM_WIDTH), BF16),
            jax.ShapeDtypeStruct((b, s, FOX_WIDTH), BF16),
            jax.ShapeDtypeStruct((b, s, FOX_WIDTH), BF16),
            jax.ShapeDtypeStruct((b, s, FOX_WIDTH), BF16),
            jax.ShapeDtypeStruct((b, FOX_HEADS, s), F32),
        ],
        scratch_shapes=[pltpu.VMEM((FOX_HEADS, LANES), F32)],
        compiler_params=pltpu.CompilerParams(
            dimension_semantics=("arbitrary", "arbitrary"), vmem_limit_bytes=VMEM_LIMIT_BYTES),
        name="inproj",
    )(x, mix_pre_g.reshape(1, d), w_main, wf_t, f_bias.reshape(FOX_HEADS, 1), tri)


def _gelu_tanh(x):
    c = math.sqrt(2.0 / math.pi)
    return 0.5 * x * (1.0 + jnp.tanh(c * (x + 0.044715 * (x * x * x))))


TILE_PITCH = 20
TIME_PITCH = 324


def _ssm_kernel(u_ref, bblk_ref, cblk_ref, are_ref, aim_ref, d_ref, gw_ref, gb_ref, g_ref,
                o_ref, s_ref, xc_ref, *, nb, tm):
    @pl.when(pl.program_id(0) == 0)
    def _():
        xc_ref[...] = jnp.zeros_like(xc_ref)

    def rows_of_tile(b, part, k):
        return pl.ds(TILE_PITCH * k + 2 * b + part, tm, stride=TIME_PITCH)

    for ut in range(U_TILES):
        ub = u_ref[:, :, ut * LANES:(ut + 1) * LANES].reshape(nb * tm, LANES)
        res = _dot(ub, bblk_ref[ut])
        for b in range(nb):
            for j in range(2 * U_TILES):
                part, k = divmod(j, U_TILES)
                s_ref[rows_of_tile(b, part, U_TILES * ut + k), :] = (
                    res[b * tm:(b + 1) * tm, j * LANES:(j + 1) * LANES])

    ar = are_ref[...]
    ai = aim_ref[...]

    def step(t, carry):
        new = []
        for b in range(nb):
            xr, xi = carry[2 * b], carry[2 * b + 1]
            ire = pl.ds(TIME_PITCH * t + 2 * b, RE_TILES, stride=TILE_PITCH)
            iim = pl.ds(TIME_PITCH * t + 2 * b + 1, RE_TILES, stride=TILE_PITCH)
            nr = ar * xr - ai * xi + s_ref[ire, :]
            ni = ar * xi + ai * xr + s_ref[iim, :]
            s_ref[ire, :] = nr
            s_ref[iim, :] = ni
            new += [nr, ni]
        return tuple(new)

    init = []
    for b in range(nb):
        init.append(xc_ref[b * STATE_TILES:b * STATE_TILES + RE_TILES, :])
        init.append(xc_ref[b * STATE_TILES + RE_TILES:(b + 1) * STATE_TILES, :])
    fin = lax.fori_loop(0, tm, step, tuple(init))
    for b in range(nb):
        xc_ref[b * STATE_TILES:b * STATE_TILES + RE_TILES, :] = fin[2 * b]
        xc_ref[b * STATE_TILES + RE_TILES:(b + 1) * STATE_TILES, :] = fin[2 * b + 1]

    ys = []
    for ot in range(U_TILES):
        per_seq = []
        for b in range(nb):
            tiles = [s_ref[rows_of_tile(b, part, U_TILES * ot + k), :].astype(BF16)
                     for part in range(2) for k in range(U_TILES)]
            per_seq.append(jnp.concatenate(tiles, axis=1))
        ys.append(_dot(jnp.concatenate(per_seq, axis=0), cblk_ref[ot]))
    u_all = u_ref[...].reshape(nb * tm, SSM_WIDTH).astype(F32)
    y = jnp.concatenate(ys, axis=1) + d_ref[...] * u_all
    g = _gelu_tanh(y)
    out = g * _sigmoid(_dot(g.astype(BF16), gw_ref[...]) + gb_ref[...])
    o_ref[...] = _rms(out, g_ref[...]).astype(BF16).reshape(nb, tm, SSM_WIDTH)


def _ssm(u, bblk, cblk, a_re, a_im, d_skip, glu_w, glu_b, out_g, tm):
    b, s, w = u.shape
    assert 2 * b <= TILE_PITCH - 4 and TIME_PITCH >= TILE_PITCH * (RE_TILES - 1) + 2 * b
    kern = functools.partial(_ssm_kernel, nb=b, tm=tm)
    c2 = lambda t: (0, 0)
    c3 = lambda t: (0, 0, 0)
    return pl.pallas_call(
        kern,
        grid=(s // tm,),
        in_specs=[
            pl.BlockSpec((b, tm, w), lambda t: (0, t, 0)),
            pl.BlockSpec(bblk.shape, c3),
            pl.BlockSpec(cblk.shape, c3),
            pl.BlockSpec(a_re.shape, c2),
            pl.BlockSpec(a_im.shape, c2),
            pl.BlockSpec((1, w), c2),
            pl.BlockSpec(glu_w.shape, c2),
            pl.BlockSpec((1, w), c2),
            pl.BlockSpec((1, w), c2),
        ],
        out_specs=pl.BlockSpec((b, tm, w), lambda t: (0, t, 0)),
        out_shape=jax.ShapeDtypeStruct((b, s, w), BF16),
        scratch_shapes=[
            pltpu.VMEM((tm * TIME_PITCH, LANES), F32),
            pltpu.VMEM((b * STATE_TILES, LANES), F32),
        ],
        compiler_params=pltpu.CompilerParams(
            dimension_semantics=("arbitrary",), vmem_limit_bytes=VMEM_LIMIT_BYTES),
        name="ssm",
    )(u, bblk, cblk, a_re, a_im, d_skip.reshape(1, w), glu_w, glu_b.reshape(1, w),
      out_g.reshape(1, w))


def _ssm_params(a_re, a_im, log_dt, b_re, b_im, c_re, c_im):
    a = lax.complex(a_re.astype(F32), a_im.astype(F32))
    dt = jnp.exp(log_dt.astype(F32))[:, None]
    a_bar = jnp.exp(a * dt)
    b_bar = ((a_bar - 1.0) / a)[..., None] * lax.complex(b_re.astype(F32), b_im.astype(F32))
    gpt = LANES // SSM_GROUP_CH
    eye = jnp.eye(gpt, dtype=F32)

    def bmat(part):
        p = part.reshape(U_TILES, gpt, SSM_STATE, SSM_GROUP_CH)
        m = jnp.einsum('ugni,gh->ugihn', p, eye)
        return m.reshape(U_TILES, LANES, gpt * SSM_STATE)

    def cmat(part):
        p = part.reshape(U_TILES, gpt, SSM_GROUP_CH, SSM_STATE)
        m = jnp.einsum('ugin,gh->ugnhi', p, eye)
        return m.reshape(U_TILES, gpt * SSM_STATE, LANES)

    bblk = jnp.concatenate([bmat(jnp.real(b_bar)), bmat(jnp.imag(b_bar))], axis=2).astype(BF16)
    cblk = jnp.concatenate([cmat(c_re.astype(F32)), cmat(-c_im.astype(F32))], axis=1).astype(BF16)
    are = jnp.real(a_bar).reshape(RE_TILES, LANES)
    aim = jnp.imag(a_bar).reshape(RE_TILES, LANES)
    return bblk, cblk, are, aim


BIAS_LANES = 8


def _split3(f):
    p1 = f.astype(BF16).astype(F32)
    r = f - p1
    p2 = r.astype(BF16).astype(F32)
    p3 = (r - p2).astype(BF16).astype(F32)
    return p1, p2, p3


def _bias_cols(f, key_side):
    n = f.shape[1]
    one = jnp.ones((1, n), F32)
    zero = jnp.zeros((2, n), F32)
    rows = []
    for h in range(2):
        pieces = [p[h:h + 1, :] for p in _split3(f)]
        rows += [one] * 3 + [-p for p in pieces] if key_side else pieces + [one] * 3
        rows.append(zero)
    pt = jnp.concatenate(rows, axis=0).astype(BF16)
    eye = (lax.broadcasted_iota(jnp.int32, (2 * BIAS_LANES, LANES), 0) ==
           lax.broadcasted_iota(jnp.int32, (2 * BIAS_LANES, LANES), 1)).astype(BF16)
    return lax.dot_general(pt, eye, (((0,), (0,)), ((), ())), preferred_element_type=F32)


def _fox_kernel(q_ref, k_ref, v_ref, frow_ref, o_ref, kaug_ref, lhs_ref, s_ref, m_ref, acc_ref, *, t):
    i = pl.program_id(2)
    nq = pl.num_programs(2)
    lane = lax.broadcasted_iota(jnp.int32, (t, LANES), 1)
    first = lane < FOX_HEAD_DIM

    def store_lhs(qi):
        r0 = pl.multiple_of(qi * t, t)
        q = q_ref[pl.ds(r0, t), :]
        bias = _bias_cols(frow_ref[:, pl.ds(r0, t)], False)
        zero = jnp.zeros_like(q)
        fzero = jnp.zeros_like(bias)
        lhs_ref[:t, :] = jnp.concatenate(
            [jnp.where(first, q, zero), jnp.where(lane < BIAS_LANES, bias, fzero).astype(BF16)], axis=1)
        lhs_ref[t:, :] = jnp.concatenate(
            [jnp.where(first, zero, q), jnp.where(lane < BIAS_LANES, fzero, bias).astype(BF16)], axis=1)

    def reset_stats():
        m_ref[...] = jnp.full_like(m_ref, NEG_INF)
        acc_ref[...] = jnp.zeros_like(acc_ref)

    def produce(j, causal):
        k0 = pl.multiple_of(j * t, t)
        rk = jnp.concatenate([k_ref[pl.ds(k0, t), :], kaug_ref[pl.ds(k0, t), :]], axis=1)
        s = _dot_nt(lhs_ref[...], rk)
        if causal:
            r = lax.broadcasted_iota(jnp.int32, s.shape, 0) & (t - 1)
            c = lax.broadcasted_iota(jnp.int32, s.shape, 1)
            s = jnp.where(r >= c, s, NEG_INF)
        s_ref[...] = s

    def consume(j):
        k0 = pl.multiple_of(j * t, t)
        rv = jnp.concatenate([v_ref[pl.ds(k0, t), :], jnp.ones((t, LANES), BF16)], axis=1)
        s = s_ref[...]
        m_old = m_ref[...]
        m_new = jnp.maximum(m_old, jnp.max(s, axis=-1, keepdims=True))
        alpha = jnp.exp2(m_old - m_new)
        p = jnp.exp2(s - jnp.concatenate([m_new] * (t // LANES), axis=1))
        acc_ref[...] = jnp.concatenate([alpha, alpha], axis=1) * acc_ref[...] + _dot(p.astype(BF16), rv)
        m_ref[...] = m_new

    @pl.when(i == 0)
    def _():
        kaug_ref[...] = _bias_cols(frow_ref[...], True).astype(BF16)
        store_lhs(0)
        reset_stats()
        produce(0, True)

    n_steady = jnp.maximum(i - 1, 0)

    def steady_pair(jj, _):
        j = 2 * jj
        consume(j)
        produce(j + 1, False)
        consume(j + 1)
        produce(j + 2, False)
        return 0
    lax.fori_loop(0, n_steady >> 1, steady_pair, 0)

    @pl.when((n_steady & 1) == 1)
    def _():
        consume(n_steady - 1)
        produce(n_steady, False)

    @pl.when(i > 0)
    def _():
        consume(i - 1)
        produce(i, True)

    consume(i)
    store_lhs(jnp.minimum(i + 1, nq - 1))
    produce(0, False)

    a0 = acc_ref[:t, :]
    a1 = acc_ref[t:, :]
    o_ref[...] = jnp.where(first, a0[:, :LANES] / a0[:, LANES:], a1[:, :LANES] / a1[:, LANES:]).astype(BF16)
    reset_stats()


def _fox(q, k, v, frow, t):
    b, s, w = q.shape
    assert s % t == 0
    kern = functools.partial(_fox_kernel, t=t)
    whole = lambda bi, p, i: (bi, 0, p)
    return pl.pallas_call(
        kern,
        grid=(b, FOX_PAIRS, s // t),
        in_specs=[
            pl.BlockSpec((None, s, LANES), whole),
            pl.BlockSpec((None, s, LANES), whole),
            pl.BlockSpec((None, s, LANES), whole),
            pl.BlockSpec((None, None, 2, s), lambda bi, p, i: (bi, p, 0, 0)),
        ],
        out_specs=pl.BlockSpec((None, t, LANES), lambda bi, p, i: (bi, i, p)),
        out_shape=jax.ShapeDtypeStruct((b, s, w), BF16),
        scratch_shapes=[
            pltpu.VMEM((s, LANES), BF16),
            pltpu.VMEM((2 * t, 2 * LANES), BF16),
            pltpu.VMEM((2 * t, t), F32),
            pltpu.VMEM((2 * t, LANES), F32),
            pltpu.VMEM((2 * t, 2 * LANES), F32),
        ],
        compiler_params=pltpu.CompilerParams(
            dimension_semantics=("arbitrary", "arbitrary", "arbitrary"),
            vmem_limit_bytes=VMEM_LIMIT_BYTES),
        name="fox",
    )(q, k, v, frow)


TAIL_SUB_ROWS = 256
FFN_CHUNK = 1024


def _tail_kernel(x_ref, ys_ref, yf_ref, kx_ref, vx_ref,
                 fox_g_ref, w_out_ref, mix_post_ref, xa_pre_ref, wq_ref, wo_ref, xa_post_ref,
                 ffn_pre_ref, wg_ref, wu_ref, wd_ref, ffn_post_ref, o_ref, *, n_sub):
    sub = x_ref.shape[0] // n_sub
    rows = [slice(r * sub, (r + 1) * sub) for r in range(n_sub)]
    each = lambda f, *lists: [f(*a) for a in zip(*lists)]

    x = [x_ref[r, :] for r in rows]
    yf = [_rms(yf_ref[r, :].astype(F32), fox_g_ref[...]).astype(BF16) for r in rows]
    mix = [_dot(ys_ref[r, :], w_out_ref[:SSM_WIDTH, :]) + _dot(f, w_out_ref[SSM_WIDTH:, :])
           for r, f in zip(rows, yf)]
    x = each(lambda xi, mi: xi + _rms(mi, mix_post_ref[...]), x, mix)

    h = [_rms(xi, xa_pre_ref[...]).astype(BF16) for xi in x]
    q = [_dot(hi, wq_ref[...]).astype(BF16) for hi in h]
    heads = [[] for _ in rows]
    for hd in range(XA_HEADS):
        sl = slice(hd * XA_HEAD_DIM, (hd + 1) * XA_HEAD_DIM)
        for qi, out in zip(q, heads):
            s = _dot_nt(qi[:, sl], kx_ref[:, sl])
            p = jnp.exp(s - jnp.max(s, axis=-1, keepdims=True))
            p = p / jnp.sum(p, axis=-1, keepdims=True)
            out.append(_dot(p.astype(BF16), vx_ref[:, sl]).astype(BF16))
    o = [jnp.concatenate(hs, axis=1) for hs in heads]
    x = each(lambda xi, oi: xi + _rms(_dot(oi, wo_ref[...]), xa_post_ref[...]), x, o)

    h = [_rms(xi, ffn_pre_ref[...]).astype(BF16) for xi in x]
    hidden = wg_ref.shape[1]
    down = [None] * n_sub
    for c0 in range(0, hidden, FFN_CHUNK):
        c1 = min(c0 + FFN_CHUNK, hidden)
        for r, hi in enumerate(h):
            gate = _dot(hi, wg_ref[:, c0:c1])
            up = _dot(hi, wu_ref[:, c0:c1])
            act = (gate * _sigmoid(gate) * up).astype(BF16)
            part = _dot(act, wd_ref[c0:c1, :])
            down[r] = part if down[r] is None else down[r] + part
    for r, xi, di in zip(rows, x, down):
        o_ref[r, :] = xi + _rms(di, ffn_post_ref[...])


def _tail(x, ys, yf, kx, vx, fox_g, w_out, mix_post_g, xa_pre_g, wq, wo, xa_post_g,
          ffn_pre_g, wg, wu, wd, ffn_post_g, tm):
    b, s, d = x.shape
    m = kx.shape[1]
    row = lambda i, j: (i, j, 0)
    per_b = lambda i, j: (i, 0, 0)
    const = lambda i, j: (0, 0)

    def resident(a):
        return pl.BlockSpec(a.shape, const, pipeline_mode=pl.Buffered(1))

    gains = [g.reshape(1, -1) for g in (fox_g, mix_post_g, xa_pre_g, xa_post_g, ffn_pre_g, ffn_post_g)]
    fox_g, mix_post_g, xa_pre_g, xa_post_g, ffn_pre_g, ffn_post_g = gains
    args = (x, ys, yf, kx, vx, fox_g, w_out, mix_post_g, xa_pre_g, wq, wo, xa_post_g,
            ffn_pre_g, wg, wu, wd, ffn_post_g)
    in_specs = [
        pl.BlockSpec((None, tm, d), row),
        pl.BlockSpec((None, tm, SSM_WIDTH), row),
        pl.BlockSpec((None, tm, FOX_WIDTH), row),
        pl.BlockSpec((None, m, d), per_b),
        pl.BlockSpec((None, m, d), per_b),
    ] + [resident(a) for a in args[5:]]
    return pl.pallas_call(
        functools.partial(_tail_kernel, n_sub=tm // TAIL_SUB_ROWS),
        grid=(b, s // tm),
        in_specs=in_specs,
        out_specs=pl.BlockSpec((None, tm, d), row),
        out_shape=jax.ShapeDtypeStruct((b, s, d), F32),
        compiler_params=pltpu.CompilerParams(
            dimension_semantics=("arbitrary", "arbitrary"), vmem_limit_bytes=VMEM_LIMIT_BYTES),
        name="tail",
    )(*args)


def _pick(n, pref):
    t = min(n, pref)
    assert n % t == 0, (n, t)
    return t


def kernel(x, mem, mix_pre_g, w_in, ssm_a_re, ssm_a_im, ssm_log_dt, ssm_b_re, ssm_b_im, ssm_c_re, ssm_c_im, ssm_d, ssm_glu_w, ssm_glu_b, fox_f_bias, ssm_out_g, fox_out_g, w_out, mix_post_g, xa_pre_g, mem_g, xa_wq, xa_wkv, xa_wo, xa_post_g, ffn_pre_g, w_gate, w_up, w_down, ffn_post_g):
    b, s, d = x.shape
    assert d == D_MODEL and s % LANES == 0
    n_main = SSM_WIDTH + 3 * FOX_WIDTH

    kx, vx = _memkv(mem, mem_g, xa_wkv.astype(BF16))

    u, q, k, v, fcum = _inproj(
        x, mix_pre_g, w_in[:, :n_main].astype(BF16), w_in[:, n_main:].T.astype(BF16),
        fox_f_bias, tm=_pick(s, 512))

    bblk, cblk, are, aim = _ssm_params(ssm_a_re, ssm_a_im, ssm_log_dt, ssm_b_re, ssm_b_im,
                                       ssm_c_re, ssm_c_im)
    y_ssm = _ssm(u, bblk, cblk, are, aim, ssm_d, ssm_glu_w.astype(BF16), ssm_glu_b, ssm_out_g,
                 tm=_pick(s, 128))

    y_fox = _fox(q, k, v, fcum.reshape(b, FOX_PAIRS, 2, s), t=_pick(s, 512))

    return _tail(x, y_ssm, y_fox, kx, vx, fox_out_g, w_out.astype(BF16), mix_post_g, xa_pre_g,
                 xa_wq.astype(BF16), xa_wo.astype(BF16), xa_post_g, ffn_pre_g,
                 w_gate.astype(BF16), w_up.astype(BF16), w_down.astype(BF16), ffn_post_g,
                 tm=_pick(s, 2 * TAIL_SUB_ROWS))
```

```python
import functools
import math

import jax
import jax.numpy as jnp
from jax import lax
from jax.experimental import pallas as pl
from jax.experimental.pallas import tpu as pltpu

F32 = jnp.float32
BF16 = jnp.bfloat16

LANES = 128
SUBLANES = 8
VMEM_LIMIT_BYTES = 56 * 1024 * 1024

D_MODEL = 1024
SSM_WIDTH = 512
SSM_GROUP_CH = 16
SSM_GROUPS = 32
SSM_STATE = 64
FOX_WIDTH = 512
FOX_HEAD_DIM = 64
FOX_HEADS = 8
FOX_PAIRS = FOX_HEADS * FOX_HEAD_DIM // LANES
XA_HEADS = 4
XA_HEAD_DIM = 256
RMS_EPS = 1e-6
NEG_INF = -1e30
LOG2E = math.log2(math.e)

STATE_TILES = 2 * SSM_GROUPS * SSM_STATE // LANES
RE_TILES = STATE_TILES // 2
U_TILES = SSM_WIDTH // LANES


def _rms(x, g):
    return x * lax.rsqrt(jnp.mean(x * x, axis=-1, keepdims=True) + RMS_EPS) * g


def _dot(a, b):
    return jnp.dot(a, b, preferred_element_type=F32)


def _dot_nt(a, b):
    return lax.dot_general(a, b, (((1,), (1,)), ((), ())), preferred_element_type=F32)


def _sigmoid(z):
    return 1.0 / (1.0 + jnp.exp(-z))


def _memkv_kernel(mem_ref, g_ref, w_ref, k_ref, v_ref):
    mn = _rms(mem_ref[...], g_ref[...]).astype(BF16)
    kv = _dot(mn, w_ref[...])
    k_ref[...] = (kv[:, :D_MODEL] * (1.0 / math.sqrt(XA_HEAD_DIM))).astype(BF16)
    v_ref[...] = kv[:, D_MODEL:].astype(BF16)


def _memkv(mem, mem_g, wkv):
    b, m, d = mem.shape
    return pl.pallas_call(
        _memkv_kernel,
        grid=(b,),
        in_specs=[
            pl.BlockSpec((None, m, d), lambda i: (i, 0, 0)),
            pl.BlockSpec((1, d), lambda i: (0, 0)),
            pl.BlockSpec((d, 2 * d), lambda i: (0, 0)),
        ],
        out_specs=[
            pl.BlockSpec((None, m, d), lambda i: (i, 0, 0)),
            pl.BlockSpec((None, m, d), lambda i: (i, 0, 0)),
        ],
        out_shape=[jax.ShapeDtypeStruct((b, m, d), BF16)] * 2,
        compiler_params=pltpu.CompilerParams(
            dimension_semantics=("arbitrary",), vmem_limit_bytes=VMEM_LIMIT_BYTES),
        name="memkv",
    )(mem, mem_g.reshape(1, d), wkv)


def _inproj_kernel(x_ref, g_ref, w_ref, wf_ref, fb_ref, tri_ref,
                   u_ref, q_ref, k_ref, v_ref, f_ref, carry_ref):
    @pl.when(pl.program_id(1) == 0)
    def _():
        carry_ref[...] = jnp.zeros_like(carry_ref)

    hb = _rms(x_ref[...], g_ref[...]).astype(BF16)
    proj = _dot(hb, w_ref[...])
    u_ref[...] = proj[:, :SSM_WIDTH].astype(BF16)
    o = SSM_WIDTH
    q_ref[...] = (proj[:, o:o + FOX_WIDTH] * (LOG2E / math.sqrt(FOX_HEAD_DIM))).astype(BF16)
    k_ref[...] = proj[:, o + FOX_WIDTH:o + 2 * FOX_WIDTH].astype(BF16)
    v_ref[...] = proj[:, o + 2 * FOX_WIDTH:].astype(BF16)

    z = _dot_nt(wf_ref[...], hb) + fb_ref[...]
    logf = jnp.minimum(z, 0.0) - jnp.log1p(jnp.exp(-jnp.abs(z)))
    p1 = logf.astype(BF16)
    r1 = logf - p1.astype(F32)
    p2 = r1.astype(BF16)
    p3 = (r1 - p2.astype(F32)).astype(BF16)
    tri = tri_ref[...]
    csum = _dot(p1, tri) + _dot(p2, tri) + _dot(p3, tri)
    f = csum + carry_ref[...][:, :1]
    f_ref[...] = f * LOG2E
    carry_ref[...] = jnp.broadcast_to(f[:, -1:], carry_ref.shape)


def _inproj(x, mix_pre_g, w_main, wf_t, f_bias, tm):
    b, s, d = x.shape
    nt = s // tm
    tri = jnp.triu(jnp.ones((tm, tm), F32)).astype(BF16)
    row = lambda i, j: (i, j, 0)
    const2 = lambda i, j: (0, 0)
    return pl.pallas_call(
        _inproj_kernel,
        grid=(b, nt),
        in_specs=[
            pl.BlockSpec((None, tm, d), row),
            pl.BlockSpec((1, d), const2),
            pl.BlockSpec(w_main.shape, const2),
            pl.BlockSpec(wf_t.shape, const2),
            pl.BlockSpec((FOX_HEADS, 1), const2),
            pl.BlockSpec((tm, tm), const2),
        ],
        out_specs=[
            pl.BlockSpec((None, tm, SSM_WIDTH), row),
            pl.BlockSpec((None, tm, FOX_WIDTH), row),
            pl.BlockSpec((None, tm, FOX_WIDTH), row),
            pl.BlockSpec((None, tm, FOX_WIDTH), row),
            pl.BlockSpec((None, FOX_HEADS, tm), lambda i, j: (i, 0, j)),
        ],
        out_shape=[
            jax.ShapeDtypeStruct((b, s, SSM_WIDTH), BF16),
            jax.ShapeDtypeStruct((b, s, FOX_WIDTH), BF16),
            jax.ShapeDtypeStruct((b, s, FOX_WIDTH), BF16),
            jax.ShapeDtypeStruct((b, s, FOX_WIDTH), BF16),
            jax.ShapeDtypeStruct((b, FOX_HEADS, s), F32),
        ],
        scratch_shapes=[pltpu.VMEM((FOX_HEADS, LANES), F32)],
        compiler_params=pltpu.CompilerParams(
            dimension_semantics=("arbitrary", "arbitrary"), vmem_limit_bytes=VMEM_LIMIT_BYTES),
        name="inproj",
    )(x, mix_pre_g.reshape(1, d), w_main, wf_t, f_bias.reshape(FOX_HEADS, 1), tri)


def _gelu_tanh(x):
    c = math.sqrt(2.0 / math.pi)
    return 0.5 * x * (1.0 + jnp.tanh(c * (x + 0.044715 * (x * x * x))))


TILE_PITCH = 20
TIME_PITCH = 324


def _ssm_kernel(u_ref, bblk_ref, cblk_ref, are_ref, aim_ref, d_ref, gw_ref, gb_ref, g_ref,
                o_ref, s_ref, xc_ref, *, nb, tm):
    @pl.when(pl.program_id(0) == 0)
    def _():
        xc_ref[...] = jnp.zeros_like(xc_ref)

    def rows_of_tile(b, part, k):
        return pl.ds(TILE_PITCH * k + 2 * b + part, tm, stride=TIME_PITCH)

    for ut in range(U_TILES):
        ub = u_ref[:, :, ut * LANES:(ut + 1) * LANES].reshape(nb * tm, LANES)
        res = _dot(ub, bblk_ref[ut])
        for b in range(nb):
            for j in range(2 * U_TILES):
                part, k = divmod(j, U_TILES)
                s_ref[rows_of_tile(b, part, U_TILES * ut + k), :] = (
                    res[b * tm:(b + 1) * tm, j * LANES:(j + 1) * LANES])

    ar = are_ref[...]
    ai = aim_ref[...]

    def step(t, carry):
        new = []
        for b in range(nb):
            xr, xi = carry[2 * b], carry[2 * b + 1]
            ire = pl.ds(TIME_PITCH * t + 2 * b, RE_TILES, stride=TILE_PITCH)
            iim = pl.ds(TIME_PITCH * t + 2 * b + 1, RE_TILES, stride=TILE_PITCH)
            nr = ar * xr - ai * xi + s_ref[ire, :]
            ni = ar * xi + ai * xr + s_ref[iim, :]
            s_ref[ire, :] = nr
            s_ref[iim, :] = ni
            new += [nr, ni]
        return tuple(new)

    init = []
    for b in range(nb):
        init.append(xc_ref[b * STATE_TILES:b * STATE_TILES + RE_TILES, :])
        init.append(xc_ref[b * STATE_TILES + RE_TILES:(b + 1) * STATE_TILES, :])
    fin = lax.fori_loop(0, tm, step, tuple(init))
    for b in range(nb):
        xc_ref[b * STATE_TILES:b * STATE_TILES + RE_TILES, :] = fin[2 * b]
        xc_ref[b * STATE_TILES + RE_TILES:(b + 1) * STATE_TILES, :] = fin[2 * b + 1]

    ys = []
    for ot in range(U_TILES):
        per_seq = []
        for b in range(nb):
            tiles = [s_ref[rows_of_tile(b, part, U_TILES * ot + k), :].astype(BF16)
                     for part in range(2) for k in range(U_TILES)]
            per_seq.append(jnp.concatenate(tiles, axis=1))
        ys.append(_dot(jnp.concatenate(per_seq, axis=0), cblk_ref[ot]))
    u_all = u_ref[...].reshape(nb * tm, SSM_WIDTH).astype(F32)
    y = jnp.concatenate(ys, axis=1) + d_ref[...] * u_all
    g = _gelu_tanh(y)
    out = g * _sigmoid(_dot(g.astype(BF16), gw_ref[...]) + gb_ref[...])
    o_ref[...] = _rms(out, g_ref[...]).astype(BF16).reshape(nb, tm, SSM_WIDTH)


def _ssm(u, bblk, cblk, a_re, a_im, d_skip, glu_w, glu_b, out_g, tm):
    b, s, w = u.shape
    assert 2 * b <= TILE_PITCH - 4 and TIME_PITCH >= TILE_PITCH * (RE_TILES - 1) + 2 * b
    kern = functools.partial(_ssm_kernel, nb=b, tm=tm)
    c2 = lambda t: (0, 0)
    c3 = lambda t: (0, 0, 0)
    return pl.pallas_call(
        kern,
        grid=(s // tm,),
        in_specs=[
            pl.BlockSpec((b, tm, w), lambda t: (0, t, 0)),
            pl.BlockSpec(bblk.shape, c3),
            pl.BlockSpec(cblk.shape, c3),
            pl.BlockSpec(a_re.shape, c2),
            pl.BlockSpec(a_im.shape, c2),
            pl.BlockSpec((1, w), c2),
            pl.BlockSpec(glu_w.shape, c2),
            pl.BlockSpec((1, w), c2),
            pl.BlockSpec((1, w), c2),
        ],
        out_specs=pl.BlockSpec((b, tm, w), lambda t: (0, t, 0)),
        out_shape=jax.ShapeDtypeStruct((b, s, w), BF16),
        scratch_shapes=[
            pltpu.VMEM((tm * TIME_PITCH, LANES), F32),
            pltpu.VMEM((b * STATE_TILES, LANES), F32),
        ],
        compiler_params=pltpu.CompilerParams(
            dimension_semantics=("arbitrary",), vmem_limit_bytes=VMEM_LIMIT_BYTES),
        name="ssm",
    )(u, bblk, cblk, a_re, a_im, d_skip.reshape(1, w), glu_w, glu_b.reshape(1, w),
      out_g.reshape(1, w))


def _ssm_params(a_re, a_im, log_dt, b_re, b_im, c_re, c_im):
    a = lax.complex(a_re.astype(F32), a_im.astype(F32))
    dt = jnp.exp(log_dt.astype(F32))[:, None]
    a_bar = jnp.exp(a * dt)
    b_bar = ((a_bar - 1.0) / a)[..., None] * lax.complex(b_re.astype(F32), b_im.astype(F32))
    gpt = LANES // SSM_GROUP_CH
    eye = jnp.eye(gpt, dtype=F32)

    def bmat(part):
        p = part.reshape(U_TILES, gpt, SSM_STATE, SSM_GROUP_CH)
        m = jnp.einsum('ugni,gh->ugihn', p, eye)
        return m.reshape(U_TILES, LANES, gpt * SSM_STATE)

    def cmat(part):
        p = part.reshape(U_TILES, gpt, SSM_GROUP_CH, SSM_STATE)
        m = jnp.einsum('ugin,gh->ugnhi', p, eye)
        return m.reshape(U_TILES, gpt * SSM_STATE, LANES)

    bblk = jnp.concatenate([bmat(jnp.real(b_bar)), bmat(jnp.imag(b_bar))], axis=2).astype(BF16)
    cblk = jnp.concatenate([cmat(c_re.astype(F32)), cmat(-c_im.astype(F32))], axis=1).astype(BF16)
    are = jnp.real(a_bar).reshape(RE_TILES, LANES)
    aim = jnp.imag(a_bar).reshape(RE_TILES, LANES)
    return bblk, cblk, are, aim


BIAS_LANES = 8


def _split3(f):
    p1 = f.astype(BF16).astype(F32)
    r = f - p1
    p2 = r.astype(BF16).astype(F32)
    p3 = (r - p2).astype(BF16).astype(F32)
    return p1, p2, p3


def _bias_cols(f, key_side):
    n = f.shape[1]
    one = jnp.ones((1, n), F32)
    zero = jnp.zeros((2, n), F32)
    rows = []
    for h in range(2):
        pieces = [p[h:h + 1, :] for p in _split3(f)]
        rows += [one] * 3 + [-p for p in pieces] if key_side else pieces + [one] * 3
        rows.append(zero)
    pt = jnp.concatenate(rows, axis=0).astype(BF16)
    eye = (lax.broadcasted_iota(jnp.int32, (2 * BIAS_LANES, LANES), 0) ==
           lax.broadcasted_iota(jnp.int32, (2 * BIAS_LANES, LANES), 1)).astype(BF16)
    return lax.dot_general(pt, eye, (((0,), (0,)), ((), ())), preferred_element_type=F32)


V_ROWS = 144
ONES_ROW = 2 * FOX_HEAD_DIM
QUERY_SLAB = 256


def _fox_kernel(q_ref, k_ref, v_ref, frow_ref, o_ref, kaug_ref, qaug_ref, vt_ref, lhs_ref, s_ref, m_ref, acc_ref, *, t):
    i = pl.program_id(2)
    nq = pl.num_programs(2)
    seq = k_ref.shape[0]
    lane = lax.broadcasted_iota(jnp.int32, (t, LANES), 1)
    first = lane < FOX_HEAD_DIM

    def eye(rows, cols):
        return (lax.broadcasted_iota(jnp.int32, (rows, cols), 0) ==
                lax.broadcasted_iota(jnp.int32, (rows, cols), 1)).astype(BF16)

    def store_lhs(qi):
        r0 = pl.multiple_of(qi * t, t)
        q = q_ref[pl.ds(r0, t), :]
        bias = qaug_ref[pl.ds(r0, t), :]
        zero = jnp.zeros_like(q)
        fzero = jnp.zeros_like(bias)
        lhs_ref[:t, :] = jnp.concatenate(
            [jnp.where(first, q, zero), jnp.where(lane < BIAS_LANES, bias, fzero)], axis=1)
        lhs_ref[t:, :] = jnp.concatenate(
            [jnp.where(first, zero, q), jnp.where(lane < BIAS_LANES, fzero, bias)], axis=1)

    def reset_stats():
        m_ref[...] = jnp.full_like(m_ref, NEG_INF)
        acc_ref[...] = jnp.zeros_like(acc_ref)

    def step(jc, jp, causal):
        if jc is not None:
            vt = vt_ref[jc]
        if jp is not None:
            k0 = pl.multiple_of(jp * t, t)
            rk = jnp.concatenate([k_ref[pl.ds(k0, t), :], kaug_ref[pl.ds(k0, t), :]], axis=1)
        for c0 in range(0, 2 * t, QUERY_SLAB):
            sl = slice(c0, c0 + QUERY_SLAB)
            if jc is not None:
                s = s_ref[:, sl]
                m_old = m_ref[:, sl]
                m_new = jnp.maximum(m_old, jnp.max(s, axis=0, keepdims=True))
                alpha = jnp.exp2(m_old - m_new)
                p = jnp.exp2(s - m_new).astype(BF16)
                acc_ref[:, sl] = alpha * acc_ref[:, sl] + _dot(vt, p)
                m_ref[:, sl] = m_new
            if jp is not None:
                s = _dot_nt(rk, lhs_ref[sl, :])
                if causal:
                    key = lax.broadcasted_iota(jnp.int32, s.shape, 0)
                    qry = (lax.broadcasted_iota(jnp.int32, s.shape, 1) + c0) & (t - 1)
                    s = jnp.where(qry >= key, s, NEG_INF)
                s_ref[:, sl] = s

    @pl.when(i == 0)
    def _():
        kaug_ref[...] = _bias_cols(frow_ref[...], True).astype(BF16)
        qaug_ref[...] = _bias_cols(frow_ref[...], False).astype(BF16)
        vt = _dot_nt(eye(V_ROWS, LANES), v_ref[...])
        row = lax.broadcasted_iota(jnp.int32, vt.shape, 0)
        vt = jnp.where(row == ONES_ROW, 1.0, vt).astype(BF16)
        for jb in range(seq // t):
            vt_ref[jb] = vt[:, jb * t:(jb + 1) * t]
        store_lhs(0)
        reset_stats()
        step(None, 0, True)

    n_steady = jnp.maximum(i - 1, 0)

    def steady_pair(jj, _):
        j = 2 * jj
        step(j, j + 1, False)
        step(j + 1, j + 2, False)
        return 0
    lax.fori_loop(0, n_steady >> 1, steady_pair, 0)

    @pl.when((n_steady & 1) == 1)
    def _():
        step(n_steady - 1, n_steady, False)

    @pl.when(i > 0)
    def _():
        step(i - 1, i, True)

    store_lhs(jnp.minimum(i + 1, nq - 1))
    step(i, 0, False)

    acc = acc_ref[...]
    inv_l = 1.0 / acc[ONES_ROW:ONES_ROW + 1, :]
    out_t = jnp.concatenate([acc[:FOX_HEAD_DIM, :t] * inv_l[:, :t],
                             acc[FOX_HEAD_DIM:ONES_ROW, t:] * inv_l[:, t:]], axis=0)
    o_ref[...] = lax.dot_general(out_t.astype(BF16), eye(LANES, LANES), (((0,), (0,)), ((), ())),
                                 preferred_element_type=F32).astype(BF16)
    reset_stats()


def _fox(q, k, v, frow, t):
    b, s, w = q.shape
    assert s % t == 0 and t & (t - 1) == 0
    kern = functools.partial(_fox_kernel, t=t)
    whole = lambda bi, p, i: (bi, 0, p)
    return pl.pallas_call(
        kern,
        grid=(b, FOX_PAIRS, s // t),
        in_specs=[
            pl.BlockSpec((None, s, LANES), whole),
            pl.BlockSpec((None, s, LANES), whole),
            pl.BlockSpec((None, s, LANES), whole),
            pl.BlockSpec((None, None, 2, s), lambda bi, p, i: (bi, p, 0, 0)),
        ],
        out_specs=pl.BlockSpec((None, t, LANES), lambda bi, p, i: (bi, i, p)),
        out_shape=jax.ShapeDtypeStruct((b, s, w), BF16),
        scratch_shapes=[
            pltpu.VMEM((s, LANES), BF16),
            pltpu.VMEM((s, LANES), BF16),
            pltpu.VMEM((s // t, V_ROWS, t), BF16),
            pltpu.VMEM((2 * t, 2 * LANES), BF16),
            pltpu.VMEM((t, 2 * t), F32),
            pltpu.VMEM((1, 2 * t), F32),
            pltpu.VMEM((V_ROWS, 2 * t), F32),
        ],
        compiler_params=pltpu.CompilerParams(
            dimension_semantics=("arbitrary", "arbitrary", "arbitrary"),
            vmem_limit_bytes=VMEM_LIMIT_BYTES),
        name="fox",
    )(q, k, v, frow)


TAIL_SUB_ROWS = 256
FFN_CHUNK = 1024


def _tail_kernel(x_ref, ys_ref, yf_ref, kx_ref, vx_ref,
                 fox_g_ref, w_out_ref, mix_post_ref, xa_pre_ref, wq_ref, wo_ref, xa_post_ref,
                 ffn_pre_ref, wg_ref, wu_ref, wd_ref, ffn_post_ref, o_ref, *, n_sub):
    sub = x_ref.shape[0] // n_sub
    rows = [slice(r * sub, (r + 1) * sub) for r in range(n_sub)]
    each = lambda f, *lists: [f(*a) for a in zip(*lists)]

    x = [x_ref[r, :] for r in rows]
    yf = [_rms(yf_ref[r, :].astype(F32), fox_g_ref[...]).astype(BF16) for r in rows]
    mix = [_dot(ys_ref[r, :], w_out_ref[:SSM_WIDTH, :]) + _dot(f, w_out_ref[SSM_WIDTH:, :])
           for r, f in zip(rows, yf)]
    x = each(lambda xi, mi: xi + _rms(mi, mix_post_ref[...]), x, mix)

    h = [_rms(xi, xa_pre_ref[...]).astype(BF16) for xi in x]
    q = [_dot(hi, wq_ref[...]).astype(BF16) for hi in h]
    heads = [[] for _ in rows]
    for hd in range(XA_HEADS):
        sl = slice(hd * XA_HEAD_DIM, (hd + 1) * XA_HEAD_DIM)
        for qi, out in zip(q, heads):
            s = _dot_nt(qi[:, sl], kx_ref[:, sl])
            p = jnp.exp(s - jnp.max(s, axis=-1, keepdims=True))
            p = p / jnp.sum(p, axis=-1, keepdims=True)
            out.append(_dot(p.astype(BF16), vx_ref[:, sl]).astype(BF16))
    o = [jnp.concatenate(hs, axis=1) for hs in heads]
    x = each(lambda xi, oi: xi + _rms(_dot(oi, wo_ref[...]), xa_post_ref[...]), x, o)

    h = [_rms(xi, ffn_pre_ref[...]).astype(BF16) for xi in x]
    hidden = wg_ref.shape[1]
    down = [None] * n_sub
    for c0 in range(0, hidden, FFN_CHUNK):
        c1 = min(c0 + FFN_CHUNK, hidden)
        for r, hi in enumerate(h):
            gate = _dot(hi, wg_ref[:, c0:c1])
            up = _dot(hi, wu_ref[:, c0:c1])
            act = (gate * _sigmoid(gate) * up).astype(BF16)
            part = _dot(act, wd_ref[c0:c1, :])
            down[r] = part if down[r] is None else down[r] + part
    for r, xi, di in zip(rows, x, down):
        o_ref[r, :] = xi + _rms(di, ffn_post_ref[...])


def _tail(x, ys, yf, kx, vx, fox_g, w_out, mix_post_g, xa_pre_g, wq, wo, xa_post_g,
          ffn_pre_g, wg, wu, wd, ffn_post_g, tm):
    b, s, d = x.shape
    m = kx.shape[1]
    row = lambda i, j: (i, j, 0)
    per_b = lambda i, j: (i, 0, 0)
    const = lambda i, j: (0, 0)

    def resident(a):
        return pl.BlockSpec(a.shape, const, pipeline_mode=pl.Buffered(1))

    gains = [g.reshape(1, -1) for g in (fox_g, mix_post_g, xa_pre_g, xa_post_g, ffn_pre_g, ffn_post_g)]
    fox_g, mix_post_g, xa_pre_g, xa_post_g, ffn_pre_g, ffn_post_g = gains
    args = (x, ys, yf, kx, vx, fox_g, w_out, mix_post_g, xa_pre_g, wq, wo, xa_post_g,
            ffn_pre_g, wg, wu, wd, ffn_post_g)
    in_specs = [
        pl.BlockSpec((None, tm, d), row),
        pl.BlockSpec((None, tm, SSM_WIDTH), row),
        pl.BlockSpec((None, tm, FOX_WIDTH), row),
        pl.BlockSpec((None, m, d), per_b),
        pl.BlockSpec((None, m, d), per_b),
    ] + [resident(a) for a in args[5:]]
    return pl.pallas_call(
        functools.partial(_tail_kernel, n_sub=tm // TAIL_SUB_ROWS),
        grid=(b, s // tm),
        in_specs=in_specs,
        out_specs=pl.BlockSpec((None, tm, d), row),
        out_shape=jax.ShapeDtypeStruct((b, s, d), F32),
        compiler_params=pltpu.CompilerParams(
            dimension_semantics=("arbitrary", "arbitrary"), vmem_limit_bytes=VMEM_LIMIT_BYTES),
        name="tail",
    )(*args)


def _pick(n, pref):
    t = min(n, pref)
    assert n % t == 0, (n, t)
    return t


def kernel(x, mem, mix_pre_g, w_in, ssm_a_re, ssm_a_im, ssm_log_dt, ssm_b_re, ssm_b_im, ssm_c_re, ssm_c_im, ssm_d, ssm_glu_w, ssm_glu_b, fox_f_bias, ssm_out_g, fox_out_g, w_out, mix_post_g, xa_pre_g, mem_g, xa_wq, xa_wkv, xa_wo, xa_post_g, ffn_pre_g, w_gate, w_up, w_down, ffn_post_g):
    b, s, d = x.shape
    assert d == D_MODEL and s % LANES == 0
    n_main = SSM_WIDTH + 3 * FOX_WIDTH

    kx, vx = _memkv(mem, mem_g, xa_wkv.astype(BF16))

    u, q, k, v, fcum = _inproj(
        x, mix_pre_g, w_in[:, :n_main].astype(BF16), w_in[:, n_main:].T.astype(BF16),
        fox_f_bias, tm=_pick(s, 512))

    bblk, cblk, are, aim = _ssm_params(ssm_a_re, ssm_a_im, ssm_log_dt, ssm_b_re, ssm_b_im,
                                       ssm_c_re, ssm_c_im)
    y_ssm = _ssm(u, bblk, cblk, are, aim, ssm_d, ssm_glu_w.astype(BF16), ssm_glu_b, ssm_out_g,
                 tm=_pick(s, 128))

    y_fox = _fox(q, k, v, fcum.reshape(b, FOX_PAIRS, 2, s), t=_pick(s, 512))

    return _tail(x, y_ssm, y_fox, kx, vx, fox_out_g, w_out.astype(BF16), mix_post_g, xa_pre_g,
                 xa_wq.astype(BF16), xa_wo.astype(BF16), xa_post_g, ffn_pre_g,
                 w_gate.astype(BF16), w_up.astype(BF16), w_down.astype(BF16), ffn_post_g,
                 tm=_pick(s, 2 * TAIL_SUB_ROWS))
```

```python
import functools
import math

import jax
import jax.numpy as jnp
from jax import lax
from jax.experimental import pallas as pl
from jax.experimental.pallas import tpu as pltpu

F32 = jnp.float32
BF16 = jnp.bfloat16

LANES = 128
SUBLANES = 8
VMEM_LIMIT_BYTES = 56 * 1024 * 1024

D_MODEL = 1024
SSM_WIDTH = 512
SSM_GROUP_CH = 16
SSM_GROUPS = 32
SSM_STATE = 64
FOX_WIDTH = 512
FOX_HEAD_DIM = 64
FOX_HEADS = 8
FOX_PAIRS = FOX_HEADS * FOX_HEAD_DIM // LANES
XA_HEADS = 4
XA_HEAD_DIM = 256
RMS_EPS = 1e-6
NEG_INF = -1e30
LOG2E = math.log2(math.e)

STATE_TILES = 2 * SSM_GROUPS * SSM_STATE // LANES
RE_TILES = STATE_TILES // 2
U_TILES = SSM_WIDTH // LANES


def _rms(x, g):
    return x * lax.rsqrt(jnp.mean(x * x, axis=-1, keepdims=True) + RMS_EPS) * g


def _dot(a, b):
    return jnp.dot(a, b, preferred_element_type=F32)


def _dot_nt(a, b):
    return lax.dot_general(a, b, (((1,), (1,)), ((), ())), preferred_element_type=F32)


def _sigmoid(z):
    return 1.0 / (1.0 + jnp.exp(-z))


def _memkv_kernel(mem_ref, g_ref, w_ref, k_ref, v_ref):
    mn = _rms(mem_ref[...], g_ref[...]).astype(BF16)
    kv = _dot(mn, w_ref[...])
    k_ref[...] = (kv[:, :D_MODEL] * (1.0 / math.sqrt(XA_HEAD_DIM))).astype(BF16)
    v_ref[...] = kv[:, D_MODEL:].astype(BF16)


def _memkv(mem, mem_g, wkv):
    b, m, d = mem.shape
    return pl.pallas_call(
        _memkv_kernel,
        grid=(b,),
        in_specs=[
            pl.BlockSpec((None, m, d), lambda i: (i, 0, 0)),
            pl.BlockSpec((1, d), lambda i: (0, 0)),
            pl.BlockSpec((d, 2 * d), lambda i: (0, 0)),
        ],
        out_specs=[
            pl.BlockSpec((None, m, d), lambda i: (i, 0, 0)),
            pl.BlockSpec((None, m, d), lambda i: (i, 0, 0)),
        ],
        out_shape=[jax.ShapeDtypeStruct((b, m, d), BF16)] * 2,
        compiler_params=pltpu.CompilerParams(
            dimension_semantics=("arbitrary",), vmem_limit_bytes=VMEM_LIMIT_BYTES),
        name="memkv",
    )(mem, mem_g.reshape(1, d), wkv)


def _inproj_kernel(x_ref, g_ref, w_ref, wf_ref, fb_ref, tri_ref,
                   u_ref, q_ref, k_ref, v_ref, f_ref, carry_ref):
    @pl.when(pl.program_id(1) == 0)
    def _():
        carry_ref[...] = jnp.zeros_like(carry_ref)

    hb = _rms(x_ref[...], g_ref[...]).astype(BF16)
    proj = _dot(hb, w_ref[...])
    u_ref[...] = proj[:, :SSM_WIDTH].astype(BF16)
    o = SSM_WIDTH
    q_ref[...] = (proj[:, o:o + FOX_WIDTH] * (LOG2E / math.sqrt(FOX_HEAD_DIM))).astype(BF16)
    k_ref[...] = proj[:, o + FOX_WIDTH:o + 2 * FOX_WIDTH].astype(BF16)
    v_ref[...] = proj[:, o + 2 * FOX_WIDTH:].astype(BF16)

    z = _dot_nt(wf_ref[...], hb) + fb_ref[...]
    logf = jnp.minimum(z, 0.0) - jnp.log1p(jnp.exp(-jnp.abs(z)))
    p1 = logf.astype(BF16)
    r1 = logf - p1.astype(F32)
    p2 = r1.astype(BF16)
    p3 = (r1 - p2.astype(F32)).astype(BF16)
    tri = tri_ref[...]
    csum = _dot(p1, tri) + _dot(p2, tri) + _dot(p3, tri)
    f = csum + carry_ref[...][:, :1]
    f_ref[...] = f * LOG2E
    carry_ref[...] = jnp.broadcast_to(f[:, -1:], carry_ref.shape)


def _inproj(x, mix_pre_g, w_main, wf_t, f_bias, tm):
    b, s, d = x.shape
    nt = s // tm
    tri = jnp.triu(jnp.ones((tm, tm), F32)).astype(BF16)
    row = lambda i, j: (i, j, 0)
    const2 = lambda i, j: (0, 0)
    return pl.pallas_call(
        _inproj_kernel,
        grid=(b, nt),
        in_specs=[
            pl.BlockSpec((None, tm, d), row),
            pl.BlockSpec((1, d), const2),
            pl.BlockSpec(w_main.shape, const2),
            pl.BlockSpec(wf_t.shape, const2),
            pl.BlockSpec((FOX_HEADS, 1), const2),
            pl.BlockSpec((tm, tm), const2),
        ],
        out_specs=[
            pl.BlockSpec((None, tm, SSM_WIDTH), row),
            pl.BlockSpec((None, tm, FOX_WIDTH), row),
            pl.BlockSpec((None, tm, FOX_WIDTH), row),
            pl.BlockSpec((None, tm, FOX_WIDTH), row),
            pl.BlockSpec((None, FOX_HEADS, tm), lambda i, j: (i, 0, j)),
        ],
        out_shape=[
            jax.ShapeDtypeStruct((b, s, SSM_WIDTH), BF16),
            jax.ShapeDtypeStruct((b, s, FOX_WIDTH), BF16),
            jax.ShapeDtypeStruct((b, s, FOX_WIDTH), BF16),
            jax.ShapeDtypeStruct((b, s, FOX_WIDTH), BF16),
            jax.ShapeDtypeStruct((b, FOX_HEADS, s), F32),
        ],
        scratch_shapes=[pltpu.VMEM((FOX_HEADS, LANES), F32)],
        compiler_params=pltpu.CompilerParams(
            dimension_semantics=("arbitrary", "arbitrary"), vmem_limit_bytes=VMEM_LIMIT_BYTES),
        name="inproj",
    )(x, mix_pre_g.reshape(1, d), w_main, wf_t, f_bias.reshape(FOX_HEADS, 1), tri)


def _gelu_tanh(x):
    c = math.sqrt(2.0 / math.pi)
    return 0.5 * x * (1.0 + jnp.tanh(c * (x + 0.044715 * (x * x * x))))


TILE_PITCH = 20
TIME_PITCH = 324


def _ssm_kernel(u_ref, bblk_ref, cblk_ref, are_ref, aim_ref, d_ref, gw_ref, gb_ref, g_ref,
                o_ref, s_ref, xc_ref, *, nb, tm):
    @pl.when(pl.program_id(0) == 0)
    def _():
        xc_ref[...] = jnp.zeros_like(xc_ref)

    def rows_of_tile(b, part, k):
        return pl.ds(TILE_PITCH * k + 2 * b + part, tm, stride=TIME_PITCH)

    for ut in range(U_TILES):
        ub = u_ref[:, :, ut * LANES:(ut + 1) * LANES].reshape(nb * tm, LANES)
        res = _dot(ub, bblk_ref[ut])
        for b in range(nb):
            for j in range(2 * U_TILES):
                part, k = divmod(j, U_TILES)
                s_ref[rows_of_tile(b, part, U_TILES * ut + k), :] = (
                    res[b * tm:(b + 1) * tm, j * LANES:(j + 1) * LANES])

    ar = are_ref[...]
    ai = aim_ref[...]

    def step(t, carry):
        new = []
        for b in range(nb):
            xr, xi = carry[2 * b], carry[2 * b + 1]
            ire = pl.ds(TIME_PITCH * t + 2 * b, RE_TILES, stride=TILE_PITCH)
            iim = pl.ds(TIME_PITCH * t + 2 * b + 1, RE_TILES, stride=TILE_PITCH)
            nr = ar * xr - ai * xi + s_ref[ire, :]
            ni = ar * xi + ai * xr + s_ref[iim, :]
            s_ref[ire, :] = nr
            s_ref[iim, :] = ni
            new += [nr, ni]
        return tuple(new)

    init = []
    for b in range(nb):
        init.append(xc_ref[b * STATE_TILES:b * STATE_TILES + RE_TILES, :])
        init.append(xc_ref[b * STATE_TILES + RE_TILES:(b + 1) * STATE_TILES, :])
    fin = lax.fori_loop(0, tm, step, tuple(init))
    for b in range(nb):
        xc_ref[b * STATE_TILES:b * STATE_TILES + RE_TILES, :] = fin[2 * b]
        xc_ref[b * STATE_TILES + RE_TILES:(b + 1) * STATE_TILES, :] = fin[2 * b + 1]

    ys = []
    for ot in range(U_TILES):
        per_seq = []
        for b in range(nb):
            tiles = [s_ref[rows_of_tile(b, part, U_TILES * ot + k), :].astype(BF16)
                     for part in range(2) for k in range(U_TILES)]
            per_seq.append(jnp.concatenate(tiles, axis=1))
        ys.append(_dot(jnp.concatenate(per_seq, axis=0), cblk_ref[ot]))
    u_all = u_ref[...].reshape(nb * tm, SSM_WIDTH).astype(F32)
    y = jnp.concatenate(ys, axis=1) + d_ref[...] * u_all
    g = _gelu_tanh(y)
    out = g * _sigmoid(_dot(g.astype(BF16), gw_ref[...]) + gb_ref[...])
    o_ref[...] = _rms(out, g_ref[...]).astype(BF16).reshape(nb, tm, SSM_WIDTH)


def _ssm(u, bblk, cblk, a_re, a_im, d_skip, glu_w, glu_b, out_g, tm):
    b, s, w = u.shape
    assert 2 * b <= TILE_PITCH - 4 and TIME_PITCH >= TILE_PITCH * (RE_TILES - 1) + 2 * b
    kern = functools.partial(_ssm_kernel, nb=b, tm=tm)
    c2 = lambda t: (0, 0)
    c3 = lambda t: (0, 0, 0)
    return pl.pallas_call(
        kern,
        grid=(s // tm,),
        in_specs=[
            pl.BlockSpec((b, tm, w), lambda t: (0, t, 0)),
            pl.BlockSpec(bblk.shape, c3),
            pl.BlockSpec(cblk.shape, c3),
            pl.BlockSpec(a_re.shape, c2),
            pl.BlockSpec(a_im.shape, c2),
            pl.BlockSpec((1, w), c2),
            pl.BlockSpec(glu_w.shape, c2),
            pl.BlockSpec((1, w), c2),
            pl.BlockSpec((1, w), c2),
        ],
        out_specs=pl.BlockSpec((b, tm, w), lambda t: (0, t, 0)),
        out_shape=jax.ShapeDtypeStruct((b, s, w), BF16),
        scratch_shapes=[
            pltpu.VMEM((tm * TIME_PITCH, LANES), F32),
            pltpu.VMEM((b * STATE_TILES, LANES), F32),
        ],
        compiler_params=pltpu.CompilerParams(
            dimension_semantics=("arbitrary",), vmem_limit_bytes=VMEM_LIMIT_BYTES),
        name="ssm",
    )(u, bblk, cblk, a_re, a_im, d_skip.reshape(1, w), glu_w, glu_b.reshape(1, w),
      out_g.reshape(1, w))


def _ssm_params(a_re, a_im, log_dt, b_re, b_im, c_re, c_im):
    a = lax.complex(a_re.astype(F32), a_im.astype(F32))
    dt = jnp.exp(log_dt.astype(F32))[:, None]
    a_bar = jnp.exp(a * dt)
    b_bar = ((a_bar - 1.0) / a)[..., None] * lax.complex(b_re.astype(F32), b_im.astype(F32))
    gpt = LANES // SSM_GROUP_CH
    eye = jnp.eye(gpt, dtype=F32)

    def bmat(part):
        p = part.reshape(U_TILES, gpt, SSM_STATE, SSM_GROUP_CH)
        m = jnp.einsum('ugni,gh->ugihn', p, eye)
        return m.reshape(U_TILES, LANES, gpt * SSM_STATE)

    def cmat(part):
        p = part.reshape(U_TILES, gpt, SSM_GROUP_CH, SSM_STATE)
        m = jnp.einsum('ugin,gh->ugnhi', p, eye)
        return m.reshape(U_TILES, gpt * SSM_STATE, LANES)

    bblk = jnp.concatenate([bmat(jnp.real(b_bar)), bmat(jnp.imag(b_bar))], axis=2).astype(BF16)
    cblk = jnp.concatenate([cmat(c_re.astype(F32)), cmat(-c_im.astype(F32))], axis=1).astype(BF16)
    are = jnp.real(a_bar).reshape(RE_TILES, LANES)
    aim = jnp.imag(a_bar).reshape(RE_TILES, LANES)
    return bblk, cblk, are, aim


BIAS_LANES = 8


def _split3(f):
    p1 = f.astype(BF16).astype(F32)
    r = f - p1
    p2 = r.astype(BF16).astype(F32)
    p3 = (r - p2).astype(BF16).astype(F32)
    return p1, p2, p3


def _bias_cols(f, key_side):
    n = f.shape[1]
    one = jnp.ones((1, n), F32)
    zero = jnp.zeros((2, n), F32)
    rows = []
    for h in range(2):
        pieces = [p[h:h + 1, :] for p in _split3(f)]
        rows += [one] * 3 + [-p for p in pieces] if key_side else pieces + [one] * 3
        rows.append(zero)
    pt = jnp.concatenate(rows, axis=0).astype(BF16)
    eye = (lax.broadcasted_iota(jnp.int32, (2 * BIAS_LANES, LANES), 0) ==
           lax.broadcasted_iota(jnp.int32, (2 * BIAS_LANES, LANES), 1)).astype(BF16)
    return lax.dot_general(pt, eye, (((0,), (0,)), ((), ())), preferred_element_type=F32)


V_ROWS = 144
ONES_ROW = 2 * FOX_HEAD_DIM
QUERY_SLAB = 256
SKIP_EXPONENT = 160.0


def _fox_kernel(q_ref, k_ref, v_ref, frow_ref, o_ref, kaug_ref, qaug_ref, vt_ref, j0_ref,
                lhs_ref, s_ref, m_ref, acc_ref, *, t):
    i = pl.program_id(2)
    nq = pl.num_programs(2)
    seq = k_ref.shape[0]
    lane = lax.broadcasted_iota(jnp.int32, (t, LANES), 1)
    first = lane < FOX_HEAD_DIM

    def eye(rows, cols):
        return (lax.broadcasted_iota(jnp.int32, (rows, cols), 0) ==
                lax.broadcasted_iota(jnp.int32, (rows, cols), 1)).astype(BF16)

    def store_lhs(qi):
        r0 = pl.multiple_of(qi * t, t)
        q = q_ref[pl.ds(r0, t), :]
        bias = qaug_ref[pl.ds(r0, t), :]
        zero = jnp.zeros_like(q)
        fzero = jnp.zeros_like(bias)
        lhs_ref[:t, :] = jnp.concatenate(
            [jnp.where(first, q, zero), jnp.where(lane < BIAS_LANES, bias, fzero)], axis=1)
        lhs_ref[t:, :] = jnp.concatenate(
            [jnp.where(first, zero, q), jnp.where(lane < BIAS_LANES, fzero, bias)], axis=1)

    def reset_stats():
        m_ref[...] = jnp.full_like(m_ref, NEG_INF)
        acc_ref[...] = jnp.zeros_like(acc_ref)

    def head_norms2(x_ref):
        xt = _dot_nt(eye(LANES, LANES), x_ref[...])
        sq = xt * xt
        return jnp.concatenate([jnp.sum(sq[:FOX_HEAD_DIM], axis=0, keepdims=True),
                                jnp.sum(sq[FOX_HEAD_DIM:], axis=0, keepdims=True)], axis=0)

    def first_block(qi, qn2, kmax2):
        r0 = qi * t
        qk = jnp.sqrt(jnp.max(qn2[:, r0:r0 + t], axis=1, keepdims=True) * kmax2)
        f = frow_ref[...]
        tau = f[:, r0:r0 + 1] + 2.0 * qk + SKIP_EXPONENT
        pos = lax.broadcasted_iota(jnp.int32, (1, seq), 1)
        far = (f[0:1, :] > tau[0:1, :]) & (f[1:2, :] > tau[1:2, :])
        ends = ((pos & (t - 1)) == t - 1) & (pos < r0)
        return jnp.sum((far & ends).astype(jnp.int32))

    def step(jc, jp, causal):
        if jc is not None:
            vt = vt_ref[jc]
        if jp is not None:
            k0 = pl.multiple_of(jp * t, t)
            rk = jnp.concatenate([k_ref[pl.ds(k0, t), :], kaug_ref[pl.ds(k0, t), :]], axis=1)
        for c0 in range(0, 2 * t, QUERY_SLAB):
            sl = slice(c0, c0 + QUERY_SLAB)
            if jc is not None:
                s = s_ref[:, sl]
                m_old = m_ref[:, sl]
                m_new = jnp.maximum(m_old, jnp.max(s, axis=0, keepdims=True))
                alpha = jnp.exp2(m_old - m_new)
                p = jnp.exp2(s - m_new).astype(BF16)
                acc_ref[:, sl] = alpha * acc_ref[:, sl] + _dot(vt, p)
                m_ref[:, sl] = m_new
            if jp is not None:
                s = _dot_nt(rk, lhs_ref[sl, :])
                if causal is not False:
                    ahead = 0 if causal is True else causal
                    key = lax.broadcasted_iota(jnp.int32, s.shape, 0)
                    qry = (lax.broadcasted_iota(jnp.int32, s.shape, 1) + c0) & (t - 1)
                    s = jnp.where(qry + ahead >= key, s, NEG_INF)
                s_ref[:, sl] = s

    @pl.when(i == 0)
    def _():
        kaug_ref[...] = _bias_cols(frow_ref[...], True).astype(BF16)
        qaug_ref[...] = _bias_cols(frow_ref[...], False).astype(BF16)
        qn2 = head_norms2(q_ref)
        kmax2 = jnp.max(head_norms2(k_ref), axis=1, keepdims=True)
        for qi in range(seq // t):
            j0_ref[qi] = first_block(qi, qn2, kmax2)
        vt = _dot_nt(eye(V_ROWS, LANES), v_ref[...])
        row = lax.broadcasted_iota(jnp.int32, vt.shape, 0)
        vt = jnp.where(row == ONES_ROW, 1.0, vt).astype(BF16)
        for jb in range(seq // t):
            vt_ref[jb] = vt[:, jb * t:(jb + 1) * t]
        store_lhs(0)
        reset_stats()
        step(None, 0, True)

    j0 = j0_ref[i]
    n_steady = jnp.maximum(i - 1 - j0, 0)

    def steady_pair(jj, _):
        j = j0 + 2 * jj
        step(j, j + 1, False)
        step(j + 1, j + 2, False)
        return 0
    lax.fori_loop(0, n_steady >> 1, steady_pair, 0)

    @pl.when((n_steady & 1) == 1)
    def _():
        step(i - 2, i - 1, False)

    @pl.when(i > j0)
    def _():
        step(i - 1, i, True)

    nxt = jnp.minimum(i + 1, nq - 1)
    j0_nxt = j0_ref[nxt]
    store_lhs(nxt)
    step(i, j0_nxt, (nxt - j0_nxt) * t)

    acc = acc_ref[...]
    inv_l = 1.0 / acc[ONES_ROW:ONES_ROW + 1, :]
    out_t = jnp.concatenate([acc[:FOX_HEAD_DIM, :t] * inv_l[:, :t],
                             acc[FOX_HEAD_DIM:ONES_ROW, t:] * inv_l[:, t:]], axis=0)
    o_ref[...] = lax.dot_general(out_t.astype(BF16), eye(LANES, LANES), (((0,), (0,)), ((), ())),
                                 preferred_element_type=F32).astype(BF16)
    reset_stats()


def _fox(q, k, v, frow, t):
    b, s, w = q.shape
    assert s % t == 0 and t & (t - 1) == 0
    kern = functools.partial(_fox_kernel, t=t)
    whole = lambda bi, p, i: (bi, 0, p)
    return pl.pallas_call(
        kern,
        grid=(b, FOX_PAIRS, s // t),
        in_specs=[
            pl.BlockSpec((None, s, LANES), whole),
            pl.BlockSpec((None, s, LANES), whole),
            pl.BlockSpec((None, s, LANES), whole),
            pl.BlockSpec((None, None, 2, s), lambda bi, p, i: (bi, p, 0, 0)),
        ],
        out_specs=pl.BlockSpec((None, t, LANES), lambda bi, p, i: (bi, i, p)),
        out_shape=jax.ShapeDtypeStruct((b, s, w), BF16),
        scratch_shapes=[
            pltpu.VMEM((s, LANES), BF16),
            pltpu.VMEM((s, LANES), BF16),
            pltpu.VMEM((s // t, V_ROWS, t), BF16),
            pltpu.SMEM((s // t,), jnp.int32),
            pltpu.VMEM((2 * t, 2 * LANES), BF16),
            pltpu.VMEM((t, 2 * t), F32),
            pltpu.VMEM((1, 2 * t), F32),
            pltpu.VMEM((V_ROWS, 2 * t), F32),
        ],
        compiler_params=pltpu.CompilerParams(
            dimension_semantics=("arbitrary", "arbitrary", "arbitrary"),
            vmem_limit_bytes=VMEM_LIMIT_BYTES),
        name="fox",
    )(q, k, v, frow)


TAIL_SUB_ROWS = 256
FFN_CHUNK = 1024


def _tail_kernel(x_ref, ys_ref, yf_ref, kx_ref, vx_ref,
                 fox_g_ref, w_out_ref, mix_post_ref, xa_pre_ref, wq_ref, wo_ref, xa_post_ref,
                 ffn_pre_ref, wg_ref, wu_ref, wd_ref, ffn_post_ref, o_ref, *, n_sub):
    sub = x_ref.shape[0] // n_sub
    rows = [slice(r * sub, (r + 1) * sub) for r in range(n_sub)]
    each = lambda f, *lists: [f(*a) for a in zip(*lists)]

    x = [x_ref[r, :] for r in rows]
    yf = [_rms(yf_ref[r, :].astype(F32), fox_g_ref[...]).astype(BF16) for r in rows]
    mix = [_dot(ys_ref[r, :], w_out_ref[:SSM_WIDTH, :]) + _dot(f, w_out_ref[SSM_WIDTH:, :])
           for r, f in zip(rows, yf)]
    x = each(lambda xi, mi: xi + _rms(mi, mix_post_ref[...]), x, mix)

    h = [_rms(xi, xa_pre_ref[...]).astype(BF16) for xi in x]
    q = [_dot(hi, wq_ref[...]).astype(BF16) for hi in h]
    heads = [[] for _ in rows]
    for hd in range(XA_HEADS):
        sl = slice(hd * XA_HEAD_DIM, (hd + 1) * XA_HEAD_DIM)
        for qi, out in zip(q, heads):
            s = _dot_nt(qi[:, sl], kx_ref[:, sl])
            p = jnp.exp(s - jnp.max(s, axis=-1, keepdims=True))
            p = p / jnp.sum(p, axis=-1, keepdims=True)
            out.append(_dot(p.astype(BF16), vx_ref[:, sl]).astype(BF16))
    o = [jnp.concatenate(hs, axis=1) for hs in heads]
    x = each(lambda xi, oi: xi + _rms(_dot(oi, wo_ref[...]), xa_post_ref[...]), x, o)

    h = [_rms(xi, ffn_pre_ref[...]).astype(BF16) for xi in x]
    hidden = wg_ref.shape[1]
    down = [None] * n_sub
    for c0 in range(0, hidden, FFN_CHUNK):
        c1 = min(c0 + FFN_CHUNK, hidden)
        for r, hi in enumerate(h):
            gate = _dot(hi, wg_ref[:, c0:c1])
            up = _dot(hi, wu_ref[:, c0:c1])
            act = (gate * _sigmoid(gate) * up).astype(BF16)
            part = _dot(act, wd_ref[c0:c1, :])
            down[r] = part if down[r] is None else down[r] + part
    for r, xi, di in zip(rows, x, down):
        o_ref[r, :] = xi + _rms(di, ffn_post_ref[...])


def _tail(x, ys, yf, kx, vx, fox_g, w_out, mix_post_g, xa_pre_g, wq, wo, xa_post_g,
          ffn_pre_g, wg, wu, wd, ffn_post_g, tm):
    b, s, d = x.shape
    m = kx.shape[1]
    row = lambda i, j: (i, j, 0)
    per_b = lambda i, j: (i, 0, 0)
    const = lambda i, j: (0, 0)

    def resident(a):
        return pl.BlockSpec(a.shape, const, pipeline_mode=pl.Buffered(1))

    gains = [g.reshape(1, -1) for g in (fox_g, mix_post_g, xa_pre_g, xa_post_g, ffn_pre_g, ffn_post_g)]
    fox_g, mix_post_g, xa_pre_g, xa_post_g, ffn_pre_g, ffn_post_g = gains
    args = (x, ys, yf, kx, vx, fox_g, w_out, mix_post_g, xa_pre_g, wq, wo, xa_post_g,
            ffn_pre_g, wg, wu, wd, ffn_post_g)
    in_specs = [
        pl.BlockSpec((None, tm, d), row),
        pl.BlockSpec((None, tm, SSM_WIDTH), row),
        pl.BlockSpec((None, tm, FOX_WIDTH), row),
        pl.BlockSpec((None, m, d), per_b),
        pl.BlockSpec((None, m, d), per_b),
    ] + [resident(a) for a in args[5:]]
    return pl.pallas_call(
        functools.partial(_tail_kernel, n_sub=tm // TAIL_SUB_ROWS),
        grid=(b, s // tm),
        in_specs=in_specs,
        out_specs=pl.BlockSpec((None, tm, d), row),
        out_shape=jax.ShapeDtypeStruct((b, s, d), F32),
        compiler_params=pltpu.CompilerParams(
            dimension_semantics=("arbitrary", "arbitrary"), vmem_limit_bytes=VMEM_LIMIT_BYTES),
        name="tail",
    )(*args)


def _pick(n, pref):
    t = min(n, pref)
    assert n % t == 0, (n, t)
    return t


def kernel(x, mem, mix_pre_g, w_in, ssm_a_re, ssm_a_im, ssm_log_dt, ssm_b_re, ssm_b_im, ssm_c_re, ssm_c_im, ssm_d, ssm_glu_w, ssm_glu_b, fox_f_bias, ssm_out_g, fox_out_g, w_out, mix_post_g, xa_pre_g, mem_g, xa_wq, xa_wkv, xa_wo, xa_post_g, ffn_pre_g, w_gate, w_up, w_down, ffn_post_g):
    b, s, d = x.shape
    assert d == D_MODEL and s % LANES == 0
    n_main = SSM_WIDTH + 3 * FOX_WIDTH

    kx, vx = _memkv(mem, mem_g, xa_wkv.astype(BF16))

    u, q, k, v, fcum = _inproj(
        x, mix_pre_g, w_in[:, :n_main].astype(BF16), w_in[:, n_main:].T.astype(BF16),
        fox_f_bias, tm=_pick(s, 512))

    bblk, cblk, are, aim = _ssm_params(ssm_a_re, ssm_a_im, ssm_log_dt, ssm_b_re, ssm_b_im,
                                       ssm_c_re, ssm_c_im)
    y_ssm = _ssm(u, bblk, cblk, are, aim, ssm_d, ssm_glu_w.astype(BF16), ssm_glu_b, ssm_out_g,
                 tm=_pick(s, 128))

    y_fox = _fox(q, k, v, fcum.reshape(b, FOX_PAIRS, 2, s), t=_pick(s, 512))

    return _tail(x, y_ssm, y_fox, kx, vx, fox_out_g, w_out.astype(BF16), mix_post_g, xa_pre_g,
                 xa_wq.astype(BF16), xa_wo.astype(BF16), xa_post_g, ffn_pre_g,
                 w_gate.astype(BF16), w_up.astype(BF16), w_down.astype(BF16), ffn_post_g,
                 tm=_pick(s, 2 * TAIL_SUB_ROWS))
```

```python
import functools
import math

import jax
import jax.numpy as jnp
from jax import lax
from jax.experimental import pallas as pl
from jax.experimental.pallas import tpu as pltpu

F32 = jnp.float32
BF16 = jnp.bfloat16

LANES = 128
SUBLANES = 8
VMEM_LIMIT_BYTES = 56 * 1024 * 1024

D_MODEL = 1024
SSM_WIDTH = 512
SSM_GROUP_CH = 16
SSM_GROUPS = 32
SSM_STATE = 64
FOX_WIDTH = 512
FOX_HEAD_DIM = 64
FOX_HEADS = 8
FOX_PAIRS = FOX_HEADS * FOX_HEAD_DIM // LANES
XA_HEADS = 4
XA_HEAD_DIM = 256
RMS_EPS = 1e-6
NEG_INF = -1e30
LOG2E = math.log2(math.e)

STATE_TILES = 2 * SSM_GROUPS * SSM_STATE // LANES
RE_TILES = STATE_TILES // 2
U_TILES = SSM_WIDTH // LANES


def _rms(x, g):
    return x * lax.rsqrt(jnp.mean(x * x, axis=-1, keepdims=True) + RMS_EPS) * g


def _dot(a, b):
    return jnp.dot(a, b, preferred_element_type=F32)


def _dot_nt(a, b):
    return lax.dot_general(a, b, (((1,), (1,)), ((), ())), preferred_element_type=F32)


def _sigmoid(z):
    return 1.0 / (1.0 + jnp.exp(-z))


def _memkv_kernel(mem_ref, g_ref, w_ref, k_ref, v_ref):
    mn = _rms(mem_ref[...], g_ref[...]).astype(BF16)
    kv = _dot(mn, w_ref[...])
    k_ref[...] = (kv[:, :D_MODEL] * (1.0 / math.sqrt(XA_HEAD_DIM))).astype(BF16)
    v_ref[...] = kv[:, D_MODEL:].astype(BF16)


def _memkv(mem, mem_g, wkv):
    b, m, d = mem.shape
    return pl.pallas_call(
        _memkv_kernel,
        grid=(b,),
        in_specs=[
            pl.BlockSpec((None, m, d), lambda i: (i, 0, 0)),
            pl.BlockSpec((1, d), lambda i: (0, 0)),
            pl.BlockSpec((d, 2 * d), lambda i: (0, 0)),
        ],
        out_specs=[
            pl.BlockSpec((None, m, d), lambda i: (i, 0, 0)),
            pl.BlockSpec((None, m, d), lambda i: (i, 0, 0)),
        ],
        out_shape=[jax.ShapeDtypeStruct((b, m, d), BF16)] * 2,
        compiler_params=pltpu.CompilerParams(
            dimension_semantics=("arbitrary",), vmem_limit_bytes=VMEM_LIMIT_BYTES),
        name="memkv",
    )(mem, mem_g.reshape(1, d), wkv)


CUMSUM_SEGMENT = 512


def _inproj_kernel(x_ref, g_ref, w_ref, wf_ref, fb_ref, tri_ref,
                   u_ref, q_ref, k_ref, v_ref, f_ref, carry_ref):
    @pl.when(pl.program_id(1) == 0)
    def _():
        carry_ref[...] = jnp.zeros_like(carry_ref)

    hb = _rms(x_ref[...], g_ref[...]).astype(BF16)
    proj = _dot(hb, w_ref[...])
    u_ref[...] = proj[:, :SSM_WIDTH].astype(BF16)
    o = SSM_WIDTH
    q_ref[...] = (proj[:, o:o + FOX_WIDTH] * (LOG2E / math.sqrt(FOX_HEAD_DIM))).astype(BF16)
    k_ref[...] = proj[:, o + FOX_WIDTH:o + 2 * FOX_WIDTH].astype(BF16)
    v_ref[...] = proj[:, o + 2 * FOX_WIDTH:].astype(BF16)

    z = _dot_nt(wf_ref[...], hb) + fb_ref[...]
    logf = jnp.minimum(z, 0.0) - jnp.log1p(jnp.exp(-jnp.abs(z)))
    p1 = logf.astype(BF16)
    r1 = logf - p1.astype(F32)
    p2 = r1.astype(BF16)
    p3 = (r1 - p2.astype(F32)).astype(BF16)
    tri = tri_ref[...]
    seg = tri.shape[0]
    run = carry_ref[...][:, :1]
    parts = []
    for c0 in range(0, logf.shape[1], seg):
        sl = slice(c0, c0 + seg)
        parts.append(_dot(p1[:, sl], tri) + _dot(p2[:, sl], tri) + _dot(p3[:, sl], tri) + run)
        run = parts[-1][:, -1:]
    f = jnp.concatenate(parts, axis=1)
    f_ref[...] = f * LOG2E
    carry_ref[...] = jnp.broadcast_to(f[:, -1:], carry_ref.shape)


def _inproj(x, mix_pre_g, w_main, wf_t, f_bias, tm):
    b, s, d = x.shape
    nt = s // tm
    seg = min(tm, CUMSUM_SEGMENT)
    tri = jnp.triu(jnp.ones((seg, seg), F32)).astype(BF16)
    row = lambda i, j: (i, j, 0)
    const2 = lambda i, j: (0, 0)
    return pl.pallas_call(
        _inproj_kernel,
        grid=(b, nt),
        in_specs=[
            pl.BlockSpec((None, tm, d), row),
            pl.BlockSpec((1, d), const2),
            pl.BlockSpec(w_main.shape, const2),
            pl.BlockSpec(wf_t.shape, const2),
            pl.BlockSpec((FOX_HEADS, 1), const2),
            pl.BlockSpec((seg, seg), const2),
        ],
        out_specs=[
            pl.BlockSpec((None, tm, SSM_WIDTH), row),
            pl.BlockSpec((None, tm, FOX_WIDTH), row),
            pl.BlockSpec((None, tm, FOX_WIDTH), row),
            pl.BlockSpec((None, tm, FOX_WIDTH), row),
            pl.BlockSpec((None, FOX_HEADS, tm), lambda i, j: (i, 0, j)),
        ],
        out_shape=[
            jax.ShapeDtypeStruct((b, s, SSM_WIDTH), BF16),
            jax.ShapeDtypeStruct((b, s, FOX_WIDTH), BF16),
            jax.ShapeDtypeStruct((b, s, FOX_WIDTH), BF16),
            jax.ShapeDtypeStruct((b, s, FOX_WIDTH), BF16),
            jax.ShapeDtypeStruct((b, FOX_HEADS, s), F32),
        ],
        scratch_shapes=[pltpu.VMEM((FOX_HEADS, LANES), F32)],
        compiler_params=pltpu.CompilerParams(
            dimension_semantics=("arbitrary", "arbitrary"), vmem_limit_bytes=VMEM_LIMIT_BYTES),
        name="inproj",
    )(x, mix_pre_g.reshape(1, d), w_main, wf_t, f_bias.reshape(FOX_HEADS, 1), tri)


def _gelu_tanh(x):
    c = math.sqrt(2.0 / math.pi)
    return 0.5 * x * (1.0 + jnp.tanh(c * (x + 0.044715 * (x * x * x))))


TILE_PITCH = 20
TIME_PITCH = 324
SCAN_UNROLL = 8


def _ssm_kernel(u_ref, bblk_ref, cblk_ref, are_ref, aim_ref, d_ref, gw_ref, gb_ref, g_ref,
                o_ref, s_ref, xc_ref, *, nb, tm):
    @pl.when(pl.program_id(0) == 0)
    def _():
        xc_ref[...] = jnp.zeros_like(xc_ref)

    def rows_of_tile(b, part, k):
        return pl.ds(TILE_PITCH * k + 2 * b + part, tm, stride=TIME_PITCH)

    for ut in range(U_TILES):
        ub = u_ref[:, :, ut * LANES:(ut + 1) * LANES].reshape(nb * tm, LANES)
        res = _dot(ub, bblk_ref[ut])
        for b in range(nb):
            for j in range(2 * U_TILES):
                part, k = divmod(j, U_TILES)
                s_ref[rows_of_tile(b, part, U_TILES * ut + k), :] = (
                    res[b * tm:(b + 1) * tm, j * LANES:(j + 1) * LANES])

    ar = are_ref[...]
    ai = aim_ref[...]

    def step(t, carry):
        new = []
        for b in range(nb):
            xr, xi = carry[2 * b], carry[2 * b + 1]
            ire = pl.ds(TIME_PITCH * t + 2 * b, RE_TILES, stride=TILE_PITCH)
            iim = pl.ds(TIME_PITCH * t + 2 * b + 1, RE_TILES, stride=TILE_PITCH)
            nr = ar * xr - ai * xi + s_ref[ire, :]
            ni = ar * xi + ai * xr + s_ref[iim, :]
            s_ref[ire, :] = nr
            s_ref[iim, :] = ni
            new += [nr, ni]
        return tuple(new)

    init = []
    for b in range(nb):
        init.append(xc_ref[b * STATE_TILES:b * STATE_TILES + RE_TILES, :])
        init.append(xc_ref[b * STATE_TILES + RE_TILES:(b + 1) * STATE_TILES, :])
    fin = lax.fori_loop(0, tm, step, tuple(init), unroll=SCAN_UNROLL)
    for b in range(nb):
        xc_ref[b * STATE_TILES:b * STATE_TILES + RE_TILES, :] = fin[2 * b]
        xc_ref[b * STATE_TILES + RE_TILES:(b + 1) * STATE_TILES, :] = fin[2 * b + 1]

    ys = []
    for ot in range(U_TILES):
        per_seq = []
        for b in range(nb):
            tiles = [s_ref[rows_of_tile(b, part, U_TILES * ot + k), :].astype(BF16)
                     for part in range(2) for k in range(U_TILES)]
            per_seq.append(jnp.concatenate(tiles, axis=1))
        ys.append(_dot(jnp.concatenate(per_seq, axis=0), cblk_ref[ot]))
    u_all = u_ref[...].reshape(nb * tm, SSM_WIDTH).astype(F32)
    y = jnp.concatenate(ys, axis=1) + d_ref[...] * u_all
    g = _gelu_tanh(y)
    out = g * _sigmoid(_dot(g.astype(BF16), gw_ref[...]) + gb_ref[...])
    o_ref[...] = _rms(out, g_ref[...]).astype(BF16).reshape(nb, tm, SSM_WIDTH)


def _ssm(u, bblk, cblk, a_re, a_im, d_skip, glu_w, glu_b, out_g, tm):
    b, s, w = u.shape
    assert 2 * b <= TILE_PITCH - 4 and TIME_PITCH >= TILE_PITCH * (RE_TILES - 1) + 2 * b
    kern = functools.partial(_ssm_kernel, nb=b, tm=tm)
    c2 = lambda t: (0, 0)
    c3 = lambda t: (0, 0, 0)
    return pl.pallas_call(
        kern,
        grid=(s // tm,),
        in_specs=[
            pl.BlockSpec((b, tm, w), lambda t: (0, t, 0)),
            pl.BlockSpec(bblk.shape, c3),
            pl.BlockSpec(cblk.shape, c3),
            pl.BlockSpec(a_re.shape, c2),
            pl.BlockSpec(a_im.shape, c2),
            pl.BlockSpec((1, w), c2),
            pl.BlockSpec(glu_w.shape, c2),
            pl.BlockSpec((1, w), c2),
            pl.BlockSpec((1, w), c2),
        ],
        out_specs=pl.BlockSpec((b, tm, w), lambda t: (0, t, 0)),
        out_shape=jax.ShapeDtypeStruct((b, s, w), BF16),
        scratch_shapes=[
            pltpu.VMEM((tm * TIME_PITCH, LANES), F32),
            pltpu.VMEM((b * STATE_TILES, LANES), F32),
        ],
        compiler_params=pltpu.CompilerParams(
            dimension_semantics=("arbitrary",), vmem_limit_bytes=VMEM_LIMIT_BYTES),
        name="ssm",
    )(u, bblk, cblk, a_re, a_im, d_skip.reshape(1, w), glu_w, glu_b.reshape(1, w),
      out_g.reshape(1, w))


def _ssm_params(a_re, a_im, log_dt, b_re, b_im, c_re, c_im):
    a = lax.complex(a_re.astype(F32), a_im.astype(F32))
    dt = jnp.exp(log_dt.astype(F32))[:, None]
    a_bar = jnp.exp(a * dt)
    b_bar = ((a_bar - 1.0) / a)[..., None] * lax.complex(b_re.astype(F32), b_im.astype(F32))
    gpt = LANES // SSM_GROUP_CH
    eye = jnp.eye(gpt, dtype=F32)

    def bmat(part):
        p = part.reshape(U_TILES, gpt, SSM_STATE, SSM_GROUP_CH)
        m = jnp.einsum('ugni,gh->ugihn', p, eye)
        return m.reshape(U_TILES, LANES, gpt * SSM_STATE)

    def cmat(part):
        p = part.reshape(U_TILES, gpt, SSM_GROUP_CH, SSM_STATE)
        m = jnp.einsum('ugin,gh->ugnhi', p, eye)
        return m.reshape(U_TILES, gpt * SSM_STATE, LANES)

    bblk = jnp.concatenate([bmat(jnp.real(b_bar)), bmat(jnp.imag(b_bar))], axis=2).astype(BF16)
    cblk = jnp.concatenate([cmat(c_re.astype(F32)), cmat(-c_im.astype(F32))], axis=1).astype(BF16)
    are = jnp.real(a_bar).reshape(RE_TILES, LANES)
    aim = jnp.imag(a_bar).reshape(RE_TILES, LANES)
    return bblk, cblk, are, aim


BIAS_LANES = 8


def _split3(f):
    p1 = f.astype(BF16).astype(F32)
    r = f - p1
    p2 = r.astype(BF16).astype(F32)
    p3 = (r - p2).astype(BF16).astype(F32)
    return p1, p2, p3


def _bias_cols(f):
    n = f.shape[1]
    one = jnp.ones((1, n), F32)
    zero = jnp.zeros((2, n), F32)
    pieces = _split3(f)
    rows = []
    for key_side in (True, False):
        for h in range(2):
            ph = [p[h:h + 1, :] for p in pieces]
            rows += [one] * 3 + [-p for p in ph] if key_side else ph + [one] * 3
            rows.append(zero)
    pt = jnp.concatenate(rows, axis=0).astype(BF16)
    r = lax.broadcasted_iota(jnp.int32, (4 * BIAS_LANES, 2 * LANES), 0)
    c = lax.broadcasted_iota(jnp.int32, (4 * BIAS_LANES, 2 * LANES), 1)
    place = (c == jnp.where(r < 2 * BIAS_LANES, r, r + LANES - 2 * BIAS_LANES)).astype(BF16)
    cols = lax.dot_general(pt, place, (((0,), (0,)), ((), ())), preferred_element_type=F32)
    return cols[:, :LANES].astype(BF16), cols[:, LANES:].astype(BF16)


V_ROWS = 144
ONES_ROW = 2 * FOX_HEAD_DIM
QUERY_SLAB = 256
SKIP_EXPONENT = 160.0


def _fox_kernel(q_ref, k_ref, v_ref, frow_ref, o_ref, kaug_ref, qaug_ref, vt_ref, j0_ref,
                lhs_ref, s_ref, m_ref, acc_ref, *, t):
    i = pl.program_id(2)
    nq = pl.num_programs(2)
    seq = k_ref.shape[0]
    lane = lax.broadcasted_iota(jnp.int32, (t, LANES), 1)
    first = lane < FOX_HEAD_DIM

    def eye(rows, cols):
        return (lax.broadcasted_iota(jnp.int32, (rows, cols), 0) ==
                lax.broadcasted_iota(jnp.int32, (rows, cols), 1)).astype(BF16)

    def store_lhs(qi):
        r0 = pl.multiple_of(qi * t, t)
        q = q_ref[pl.ds(r0, t), :]
        bias = qaug_ref[pl.ds(r0, t), :]
        zero = jnp.zeros_like(q)
        fzero = jnp.zeros_like(bias)
        lhs_ref[:t, :] = jnp.concatenate(
            [jnp.where(first, q, zero), jnp.where(lane < BIAS_LANES, bias, fzero)], axis=1)
        lhs_ref[t:, :] = jnp.concatenate(
            [jnp.where(first, zero, q), jnp.where(lane < BIAS_LANES, fzero, bias)], axis=1)

    def reset_stats():
        m_ref[...] = jnp.full_like(m_ref, NEG_INF)
        acc_ref[...] = jnp.zeros_like(acc_ref)

    def head_norms2():
        qk = jnp.concatenate([q_ref[...], k_ref[...]], axis=1)
        sq = _dot_nt(eye(2 * LANES, 2 * LANES), qk)
        sq = sq * sq
        sums = [jnp.sum(sq[r:r + FOX_HEAD_DIM], axis=0, keepdims=True)
                for r in range(0, 2 * LANES, FOX_HEAD_DIM)]
        return jnp.concatenate(sums[:2], axis=0), jnp.concatenate(sums[2:], axis=0)

    def first_block(qi, qn2, kmax2):
        r0 = qi * t
        qk = jnp.sqrt(jnp.max(qn2[:, r0:r0 + t], axis=1, keepdims=True) * kmax2)
        f = frow_ref[...]
        tau = f[:, r0:r0 + 1] + 2.0 * qk + SKIP_EXPONENT
        pos = lax.broadcasted_iota(jnp.int32, (1, seq), 1)
        far = (f[0:1, :] > tau[0:1, :]) & (f[1:2, :] > tau[1:2, :])
        ends = ((pos & (t - 1)) == t - 1) & (pos < r0)
        return jnp.sum((far & ends).astype(jnp.int32))

    def step(jc, jp, causal):
        if jc is not None:
            vt = vt_ref[jc]
        if jp is not None:
            k0 = pl.multiple_of(jp * t, t)
            rk = jnp.concatenate([k_ref[pl.ds(k0, t), :], kaug_ref[pl.ds(k0, t), :]], axis=1)
        for c0 in range(0, 2 * t, QUERY_SLAB):
            sl = slice(c0, c0 + QUERY_SLAB)
            if jc is not None:
                s = s_ref[:, sl]
                m_old = m_ref[:, sl]
                m_new = jnp.maximum(m_old, jnp.max(s, axis=0, keepdims=True))
                alpha = jnp.exp2(m_old - m_new)
                p = jnp.exp2(s - m_new).astype(BF16)
                acc_ref[:, sl] = alpha * acc_ref[:, sl] + _dot(vt, p)
                m_ref[:, sl] = m_new
            if jp is not None:
                s = _dot_nt(rk, lhs_ref[sl, :])
                if causal is not False:
                    ahead = 0 if causal is True else causal
                    key = lax.broadcasted_iota(jnp.int32, s.shape, 0)
                    qry = (lax.broadcasted_iota(jnp.int32, s.shape, 1) + c0) & (t - 1)
                    s = jnp.where(qry + ahead >= key, s, NEG_INF)
                s_ref[:, sl] = s

    @pl.when(i == 0)
    def _():
        kaug_ref[...], qaug_ref[...] = _bias_cols(frow_ref[...])
        qn2, kn2 = head_norms2()
        kmax2 = jnp.max(kn2, axis=1, keepdims=True)
        for qi in range(seq // t):
            j0_ref[qi] = first_block(qi, qn2, kmax2)
        vt = _dot_nt(eye(V_ROWS, LANES), v_ref[...])
        row = lax.broadcasted_iota(jnp.int32, vt.shape, 0)
        vt = jnp.where(row == ONES_ROW, 1.0, vt).astype(BF16)
        for jb in range(seq // t):
            vt_ref[jb] = vt[:, jb * t:(jb + 1) * t]
        store_lhs(0)
        reset_stats()
        step(None, 0, True)

    j0 = j0_ref[i]
    n_steady = jnp.maximum(i - 1 - j0, 0)

    def steady_pair(jj, _):
        j = j0 + 2 * jj
        step(j, j + 1, False)
        step(j + 1, j + 2, False)
        return 0
    lax.fori_loop(0, n_steady >> 1, steady_pair, 0)

    @pl.when((n_steady & 1) == 1)
    def _():
        step(i - 2, i - 1, False)

    def finish():
        nxt = jnp.minimum(i + 1, nq - 1)
        j0_nxt = j0_ref[nxt]
        store_lhs(nxt)
        step(i, j0_nxt, (nxt - j0_nxt) * t)

        acc = acc_ref[...]
        inv_l = 1.0 / acc[ONES_ROW:ONES_ROW + 1, :]
        out_t = jnp.concatenate([acc[:FOX_HEAD_DIM, :t] * inv_l[:, :t],
                                 acc[FOX_HEAD_DIM:ONES_ROW, t:] * inv_l[:, t:]], axis=0)
        o_ref[...] = lax.dot_general(out_t.astype(BF16), eye(LANES, LANES), (((0,), (0,)), ((), ())),
                                     preferred_element_type=F32).astype(BF16)
        reset_stats()

    @pl.when(i > j0)
    def _():
        step(i - 1, i, True)

    finish()


def _fox(q, k, v, frow, t):
    b, s, w = q.shape
    assert s % t == 0 and t & (t - 1) == 0
    kern = functools.partial(_fox_kernel, t=t)
    whole = lambda bi, p, i: (bi, 0, p)
    return pl.pallas_call(
        kern,
        grid=(b, FOX_PAIRS, s // t),
        in_specs=[
            pl.BlockSpec((None, s, LANES), whole),
            pl.BlockSpec((None, s, LANES), whole),
            pl.BlockSpec((None, s, LANES), whole),
            pl.BlockSpec((None, None, 2, s), lambda bi, p, i: (bi, p, 0, 0)),
        ],
        out_specs=pl.BlockSpec((None, t, LANES), lambda bi, p, i: (bi, i, p)),
        out_shape=jax.ShapeDtypeStruct((b, s, w), BF16),
        scratch_shapes=[
            pltpu.VMEM((s, LANES), BF16),
            pltpu.VMEM((s, LANES), BF16),
            pltpu.VMEM((s // t, V_ROWS, t), BF16),
            pltpu.SMEM((s // t,), jnp.int32),
            pltpu.VMEM((2 * t, 2 * LANES), BF16),
            pltpu.VMEM((t, 2 * t), F32),
            pltpu.VMEM((1, 2 * t), F32),
            pltpu.VMEM((V_ROWS, 2 * t), F32),
        ],
        compiler_params=pltpu.CompilerParams(
            dimension_semantics=("arbitrary", "arbitrary", "arbitrary"),
            vmem_limit_bytes=VMEM_LIMIT_BYTES),
        name="fox",
    )(q, k, v, frow)


TAIL_SUB_ROWS = 256
FFN_CHUNK = 1024


def _tail_kernel(x_ref, ys_ref, yf_ref, kx_ref, vx_ref,
                 fox_g_ref, w_out_ref, mix_post_ref, xa_pre_ref, wq_ref, wo_ref, xa_post_ref,
                 ffn_pre_ref, wg_ref, wu_ref, wd_ref, ffn_post_ref, o_ref, *, n_sub):
    sub = x_ref.shape[0] // n_sub
    rows = [slice(r * sub, (r + 1) * sub) for r in range(n_sub)]
    each = lambda f, *lists: [f(*a) for a in zip(*lists)]

    x = [x_ref[r, :] for r in rows]
    yf = [_rms(yf_ref[r, :].astype(F32), fox_g_ref[...]).astype(BF16) for r in rows]
    mix = [_dot(ys_ref[r, :], w_out_ref[:SSM_WIDTH, :]) + _dot(f, w_out_ref[SSM_WIDTH:, :])
           for r, f in zip(rows, yf)]
    x = each(lambda xi, mi: xi + _rms(mi, mix_post_ref[...]), x, mix)

    h = [_rms(xi, xa_pre_ref[...]).astype(BF16) for xi in x]
    q = [_dot(hi, wq_ref[...]).astype(BF16) for hi in h]
    heads = [[] for _ in rows]
    for hd in range(XA_HEADS):
        sl = slice(hd * XA_HEAD_DIM, (hd + 1) * XA_HEAD_DIM)
        for qi, out in zip(q, heads):
            s = _dot_nt(qi[:, sl], kx_ref[:, sl])
            p = jnp.exp(s - jnp.max(s, axis=-1, keepdims=True))
            p = p / jnp.sum(p, axis=-1, keepdims=True)
            out.append(_dot(p.astype(BF16), vx_ref[:, sl]).astype(BF16))
    o = [jnp.concatenate(hs, axis=1) for hs in heads]
    x = each(lambda xi, oi: xi + _rms(_dot(oi, wo_ref[...]), xa_post_ref[...]), x, o)

    h = [_rms(xi, ffn_pre_ref[...]).astype(BF16) for xi in x]
    hidden = wg_ref.shape[1]
    down = [None] * n_sub
    for c0 in range(0, hidden, FFN_CHUNK):
        c1 = min(c0 + FFN_CHUNK, hidden)
        for r, hi in enumerate(h):
            gate = _dot(hi, wg_ref[:, c0:c1])
            up = _dot(hi, wu_ref[:, c0:c1])
            act = (gate * _sigmoid(gate) * up).astype(BF16)
            part = _dot(act, wd_ref[c0:c1, :])
            down[r] = part if down[r] is None else down[r] + part
    for r, xi, di in zip(rows, x, down):
        o_ref[r, :] = xi + _rms(di, ffn_post_ref[...])


def _tail(x, ys, yf, kx, vx, fox_g, w_out, mix_post_g, xa_pre_g, wq, wo, xa_post_g,
          ffn_pre_g, wg, wu, wd, ffn_post_g, tm):
    b, s, d = x.shape
    m = kx.shape[1]
    row = lambda i, j: (i, j, 0)
    per_b = lambda i, j: (i, 0, 0)
    const = lambda i, j: (0, 0)

    def resident(a):
        return pl.BlockSpec(a.shape, const, pipeline_mode=pl.Buffered(1))

    gains = [g.reshape(1, -1) for g in (fox_g, mix_post_g, xa_pre_g, xa_post_g, ffn_pre_g, ffn_post_g)]
    fox_g, mix_post_g, xa_pre_g, xa_post_g, ffn_pre_g, ffn_post_g = gains
    args = (x, ys, yf, kx, vx, fox_g, w_out, mix_post_g, xa_pre_g, wq, wo, xa_post_g,
            ffn_pre_g, wg, wu, wd, ffn_post_g)
    in_specs = [
        pl.BlockSpec((None, tm, d), row),
        pl.BlockSpec((None, tm, SSM_WIDTH), row),
        pl.BlockSpec((None, tm, FOX_WIDTH), row),
        pl.BlockSpec((None, m, d), per_b),
        pl.BlockSpec((None, m, d), per_b),
    ] + [resident(a) for a in args[5:]]
    return pl.pallas_call(
        functools.partial(_tail_kernel, n_sub=tm // TAIL_SUB_ROWS),
        grid=(b, s // tm),
        in_specs=in_specs,
        out_specs=pl.BlockSpec((None, tm, d), row),
        out_shape=jax.ShapeDtypeStruct((b, s, d), F32),
        compiler_params=pltpu.CompilerParams(
            dimension_semantics=("arbitrary", "arbitrary"), vmem_limit_bytes=VMEM_LIMIT_BYTES),
        name="tail",
    )(*args)


def _pick(n, pref):
    t = min(n, pref)
    assert n % t == 0, (n, t)
    return t


def kernel(x, mem, mix_pre_g, w_in, ssm_a_re, ssm_a_im, ssm_log_dt, ssm_b_re, ssm_b_im, ssm_c_re, ssm_c_im, ssm_d, ssm_glu_w, ssm_glu_b, fox_f_bias, ssm_out_g, fox_out_g, w_out, mix_post_g, xa_pre_g, mem_g, xa_wq, xa_wkv, xa_wo, xa_post_g, ffn_pre_g, w_gate, w_up, w_down, ffn_post_g):
    b, s, d = x.shape
    assert d == D_MODEL and s % LANES == 0
    n_main = SSM_WIDTH + 3 * FOX_WIDTH

    kx, vx = _memkv(mem, mem_g, xa_wkv.astype(BF16))

    u, q, k, v, fcum = _inproj(
        x, mix_pre_g, w_in[:, :n_main].astype(BF16), w_in[:, n_main:].T.astype(BF16),
        fox_f_bias, tm=_pick(s, 1024))

    bblk, cblk, are, aim = _ssm_params(ssm_a_re, ssm_a_im, ssm_log_dt, ssm_b_re, ssm_b_im,
                                       ssm_c_re, ssm_c_im)
    y_ssm = _ssm(u, bblk, cblk, are, aim, ssm_d, ssm_glu_w.astype(BF16), ssm_glu_b, ssm_out_g,
                 tm=_pick(s, 128))

    y_fox = _fox(q, k, v, fcum.reshape(b, FOX_PAIRS, 2, s), t=_pick(s, 512))

    return _tail(x, y_ssm, y_fox, kx, vx, fox_out_g, w_out.astype(BF16), mix_post_g, xa_pre_g,
                 xa_wq.astype(BF16), xa_wo.astype(BF16), xa_post_g, ffn_pre_g,
                 w_gate.astype(BF16), w_up.astype(BF16), w_down.astype(BF16), ffn_post_g,
                 tm=_pick(s, 2 * TAIL_SUB_ROWS))
```

```python
import functools
import math

import jax
import jax.numpy as jnp
from jax import lax
from jax.experimental import pallas as pl
from jax.experimental.pallas import tpu as pltpu

F32 = jnp.float32
BF16 = jnp.bfloat16

LANES = 128
SUBLANES = 8
VMEM_LIMIT_BYTES = 56 * 1024 * 1024

D_MODEL = 1024
SSM_WIDTH = 512
SSM_GROUP_CH = 16
SSM_GROUPS = 32
SSM_STATE = 64
FOX_WIDTH = 512
FOX_HEAD_DIM = 64
FOX_HEADS = 8
FOX_PAIRS = FOX_HEADS * FOX_HEAD_DIM // LANES
XA_HEADS = 4
XA_HEAD_DIM = 256
RMS_EPS = 1e-6
NEG_INF = -1e30
LOG2E = math.log2(math.e)

STATE_TILES = 2 * SSM_GROUPS * SSM_STATE // LANES
RE_TILES = STATE_TILES // 2
U_TILES = SSM_WIDTH // LANES


def _rms(x, g):
    return x * lax.rsqrt(jnp.mean(x * x, axis=-1, keepdims=True) + RMS_EPS) * g


def _dot(a, b):
    return jnp.dot(a, b, preferred_element_type=F32)


def _dot_nt(a, b):
    return lax.dot_general(a, b, (((1,), (1,)), ((), ())), preferred_element_type=F32)


def _sigmoid(z):
    return 1.0 / (1.0 + jnp.exp(-z))


def _memkv_kernel(mem_ref, g_ref, w_ref, k_ref, v_ref):
    mn = _rms(mem_ref[...], g_ref[...]).astype(BF16)
    kv = _dot(mn, w_ref[...])
    k_ref[...] = (kv[:, :D_MODEL] * (1.0 / math.sqrt(XA_HEAD_DIM))).astype(BF16)
    v_ref[...] = kv[:, D_MODEL:].astype(BF16)


def _memkv(mem, mem_g, wkv):
    b, m, d = mem.shape
    return pl.pallas_call(
        _memkv_kernel,
        grid=(b,),
        in_specs=[
            pl.BlockSpec((None, m, d), lambda i: (i, 0, 0)),
            pl.BlockSpec((1, d), lambda i: (0, 0)),
            pl.BlockSpec((d, 2 * d), lambda i: (0, 0)),
        ],
        out_specs=[
            pl.BlockSpec((None, m, d), lambda i: (i, 0, 0)),
            pl.BlockSpec((None, m, d), lambda i: (i, 0, 0)),
        ],
        out_shape=[jax.ShapeDtypeStruct((b, m, d), BF16)] * 2,
        compiler_params=pltpu.CompilerParams(
            dimension_semantics=("arbitrary",), vmem_limit_bytes=VMEM_LIMIT_BYTES),
        name="memkv",
    )(mem, mem_g.reshape(1, d), wkv)


CUMSUM_SEGMENT = 512


def _inproj_kernel(x_ref, g_ref, w_ref, wf_ref, fb_ref, tri_ref,
                   u_ref, q_ref, k_ref, v_ref, f_ref, carry_ref):
    @pl.when(pl.program_id(1) == 0)
    def _():
        carry_ref[...] = jnp.zeros_like(carry_ref)

    hb = _rms(x_ref[...], g_ref[...]).astype(BF16)
    proj = _dot(hb, w_ref[...])
    u_ref[...] = proj[:, :SSM_WIDTH].astype(BF16)
    o = SSM_WIDTH
    q_ref[...] = (proj[:, o:o + FOX_WIDTH] * (LOG2E / math.sqrt(FOX_HEAD_DIM))).astype(BF16)
    k_ref[...] = proj[:, o + FOX_WIDTH:o + 2 * FOX_WIDTH].astype(BF16)
    v_ref[...] = proj[:, o + 2 * FOX_WIDTH:].astype(BF16)

    z = _dot_nt(wf_ref[...], hb) + fb_ref[...]
    logf = jnp.minimum(z, 0.0) - jnp.log1p(jnp.exp(-jnp.abs(z)))
    p1 = logf.astype(BF16)
    r1 = logf - p1.astype(F32)
    p2 = r1.astype(BF16)
    p3 = (r1 - p2.astype(F32)).astype(BF16)
    tri = tri_ref[...]
    seg = tri.shape[0]
    run = carry_ref[...][:, :1]
    parts = []
    for c0 in range(0, logf.shape[1], seg):
        sl = slice(c0, c0 + seg)
        parts.append(_dot(p1[:, sl], tri) + _dot(p2[:, sl], tri) + _dot(p3[:, sl], tri) + run)
        run = parts[-1][:, -1:]
    f = jnp.concatenate(parts, axis=1)
    f_ref[...] = f * LOG2E
    carry_ref[...] = jnp.broadcast_to(f[:, -1:], carry_ref.shape)


def _inproj(x, mix_pre_g, w_main, wf_t, f_bias, tm):
    b, s, d = x.shape
    nt = s // tm
    seg = min(tm, CUMSUM_SEGMENT)
    tri = jnp.triu(jnp.ones((seg, seg), F32)).astype(BF16)
    row = lambda i, j: (i, j, 0)
    const2 = lambda i, j: (0, 0)
    return pl.pallas_call(
        _inproj_kernel,
        grid=(b, nt),
        in_specs=[
            pl.BlockSpec((None, tm, d), row),
            pl.BlockSpec((1, d), const2),
            pl.BlockSpec(w_main.shape, const2),
            pl.BlockSpec(wf_t.shape, const2),
            pl.BlockSpec((FOX_HEADS, 1), const2),
            pl.BlockSpec((seg, seg), const2),
        ],
        out_specs=[
            pl.BlockSpec((None, tm, SSM_WIDTH), row),
            pl.BlockSpec((None, tm, FOX_WIDTH), row),
            pl.BlockSpec((None, tm, FOX_WIDTH), row),
            pl.BlockSpec((None, tm, FOX_WIDTH), row),
            pl.BlockSpec((None, FOX_HEADS, tm), lambda i, j: (i, 0, j)),
        ],
        out_shape=[
            jax.ShapeDtypeStruct((b, s, SSM_WIDTH), BF16),
            jax.ShapeDtypeStruct((b, s, FOX_WIDTH), BF16),
            jax.ShapeDtypeStruct((b, s, FOX_WIDTH), BF16),
            jax.ShapeDtypeStruct((b, s, FOX_WIDTH), BF16),
            jax.ShapeDtypeStruct((b, FOX_HEADS, s), F32),
        ],
        scratch_shapes=[pltpu.VMEM((FOX_HEADS, LANES), F32)],
        compiler_params=pltpu.CompilerParams(
            dimension_semantics=("arbitrary", "arbitrary"), vmem_limit_bytes=VMEM_LIMIT_BYTES),
        name="inproj",
    )(x, mix_pre_g.reshape(1, d), w_main, wf_t, f_bias.reshape(FOX_HEADS, 1), tri)


def _gelu_tanh(x):
    c = math.sqrt(2.0 / math.pi)
    return 0.5 * x * (1.0 + jnp.tanh(c * (x + 0.044715 * (x * x * x))))


TILE_PITCH = 20
TIME_PITCH = 324
SCAN_UNROLL = 8


def _ssm_kernel(u_ref, bblk_ref, cblk_ref, are_ref, aim_ref, d_ref, gw_ref, gb_ref, g_ref,
                o_ref, s_ref, xc_ref, *, nb, tm):
    @pl.when(pl.program_id(0) == 0)
    def _():
        xc_ref[...] = jnp.zeros_like(xc_ref)

    def rows_of_tile(b, part, k):
        return pl.ds(TILE_PITCH * k + 2 * b + part, tm, stride=TIME_PITCH)

    for ut in range(U_TILES):
        ub = u_ref[:, :, ut * LANES:(ut + 1) * LANES].reshape(nb * tm, LANES)
        res = _dot(ub, bblk_ref[ut])
        for b in range(nb):
            for j in range(2 * U_TILES):
                part, k = divmod(j, U_TILES)
                s_ref[rows_of_tile(b, part, U_TILES * ut + k), :] = (
                    res[b * tm:(b + 1) * tm, j * LANES:(j + 1) * LANES])

    ar = are_ref[...]
    ai = aim_ref[...]

    def step(t, carry):
        new = []
        for b in range(nb):
            xr, xi = carry[2 * b], carry[2 * b + 1]
            ire = pl.ds(TIME_PITCH * t + 2 * b, RE_TILES, stride=TILE_PITCH)
            iim = pl.ds(TIME_PITCH * t + 2 * b + 1, RE_TILES, stride=TILE_PITCH)
            nr = ar * xr - ai * xi + s_ref[ire, :]
            ni = ar * xi + ai * xr + s_ref[iim, :]
            s_ref[ire, :] = nr
            s_ref[iim, :] = ni
            new += [nr, ni]
        return tuple(new)

    init = []
    for b in range(nb):
        init.append(xc_ref[b * STATE_TILES:b * STATE_TILES + RE_TILES, :])
        init.append(xc_ref[b * STATE_TILES + RE_TILES:(b + 1) * STATE_TILES, :])
    fin = lax.fori_loop(0, tm, step, tuple(init), unroll=SCAN_UNROLL)
    for b in range(nb):
        xc_ref[b * STATE_TILES:b * STATE_TILES + RE_TILES, :] = fin[2 * b]
        xc_ref[b * STATE_TILES + RE_TILES:(b + 1) * STATE_TILES, :] = fin[2 * b + 1]

    ys = []
    for ot in range(U_TILES):
        per_seq = []
        for b in range(nb):
            tiles = [s_ref[rows_of_tile(b, part, U_TILES * ot + k), :].astype(BF16)
                     for part in range(2) for k in range(U_TILES)]
            per_seq.append(jnp.concatenate(tiles, axis=1))
        ys.append(_dot(jnp.concatenate(per_seq, axis=0), cblk_ref[ot]))
    u_all = u_ref[...].reshape(nb * tm, SSM_WIDTH).astype(F32)
    y = jnp.concatenate(ys, axis=1) + d_ref[...] * u_all
    g = _gelu_tanh(y)
    out = g * _sigmoid(_dot(g.astype(BF16), gw_ref[...]) + gb_ref[...])
    o_ref[...] = _rms(out, g_ref[...]).astype(BF16).reshape(nb, tm, SSM_WIDTH)


def _ssm(u, bblk, cblk, a_re, a_im, d_skip, glu_w, glu_b, out_g, tm):
    b, s, w = u.shape
    assert 2 * b <= TILE_PITCH - 4 and TIME_PITCH >= TILE_PITCH * (RE_TILES - 1) + 2 * b
    kern = functools.partial(_ssm_kernel, nb=b, tm=tm)
    c2 = lambda t: (0, 0)
    c3 = lambda t: (0, 0, 0)
    return pl.pallas_call(
        kern,
        grid=(s // tm,),
        in_specs=[
            pl.BlockSpec((b, tm, w), lambda t: (0, t, 0)),
            pl.BlockSpec(bblk.shape, c3),
            pl.BlockSpec(cblk.shape, c3),
            pl.BlockSpec(a_re.shape, c2),
            pl.BlockSpec(a_im.shape, c2),
            pl.BlockSpec((1, w), c2),
            pl.BlockSpec(glu_w.shape, c2),
            pl.BlockSpec((1, w), c2),
            pl.BlockSpec((1, w), c2),
        ],
        out_specs=pl.BlockSpec((b, tm, w), lambda t: (0, t, 0)),
        out_shape=jax.ShapeDtypeStruct((b, s, w), BF16),
        scratch_shapes=[
            pltpu.VMEM((tm * TIME_PITCH, LANES), F32),
            pltpu.VMEM((b * STATE_TILES, LANES), F32),
        ],
        compiler_params=pltpu.CompilerParams(
            dimension_semantics=("arbitrary",), vmem_limit_bytes=VMEM_LIMIT_BYTES),
        name="ssm",
    )(u, bblk, cblk, a_re, a_im, d_skip.reshape(1, w), glu_w, glu_b.reshape(1, w),
      out_g.reshape(1, w))


def _ssm_params(a_re, a_im, log_dt, b_re, b_im, c_re, c_im):
    a = lax.complex(a_re.astype(F32), a_im.astype(F32))
    dt = jnp.exp(log_dt.astype(F32))[:, None]
    a_bar = jnp.exp(a * dt)
    b_bar = ((a_bar - 1.0) / a)[..., None] * lax.complex(b_re.astype(F32), b_im.astype(F32))
    gpt = LANES // SSM_GROUP_CH
    eye = jnp.eye(gpt, dtype=F32)

    def bmat(part):
        p = part.reshape(U_TILES, gpt, SSM_STATE, SSM_GROUP_CH)
        m = jnp.einsum('ugni,gh->ugihn', p, eye)
        return m.reshape(U_TILES, LANES, gpt * SSM_STATE)

    def cmat(part):
        p = part.reshape(U_TILES, gpt, SSM_GROUP_CH, SSM_STATE)
        m = jnp.einsum('ugin,gh->ugnhi', p, eye)
        return m.reshape(U_TILES, gpt * SSM_STATE, LANES)

    bblk = jnp.concatenate([bmat(jnp.real(b_bar)), bmat(jnp.imag(b_bar))], axis=2).astype(BF16)
    cblk = jnp.concatenate([cmat(c_re.astype(F32)), cmat(-c_im.astype(F32))], axis=1).astype(BF16)
    are = jnp.real(a_bar).reshape(RE_TILES, LANES)
    aim = jnp.imag(a_bar).reshape(RE_TILES, LANES)
    return bblk, cblk, are, aim


BIAS_LANES = 8


def _split3(f):
    p1 = f.astype(BF16).astype(F32)
    r = f - p1
    p2 = r.astype(BF16).astype(F32)
    p3 = (r - p2).astype(BF16).astype(F32)
    return p1, p2, p3


def _bias_cols(f):
    n = f.shape[1]
    one = jnp.ones((1, n), F32)
    zero = jnp.zeros((2, n), F32)
    pieces = _split3(f)
    rows = []
    for key_side in (True, False):
        for h in range(2):
            ph = [p[h:h + 1, :] for p in pieces]
            rows += [one] * 3 + [-p for p in ph] if key_side else ph + [one] * 3
            rows.append(zero)
    pt = jnp.concatenate(rows, axis=0).astype(BF16)
    r = lax.broadcasted_iota(jnp.int32, (4 * BIAS_LANES, 2 * LANES), 0)
    c = lax.broadcasted_iota(jnp.int32, (4 * BIAS_LANES, 2 * LANES), 1)
    place = (c == jnp.where(r < 2 * BIAS_LANES, r, r + LANES - 2 * BIAS_LANES)).astype(BF16)
    cols = lax.dot_general(pt, place, (((0,), (0,)), ((), ())), preferred_element_type=F32)
    return cols[:, :LANES].astype(BF16), cols[:, LANES:].astype(BF16)


V_ROWS = 144
ONES_ROW = 2 * FOX_HEAD_DIM
QUERY_SLAB = 256
SKIP_EXPONENT = 160.0


def _fox_kernel(q_ref, k_ref, v_ref, frow_ref, o_ref, kaug_ref, qaug_ref, vt_ref, j0_ref,
                lhs_ref, s_ref, m_ref, acc_ref, *, t):
    i = pl.program_id(2)
    nq = pl.num_programs(2)
    seq = k_ref.shape[0]
    lane = lax.broadcasted_iota(jnp.int32, (t, LANES), 1)
    first = lane < FOX_HEAD_DIM

    def eye(rows, cols):
        return (lax.broadcasted_iota(jnp.int32, (rows, cols), 0) ==
                lax.broadcasted_iota(jnp.int32, (rows, cols), 1)).astype(BF16)

    def store_lhs(qi):
        r0 = pl.multiple_of(qi * t, t)
        q = q_ref[pl.ds(r0, t), :]
        bias = qaug_ref[pl.ds(r0, t), :]
        zero = jnp.zeros_like(q)
        fzero = jnp.zeros_like(bias)
        slot = qi & 1
        lhs_ref[slot, :t, :] = jnp.concatenate(
            [jnp.where(first, q, zero), jnp.where(lane < BIAS_LANES, bias, fzero)], axis=1)
        lhs_ref[slot, t:, :] = jnp.concatenate(
            [jnp.where(first, zero, q), jnp.where(lane < BIAS_LANES, fzero, bias)], axis=1)

    def reset_stats():
        m_ref[...] = jnp.full_like(m_ref, NEG_INF)
        acc_ref[...] = jnp.zeros_like(acc_ref)

    def head_norms2():
        qk = jnp.concatenate([q_ref[...], k_ref[...]], axis=1)
        sq = _dot_nt(eye(2 * LANES, 2 * LANES), qk)
        sq = sq * sq
        sums = [jnp.sum(sq[r:r + FOX_HEAD_DIM], axis=0, keepdims=True)
                for r in range(0, 2 * LANES, FOX_HEAD_DIM)]
        return jnp.concatenate(sums[:2], axis=0), jnp.concatenate(sums[2:], axis=0)

    def first_block(qi, qn2, kmax2):
        r0 = qi * t
        qk = jnp.sqrt(jnp.max(qn2[:, r0:r0 + t], axis=1, keepdims=True) * kmax2)
        f = frow_ref[...]
        tau = f[:, r0:r0 + 1] + 2.0 * qk + SKIP_EXPONENT
        pos = lax.broadcasted_iota(jnp.int32, (1, seq), 1)
        far = (f[0:1, :] > tau[0:1, :]) & (f[1:2, :] > tau[1:2, :])
        ends = ((pos & (t - 1)) == t - 1) & (pos < r0)
        return jnp.sum((far & ends).astype(jnp.int32))

    def step(jc, jp, causal, qp=None):
        slot = (i if qp is None else qp) & 1
        if jc is not None:
            vt = vt_ref[jc]
        if jp is not None:
            k0 = pl.multiple_of(jp * t, t)
            rk = jnp.concatenate([k_ref[pl.ds(k0, t), :], kaug_ref[pl.ds(k0, t), :]], axis=1)
        for c0 in range(0, 2 * t, QUERY_SLAB):
            sl = slice(c0, c0 + QUERY_SLAB)
            if jc is not None:
                s = s_ref[:, sl]
                m_old = m_ref[:, sl]
                m_new = jnp.maximum(m_old, jnp.max(s, axis=0, keepdims=True))
                alpha = jnp.exp2(m_old - m_new)
                p = jnp.exp2(s - m_new).astype(BF16)
                acc_ref[:, sl] = alpha * acc_ref[:, sl] + _dot(vt, p)
                m_ref[:, sl] = m_new
            if jp is not None:
                s = _dot_nt(rk, lhs_ref[slot, sl, :])
                if causal is not False:
                    ahead = 0 if causal is True else causal
                    key = lax.broadcasted_iota(jnp.int32, s.shape, 0)
                    qry = (lax.broadcasted_iota(jnp.int32, s.shape, 1) + c0) & (t - 1)
                    s = jnp.where(qry + ahead >= key, s, NEG_INF)
                s_ref[:, sl] = s

    @pl.when(i == 0)
    def _():
        kaug_ref[...], qaug_ref[...] = _bias_cols(frow_ref[...])
        qn2, kn2 = head_norms2()
        kmax2 = jnp.max(kn2, axis=1, keepdims=True)
        for qi in range(seq // t):
            j0_ref[qi] = first_block(qi, qn2, kmax2)
        vt = _dot_nt(eye(V_ROWS, LANES), v_ref[...])
        row = lax.broadcasted_iota(jnp.int32, vt.shape, 0)
        vt = jnp.where(row == ONES_ROW, 1.0, vt).astype(BF16)
        for jb in range(seq // t):
            vt_ref[jb] = vt[:, jb * t:(jb + 1) * t]
        store_lhs(0)
        reset_stats()
        step(None, 0, True)

    j0 = j0_ref[i]
    n_steady = jnp.maximum(i - 1 - j0, 0)

    def steady_pair(jj, _):
        j = j0 + 2 * jj
        step(j, j + 1, False)
        step(j + 1, j + 2, False)
        return 0
    lax.fori_loop(0, n_steady >> 1, steady_pair, 0)

    @pl.when((n_steady & 1) == 1)
    def _():
        step(i - 2, i - 1, False)

    def finish():
        nxt = jnp.minimum(i + 1, nq - 1)
        j0_nxt = j0_ref[nxt]
        store_lhs(nxt)
        step(i, j0_nxt, (nxt - j0_nxt) * t, nxt)

        acc = acc_ref[...]
        inv_l = 1.0 / acc[ONES_ROW:ONES_ROW + 1, :]
        out_t = jnp.concatenate([acc[:FOX_HEAD_DIM, :t] * inv_l[:, :t],
                                 acc[FOX_HEAD_DIM:ONES_ROW, t:] * inv_l[:, t:]], axis=0)
        o_ref[...] = lax.dot_general(out_t.astype(BF16), eye(LANES, LANES), (((0,), (0,)), ((), ())),
                                     preferred_element_type=F32).astype(BF16)
        reset_stats()

    @pl.when(i > j0)
    def _():
        step(i - 1, i, True)
        finish()

    @pl.when(i <= j0)
    def _():
        finish()


def _fox(q, k, v, frow, t):
    b, s, w = q.shape
    assert s % t == 0 and t & (t - 1) == 0
    kern = functools.partial(_fox_kernel, t=t)
    whole = lambda bi, p, i: (bi, 0, p)
    return pl.pallas_call(
        kern,
        grid=(b, FOX_PAIRS, s // t),
        in_specs=[
            pl.BlockSpec((None, s, LANES), whole),
            pl.BlockSpec((None, s, LANES), whole),
            pl.BlockSpec((None, s, LANES), whole),
            pl.BlockSpec((None, None, 2, s), lambda bi, p, i: (bi, p, 0, 0)),
        ],
        out_specs=pl.BlockSpec((None, t, LANES), lambda bi, p, i: (bi, i, p)),
        out_shape=jax.ShapeDtypeStruct((b, s, w), BF16),
        scratch_shapes=[
            pltpu.VMEM((s, LANES), BF16),
            pltpu.VMEM((s, LANES), BF16),
            pltpu.VMEM((s // t, V_ROWS, t), BF16),
            pltpu.SMEM((s // t,), jnp.int32),
            pltpu.VMEM((2, 2 * t, 2 * LANES), BF16),
            pltpu.VMEM((t, 2 * t), F32),
            pltpu.VMEM((1, 2 * t), F32),
            pltpu.VMEM((V_ROWS, 2 * t), F32),
        ],
        compiler_params=pltpu.CompilerParams(
            dimension_semantics=("arbitrary", "arbitrary", "arbitrary"),
            vmem_limit_bytes=VMEM_LIMIT_BYTES),
        name="fox",
    )(q, k, v, frow)


TAIL_SUB_ROWS = 256
FFN_CHUNK = 1024


def _tail_kernel(x_ref, ys_ref, yf_ref, kx_ref, vx_ref,
                 fox_g_ref, w_out_ref, mix_post_ref, xa_pre_ref, wq_ref, wo_ref, xa_post_ref,
                 ffn_pre_ref, wg_ref, wu_ref, wd_ref, ffn_post_ref, o_ref, *, n_sub):
    sub = x_ref.shape[0] // n_sub
    rows = [slice(r * sub, (r + 1) * sub) for r in range(n_sub)]
    each = lambda f, *lists: [f(*a) for a in zip(*lists)]

    x = [x_ref[r, :] for r in rows]
    yf = [_rms(yf_ref[r, :].astype(F32), fox_g_ref[...]).astype(BF16) for r in rows]
    mix = [_dot(ys_ref[r, :], w_out_ref[:SSM_WIDTH, :]) + _dot(f, w_out_ref[SSM_WIDTH:, :])
           for r, f in zip(rows, yf)]
    x = each(lambda xi, mi: xi + _rms(mi, mix_post_ref[...]), x, mix)

    h = [_rms(xi, xa_pre_ref[...]).astype(BF16) for xi in x]
    q = [_dot(hi, wq_ref[...]).astype(BF16) for hi in h]
    heads = [[] for _ in rows]
    for hd in range(XA_HEADS):
        sl = slice(hd * XA_HEAD_DIM, (hd + 1) * XA_HEAD_DIM)
        for qi, out in zip(q, heads):
            s = _dot_nt(qi[:, sl], kx_ref[:, sl])
            p = jnp.exp(s - jnp.max(s, axis=-1, keepdims=True))
            p = p / jnp.sum(p, axis=-1, keepdims=True)
            out.append(_dot(p.astype(BF16), vx_ref[:, sl]).astype(BF16))
    o = [jnp.concatenate(hs, axis=1) for hs in heads]
    x = each(lambda xi, oi: xi + _rms(_dot(oi, wo_ref[...]), xa_post_ref[...]), x, o)

    h = [_rms(xi, ffn_pre_ref[...]).astype(BF16) for xi in x]
    hidden = wg_ref.shape[1]
    down = [None] * n_sub
    for c0 in range(0, hidden, FFN_CHUNK):
        c1 = min(c0 + FFN_CHUNK, hidden)
        for r, hi in enumerate(h):
            gate = _dot(hi, wg_ref[:, c0:c1])
            up = _dot(hi, wu_ref[:, c0:c1])
            act = (gate * _sigmoid(gate) * up).astype(BF16)
            part = _dot(act, wd_ref[c0:c1, :])
            down[r] = part if down[r] is None else down[r] + part
    for r, xi, di in zip(rows, x, down):
        o_ref[r, :] = xi + _rms(di, ffn_post_ref[...])


def _tail(x, ys, yf, kx, vx, fox_g, w_out, mix_post_g, xa_pre_g, wq, wo, xa_post_g,
          ffn_pre_g, wg, wu, wd, ffn_post_g, tm):
    b, s, d = x.shape
    m = kx.shape[1]
    row = lambda i, j: (i, j, 0)
    per_b = lambda i, j: (i, 0, 0)
    const = lambda i, j: (0, 0)

    def resident(a):
        return pl.BlockSpec(a.shape, const, pipeline_mode=pl.Buffered(1))

    gains = [g.reshape(1, -1) for g in (fox_g, mix_post_g, xa_pre_g, xa_post_g, ffn_pre_g, ffn_post_g)]
    fox_g, mix_post_g, xa_pre_g, xa_post_g, ffn_pre_g, ffn_post_g = gains
    args = (x, ys, yf, kx, vx, fox_g, w_out, mix_post_g, xa_pre_g, wq, wo, xa_post_g,
            ffn_pre_g, wg, wu, wd, ffn_post_g)
    in_specs = [
        pl.BlockSpec((None, tm, d), row),
        pl.BlockSpec((None, tm, SSM_WIDTH), row),
        pl.BlockSpec((None, tm, FOX_WIDTH), row),
        pl.BlockSpec((None, m, d), per_b),
        pl.BlockSpec((None, m, d), per_b),
    ] + [resident(a) for a in args[5:]]
    return pl.pallas_call(
        functools.partial(_tail_kernel, n_sub=tm // TAIL_SUB_ROWS),
        grid=(b, s // tm),
        in_specs=in_specs,
        out_specs=pl.BlockSpec((None, tm, d), row),
        out_shape=jax.ShapeDtypeStruct((b, s, d), F32),
        compiler_params=pltpu.CompilerParams(
            dimension_semantics=("arbitrary", "arbitrary"), vmem_limit_bytes=VMEM_LIMIT_BYTES),
        name="tail",
    )(*args)


def _pick(n, pref):
    t = min(n, pref)
    assert n % t == 0, (n, t)
    return t


def kernel(x, mem, mix_pre_g, w_in, ssm_a_re, ssm_a_im, ssm_log_dt, ssm_b_re, ssm_b_im, ssm_c_re, ssm_c_im, ssm_d, ssm_glu_w, ssm_glu_b, fox_f_bias, ssm_out_g, fox_out_g, w_out, mix_post_g, xa_pre_g, mem_g, xa_wq, xa_wkv, xa_wo, xa_post_g, ffn_pre_g, w_gate, w_up, w_down, ffn_post_g):
    b, s, d = x.shape
    assert d == D_MODEL and s % LANES == 0
    n_main = SSM_WIDTH + 3 * FOX_WIDTH

    kx, vx = _memkv(mem, mem_g, xa_wkv.astype(BF16))

    u, q, k, v, fcum = _inproj(
        x, mix_pre_g, w_in[:, :n_main].astype(BF16), w_in[:, n_main:].T.astype(BF16),
        fox_f_bias, tm=_pick(s, 1024))

    bblk, cblk, are, aim = _ssm_params(ssm_a_re, ssm_a_im, ssm_log_dt, ssm_b_re, ssm_b_im,
                                       ssm_c_re, ssm_c_im)
    y_ssm = _ssm(u, bblk, cblk, are, aim, ssm_d, ssm_glu_w.astype(BF16), ssm_glu_b, ssm_out_g,
                 tm=_pick(s, 128))

    y_fox = _fox(q, k, v, fcum.reshape(b, FOX_PAIRS, 2, s), t=_pick(s, 512))

    return _tail(x, y_ssm, y_fox, kx, vx, fox_out_g, w_out.astype(BF16), mix_post_g, xa_pre_g,
                 xa_wq.astype(BF16), xa_wo.astype(BF16), xa_post_g, ffn_pre_g,
                 w_gate.astype(BF16), w_up.astype(BF16), w_down.astype(BF16), ffn_post_g,
                 tm=_pick(s, 2 * TAIL_SUB_ROWS))
```

```python
import functools
import math

import jax
import jax.numpy as jnp
from jax import lax
from jax.experimental import pallas as pl
from jax.experimental.pallas import tpu as pltpu

F32 = jnp.float32
BF16 = jnp.bfloat16

LANES = 128
SUBLANES = 8
VMEM_LIMIT_BYTES = 56 * 1024 * 1024

D_MODEL = 1024
SSM_WIDTH = 512
SSM_GROUP_CH = 16
SSM_GROUPS = 32
SSM_STATE = 64
FOX_WIDTH = 512
FOX_HEAD_DIM = 64
FOX_HEADS = 8
FOX_PAIRS = FOX_HEADS * FOX_HEAD_DIM // LANES
XA_HEADS = 4
XA_HEAD_DIM = 256
RMS_EPS = 1e-6
NEG_INF = -1e30
LOG2E = math.log2(math.e)

STATE_TILES = 2 * SSM_GROUPS * SSM_STATE // LANES
RE_TILES = STATE_TILES // 2
U_TILES = SSM_WIDTH // LANES


def _rms(x, g):
    return x * lax.rsqrt(jnp.mean(x * x, axis=-1, keepdims=True) + RMS_EPS) * g


def _dot(a, b):
    return jnp.dot(a, b, preferred_element_type=F32)


def _dot_nt(a, b):
    return lax.dot_general(a, b, (((1,), (1,)), ((), ())), preferred_element_type=F32)


def _sigmoid(z):
    return 1.0 / (1.0 + jnp.exp(-z))


def _memkv_kernel(mem_ref, g_ref, w_ref, k_ref, v_ref):
    mn = _rms(mem_ref[...], g_ref[...]).astype(BF16)
    kv = _dot(mn, w_ref[...])
    k_ref[...] = (kv[:, :D_MODEL] * (1.0 / math.sqrt(XA_HEAD_DIM))).astype(BF16)
    v_ref[...] = kv[:, D_MODEL:].astype(BF16)


def _memkv(mem, mem_g, wkv):
    b, m, d = mem.shape
    return pl.pallas_call(
        _memkv_kernel,
        grid=(b,),
        in_specs=[
            pl.BlockSpec((None, m, d), lambda i: (i, 0, 0)),
            pl.BlockSpec((1, d), lambda i: (0, 0)),
            pl.BlockSpec((d, 2 * d), lambda i: (0, 0)),
        ],
        out_specs=[
            pl.BlockSpec((None, m, d), lambda i: (i, 0, 0)),
            pl.BlockSpec((None, m, d), lambda i: (i, 0, 0)),
        ],
        out_shape=[jax.ShapeDtypeStruct((b, m, d), BF16)] * 2,
        compiler_params=pltpu.CompilerParams(
            dimension_semantics=("arbitrary",), vmem_limit_bytes=VMEM_LIMIT_BYTES),
        name="memkv",
    )(mem, mem_g.reshape(1, d), wkv)


CUMSUM_SEGMENT = 512


def _inproj_kernel(x_ref, g_ref, w_ref, wf_ref, fb_ref, tri_ref,
                   u_ref, q_ref, k_ref, v_ref, f_ref, carry_ref):
    @pl.when(pl.program_id(1) == 0)
    def _():
        carry_ref[...] = jnp.zeros_like(carry_ref)

    hb = _rms(x_ref[...], g_ref[...]).astype(BF16)
    proj = _dot(hb, w_ref[...])
    u_ref[...] = proj[:, :SSM_WIDTH].astype(BF16)
    o = SSM_WIDTH
    q_ref[...] = (proj[:, o:o + FOX_WIDTH] * (LOG2E / math.sqrt(FOX_HEAD_DIM))).astype(BF16)
    k_ref[...] = proj[:, o + FOX_WIDTH:o + 2 * FOX_WIDTH].astype(BF16)
    v_ref[...] = proj[:, o + 2 * FOX_WIDTH:].astype(BF16)

    z = _dot_nt(wf_ref[...], hb) + fb_ref[...]
    logf = jnp.minimum(z, 0.0) - jnp.log1p(jnp.exp(-jnp.abs(z)))
    p1 = logf.astype(BF16)
    r1 = logf - p1.astype(F32)
    p2 = r1.astype(BF16)
    p3 = (r1 - p2.astype(F32)).astype(BF16)
    tri = tri_ref[...]
    seg = tri.shape[0]
    run = carry_ref[...][:, :1]
    parts = []
    for c0 in range(0, logf.shape[1], seg):
        sl = slice(c0, c0 + seg)
        parts.append(_dot(p1[:, sl], tri) + _dot(p2[:, sl], tri) + _dot(p3[:, sl], tri) + run)
        run = parts[-1][:, -1:]
    f = jnp.concatenate(parts, axis=1)
    f_ref[...] = f * LOG2E
    carry_ref[...] = jnp.broadcast_to(f[:, -1:], carry_ref.shape)


def _inproj(x, mix_pre_g, w_main, wf_t, f_bias, tm):
    b, s, d = x.shape
    nt = s // tm
    seg = min(tm, CUMSUM_SEGMENT)
    tri = jnp.triu(jnp.ones((seg, seg), F32)).astype(BF16)
    row = lambda i, j: (i, j, 0)
    const2 = lambda i, j: (0, 0)
    return pl.pallas_call(
        _inproj_kernel,
        grid=(b, nt),
        in_specs=[
            pl.BlockSpec((None, tm, d), row),
            pl.BlockSpec((1, d), const2),
            pl.BlockSpec(w_main.shape, const2),
            pl.BlockSpec(wf_t.shape, const2),
            pl.BlockSpec((FOX_HEADS, 1), const2),
            pl.BlockSpec((seg, seg), const2),
        ],
        out_specs=[
            pl.BlockSpec((None, tm, SSM_WIDTH), row),
            pl.BlockSpec((None, tm, FOX_WIDTH), row),
            pl.BlockSpec((None, tm, FOX_WIDTH), row),
            pl.BlockSpec((None, tm, FOX_WIDTH), row),
            pl.BlockSpec((None, FOX_HEADS, tm), lambda i, j: (i, 0, j)),
        ],
        out_shape=[
            jax.ShapeDtypeStruct((b, s, SSM_WIDTH), BF16),
            jax.ShapeDtypeStruct((b, s, FOX_WIDTH), BF16),
            jax.ShapeDtypeStruct((b, s, FOX_WIDTH), BF16),
            jax.ShapeDtypeStruct((b, s, FOX_WIDTH), BF16),
            jax.ShapeDtypeStruct((b, FOX_HEADS, s), F32),
        ],
        scratch_shapes=[pltpu.VMEM((FOX_HEADS, LANES), F32)],
        compiler_params=pltpu.CompilerParams(
            dimension_semantics=("arbitrary", "arbitrary"), vmem_limit_bytes=VMEM_LIMIT_BYTES),
        name="inproj",
    )(x, mix_pre_g.reshape(1, d), w_main, wf_t, f_bias.reshape(FOX_HEADS, 1), tri)


def _gelu_tanh(x):
    c = math.sqrt(2.0 / math.pi)
    return 0.5 * x * (1.0 + jnp.tanh(c * (x + 0.044715 * (x * x * x))))


TILE_PITCH = 20
TIME_PITCH = 324
SCAN_UNROLL = 8


def _ssm_kernel(u_ref, bblk_ref, cblk_ref, are_ref, aim_ref, d_ref, gw_ref, gb_ref, g_ref,
                o_ref, s_ref, xc_ref, *, nb, tm):
    @pl.when(pl.program_id(0) == 0)
    def _():
        xc_ref[...] = jnp.zeros_like(xc_ref)

    def rows_of_tile(b, part, k):
        return pl.ds(TILE_PITCH * k + 2 * b + part, tm, stride=TIME_PITCH)

    for ut in range(U_TILES):
        ub = u_ref[:, :, ut * LANES:(ut + 1) * LANES].reshape(nb * tm, LANES)
        res = _dot(ub, bblk_ref[ut])
        for b in range(nb):
            for j in range(2 * U_TILES):
                part, k = divmod(j, U_TILES)
                s_ref[rows_of_tile(b, part, U_TILES * ut + k), :] = (
                    res[b * tm:(b + 1) * tm, j * LANES:(j + 1) * LANES])

    ar = are_ref[...]
    ai = aim_ref[...]

    def step(t, carry):
        new = []
        for b in range(nb):
            xr, xi = carry[2 * b], carry[2 * b + 1]
            ire = pl.ds(TIME_PITCH * t + 2 * b, RE_TILES, stride=TILE_PITCH)
            iim = pl.ds(TIME_PITCH * t + 2 * b + 1, RE_TILES, stride=TILE_PITCH)
            nr = ar * xr - ai * xi + s_ref[ire, :]
            ni = ar * xi + ai * xr + s_ref[iim, :]
            s_ref[ire, :] = nr
            s_ref[iim, :] = ni
            new += [nr, ni]
        return tuple(new)

    init = []
    for b in range(nb):
        init.append(xc_ref[b * STATE_TILES:b * STATE_TILES + RE_TILES, :])
        init.append(xc_ref[b * STATE_TILES + RE_TILES:(b + 1) * STATE_TILES, :])
    fin = lax.fori_loop(0, tm, step, tuple(init), unroll=SCAN_UNROLL)
    for b in range(nb):
        xc_ref[b * STATE_TILES:b * STATE_TILES + RE_TILES, :] = fin[2 * b]
        xc_ref[b * STATE_TILES + RE_TILES:(b + 1) * STATE_TILES, :] = fin[2 * b + 1]

    ys = []
    for ot in range(U_TILES):
        per_seq = []
        for b in range(nb):
            tiles = [s_ref[rows_of_tile(b, part, U_TILES * ot + k), :].astype(BF16)
                     for part in range(2) for k in range(U_TILES)]
            per_seq.append(jnp.concatenate(tiles, axis=1))
        ys.append(_dot(jnp.concatenate(per_seq, axis=0), cblk_ref[ot]))
    u_all = u_ref[...].reshape(nb * tm, SSM_WIDTH).astype(F32)
    y = jnp.concatenate(ys, axis=1) + d_ref[...] * u_all
    g = _gelu_tanh(y)
    out = g * _sigmoid(_dot(g.astype(BF16), gw_ref[...]) + gb_ref[...])
    o_ref[...] = _rms(out, g_ref[...]).astype(BF16).reshape(nb, tm, SSM_WIDTH)


def _ssm(u, bblk, cblk, a_re, a_im, d_skip, glu_w, glu_b, out_g, tm):
    b, s, w = u.shape
    assert 2 * b <= TILE_PITCH - 4 and TIME_PITCH >= TILE_PITCH * (RE_TILES - 1) + 2 * b
    kern = functools.partial(_ssm_kernel, nb=b, tm=tm)
    c2 = lambda t: (0, 0)
    c3 = lambda t: (0, 0, 0)
    return pl.pallas_call(
        kern,
        grid=(s // tm,),
        in_specs=[
            pl.BlockSpec((b, tm, w), lambda t: (0, t, 0)),
            pl.BlockSpec(bblk.shape, c3),
            pl.BlockSpec(cblk.shape, c3),
            pl.BlockSpec(a_re.shape, c2),
            pl.BlockSpec(a_im.shape, c2),
            pl.BlockSpec((1, w), c2),
            pl.BlockSpec(glu_w.shape, c2),
            pl.BlockSpec((1, w), c2),
            pl.BlockSpec((1, w), c2),
        ],
        out_specs=pl.BlockSpec((b, tm, w), lambda t: (0, t, 0)),
        out_shape=jax.ShapeDtypeStruct((b, s, w), BF16),
        scratch_shapes=[
            pltpu.VMEM((tm * TIME_PITCH, LANES), F32),
            pltpu.VMEM((b * STATE_TILES, LANES), F32),
        ],
        compiler_params=pltpu.CompilerParams(
            dimension_semantics=("arbitrary",), vmem_limit_bytes=VMEM_LIMIT_BYTES),
        name="ssm",
    )(u, bblk, cblk, a_re, a_im, d_skip.reshape(1, w), glu_w, glu_b.reshape(1, w),
      out_g.reshape(1, w))


def _ssm_params(a_re, a_im, log_dt, b_re, b_im, c_re, c_im):
    a = lax.complex(a_re.astype(F32), a_im.astype(F32))
    dt = jnp.exp(log_dt.astype(F32))[:, None]
    a_bar = jnp.exp(a * dt)
    b_bar = ((a_bar - 1.0) / a)[..., None] * lax.complex(b_re.astype(F32), b_im.astype(F32))
    gpt = LANES // SSM_GROUP_CH
    eye = jnp.eye(gpt, dtype=F32)

    def bmat(part):
        p = part.reshape(U_TILES, gpt, SSM_STATE, SSM_GROUP_CH)
        m = jnp.einsum('ugni,gh->ugihn', p, eye)
        return m.reshape(U_TILES, LANES, gpt * SSM_STATE)

    def cmat(part):
        p = part.reshape(U_TILES, gpt, SSM_GROUP_CH, SSM_STATE)
        m = jnp.einsum('ugin,gh->ugnhi', p, eye)
        return m.reshape(U_TILES, gpt * SSM_STATE, LANES)

    bblk = jnp.concatenate([bmat(jnp.real(b_bar)), bmat(jnp.imag(b_bar))], axis=2).astype(BF16)
    cblk = jnp.concatenate([cmat(c_re.astype(F32)), cmat(-c_im.astype(F32))], axis=1).astype(BF16)
    are = jnp.real(a_bar).reshape(RE_TILES, LANES)
    aim = jnp.imag(a_bar).reshape(RE_TILES, LANES)
    return bblk, cblk, are, aim


BIAS_LANES = 8


def _split3(f):
    p1 = f.astype(BF16).astype(F32)
    r = f - p1
    p2 = r.astype(BF16).astype(F32)
    p3 = (r - p2).astype(BF16).astype(F32)
    return p1, p2, p3


def _bias_cols(f):
    n = f.shape[1]
    one = jnp.ones((1, n), F32)
    zero = jnp.zeros((2, n), F32)
    pieces = _split3(f)
    rows = []
    for key_side in (True, False):
        for h in range(2):
            ph = [p[h:h + 1, :] for p in pieces]
            rows += [one] * 3 + [-p for p in ph] if key_side else ph + [one] * 3
            rows.append(zero)
    pt = jnp.concatenate(rows, axis=0).astype(BF16)
    r = lax.broadcasted_iota(jnp.int32, (4 * BIAS_LANES, 2 * LANES), 0)
    c = lax.broadcasted_iota(jnp.int32, (4 * BIAS_LANES, 2 * LANES), 1)
    place = (c == jnp.where(r < 2 * BIAS_LANES, r, r + LANES - 2 * BIAS_LANES)).astype(BF16)
    cols = lax.dot_general(pt, place, (((0,), (0,)), ((), ())), preferred_element_type=F32)
    return cols[:, :LANES].astype(BF16), cols[:, LANES:].astype(BF16)


V_ROWS = 144
ONES_ROW = 2 * FOX_HEAD_DIM
QUERY_SLAB = 256
SKIP_EXPONENT = 160.0


def _fox_kernel(q_ref, k_ref, v_ref, frow_ref, o_ref, kaug_ref, qaug_ref, vt_ref, j0_ref,
                lhs_ref, s_ref, m_ref, acc_ref, *, t):
    i = pl.program_id(2)
    nq = pl.num_programs(2)
    seq = k_ref.shape[0]
    lane = lax.broadcasted_iota(jnp.int32, (t, LANES), 1)
    first = lane < FOX_HEAD_DIM

    def eye(rows, cols):
        return (lax.broadcasted_iota(jnp.int32, (rows, cols), 0) ==
                lax.broadcasted_iota(jnp.int32, (rows, cols), 1)).astype(BF16)

    def store_lhs(qi):
        r0 = pl.multiple_of(qi * t, t)
        q = q_ref[pl.ds(r0, t), :]
        bias = qaug_ref[pl.ds(r0, t), :]
        zero = jnp.zeros_like(q)
        fzero = jnp.zeros_like(bias)
        slot = qi & 1
        lhs_ref[slot, :t, :] = jnp.concatenate(
            [jnp.where(first, q, zero), jnp.where(lane < BIAS_LANES, bias, fzero)], axis=1)
        lhs_ref[slot, t:, :] = jnp.concatenate(
            [jnp.where(first, zero, q), jnp.where(lane < BIAS_LANES, fzero, bias)], axis=1)

    def reset_stats():
        m_ref[...] = jnp.full_like(m_ref, NEG_INF)
        acc_ref[...] = jnp.zeros_like(acc_ref)

    def head_stats():
        qk = jnp.concatenate([q_ref[...], k_ref[...]], axis=1)
        qkt = _dot_nt(eye(2 * LANES, 2 * LANES), qk)
        qt, kt = qkt[:LANES], qkt[LANES:]

        def per_head(x):
            return jnp.concatenate([jnp.sum(x[:FOX_HEAD_DIM], axis=0, keepdims=True),
                                    jnp.sum(x[FOX_HEAD_DIM:], axis=0, keepdims=True)], axis=0)
        return per_head(qt * qt), per_head(kt * kt), per_head(qt * kt)

    def first_block(qi, qn2, kmax2, diag):
        r0 = qi * t
        qk = jnp.sqrt(jnp.max(qn2[:, r0:r0 + t], axis=1, keepdims=True) * kmax2)
        d = jnp.min(diag[:, r0:r0 + t], axis=1, keepdims=True)
        f = frow_ref[...]
        tau = f[:, r0:r0 + 1] + qk - d + SKIP_EXPONENT
        pos = lax.broadcasted_iota(jnp.int32, (1, seq), 1)
        far = (f[0:1, :] > tau[0:1, :]) & (f[1:2, :] > tau[1:2, :])
        ends = ((pos & (t - 1)) == t - 1) & (pos < r0)
        return jnp.sum((far & ends).astype(jnp.int32))

    def step(jc, jp, causal, qp=None):
        slot = (i if qp is None else qp) & 1
        if jc is not None:
            vt = vt_ref[jc]
        if jp is not None:
            k0 = pl.multiple_of(jp * t, t)
            rk = jnp.concatenate([k_ref[pl.ds(k0, t), :], kaug_ref[pl.ds(k0, t), :]], axis=1)
        for c0 in range(0, 2 * t, QUERY_SLAB):
            sl = slice(c0, c0 + QUERY_SLAB)
            if jc is not None:
                s = s_ref[:, sl]
                m_old = m_ref[:, sl]
                m_new = jnp.maximum(m_old, jnp.max(s, axis=0, keepdims=True))
                alpha = jnp.exp2(m_old - m_new)
                p = jnp.exp2(s - m_new).astype(BF16)
                acc_ref[:, sl] = alpha * acc_ref[:, sl] + _dot(vt, p)
                m_ref[:, sl] = m_new
            if jp is not None:
                s = _dot_nt(rk, lhs_ref[slot, sl, :])
                if causal is not False:
                    ahead = 0 if causal is True else causal
                    key = lax.broadcasted_iota(jnp.int32, s.shape, 0)
                    qry = (lax.broadcasted_iota(jnp.int32, s.shape, 1) + c0) & (t - 1)
                    s = jnp.where(qry + ahead >= key, s, NEG_INF)
                s_ref[:, sl] = s

    @pl.when(i == 0)
    def _():
        kaug_ref[...], qaug_ref[...] = _bias_cols(frow_ref[...])
        qn2, kn2, diag = head_stats()
        kmax2 = jnp.max(kn2, axis=1, keepdims=True)
        for qi in range(seq // t):
            j0_ref[qi] = first_block(qi, qn2, kmax2, diag)
        vt = _dot_nt(eye(V_ROWS, LANES), v_ref[...])
        row = lax.broadcasted_iota(jnp.int32, vt.shape, 0)
        vt = jnp.where(row == ONES_ROW, 1.0, vt).astype(BF16)
        for jb in range(seq // t):
            vt_ref[jb] = vt[:, jb * t:(jb + 1) * t]
        store_lhs(0)
        reset_stats()
        step(None, 0, True)

    j0 = j0_ref[i]
    n_steady = jnp.maximum(i - 1 - j0, 0)

    def steady_pair(jj, _):
        j = j0 + 2 * jj
        step(j, j + 1, False)
        step(j + 1, j + 2, False)
        return 0
    lax.fori_loop(0, n_steady >> 1, steady_pair, 0)

    def finish():
        nxt = jnp.minimum(i + 1, nq - 1)
        j0_nxt = j0_ref[nxt]
        store_lhs(nxt)
        step(i, j0_nxt, (nxt - j0_nxt) * t, nxt)

        acc = acc_ref[...]
        inv_l = 1.0 / acc[ONES_ROW:ONES_ROW + 1, :]
        out_t = jnp.concatenate([acc[:FOX_HEAD_DIM, :t] * inv_l[:, :t],
                                 acc[FOX_HEAD_DIM:ONES_ROW, t:] * inv_l[:, t:]], axis=0)
        o_ref[...] = lax.dot_general(out_t.astype(BF16), eye(LANES, LANES), (((0,), (0,)), ((), ())),
                                     preferred_element_type=F32).astype(BF16)
        reset_stats()

    odd = (n_steady & 1) == 1

    @pl.when(odd)
    def _():
        step(i - 2, i - 1, False)
        step(i - 1, i, True)
        finish()

    @pl.when((i > j0) & jnp.logical_not(odd))
    def _():
        step(i - 1, i, True)
        finish()

    @pl.when(i <= j0)
    def _():
        finish()


def _fox(q, k, v, frow, t):
    b, s, w = q.shape
    assert s % t == 0 and t & (t - 1) == 0
    kern = functools.partial(_fox_kernel, t=t)
    whole = lambda bi, p, i: (bi, 0, p)
    return pl.pallas_call(
        kern,
        grid=(b, FOX_PAIRS, s // t),
        in_specs=[
            pl.BlockSpec((None, s, LANES), whole),
            pl.BlockSpec((None, s, LANES), whole),
            pl.BlockSpec((None, s, LANES), whole),
            pl.BlockSpec((None, None, 2, s), lambda bi, p, i: (bi, p, 0, 0)),
        ],
        out_specs=pl.BlockSpec((None, t, LANES), lambda bi, p, i: (bi, i, p)),
        out_shape=jax.ShapeDtypeStruct((b, s, w), BF16),
        scratch_shapes=[
            pltpu.VMEM((s, LANES), BF16),
            pltpu.VMEM((s, LANES), BF16),
            pltpu.VMEM((s // t, V_ROWS, t), BF16),
            pltpu.SMEM((s // t,), jnp.int32),
            pltpu.VMEM((2, 2 * t, 2 * LANES), BF16),
            pltpu.VMEM((t, 2 * t), F32),
            pltpu.VMEM((1, 2 * t), F32),
            pltpu.VMEM((V_ROWS, 2 * t), F32),
        ],
        compiler_params=pltpu.CompilerParams(
            dimension_semantics=("arbitrary", "arbitrary", "arbitrary"),
            vmem_limit_bytes=VMEM_LIMIT_BYTES),
        name="fox",
    )(q, k, v, frow)


TAIL_SUB_ROWS = 256
FFN_CHUNK = 1024


def _tail_kernel(x_ref, ys_ref, yf_ref, kx_ref, vx_ref,
                 fox_g_ref, w_out_ref, mix_post_ref, xa_pre_ref, wq_ref, wo_ref, xa_post_ref,
                 ffn_pre_ref, wg_ref, wu_ref, wd_ref, ffn_post_ref, o_ref, *, n_sub):
    sub = x_ref.shape[0] // n_sub
    rows = [slice(r * sub, (r + 1) * sub) for r in range(n_sub)]
    each = lambda f, *lists: [f(*a) for a in zip(*lists)]

    x = [x_ref[r, :] for r in rows]
    yf = [_rms(yf_ref[r, :].astype(F32), fox_g_ref[...]).astype(BF16) for r in rows]
    mix = [_dot(ys_ref[r, :], w_out_ref[:SSM_WIDTH, :]) + _dot(f, w_out_ref[SSM_WIDTH:, :])
           for r, f in zip(rows, yf)]
    x = each(lambda xi, mi: xi + _rms(mi, mix_post_ref[...]), x, mix)

    h = [_rms(xi, xa_pre_ref[...]).astype(BF16) for xi in x]
    q = [_dot(hi, wq_ref[...]).astype(BF16) for hi in h]
    heads = [[] for _ in rows]
    for hd in range(XA_HEADS):
        sl = slice(hd * XA_HEAD_DIM, (hd + 1) * XA_HEAD_DIM)
        for qi, out in zip(q, heads):
            s = _dot_nt(qi[:, sl], kx_ref[:, sl])
            p = jnp.exp(s - jnp.max(s, axis=-1, keepdims=True))
            p = p / jnp.sum(p, axis=-1, keepdims=True)
            out.append(_dot(p.astype(BF16), vx_ref[:, sl]).astype(BF16))
    o = [jnp.concatenate(hs, axis=1) for hs in heads]
    x = each(lambda xi, oi: xi + _rms(_dot(oi, wo_ref[...]), xa_post_ref[...]), x, o)

    h = [_rms(xi, ffn_pre_ref[...]).astype(BF16) for xi in x]
    hidden = wg_ref.shape[1]
    down = [None] * n_sub
    for c0 in range(0, hidden, FFN_CHUNK):
        c1 = min(c0 + FFN_CHUNK, hidden)
        for r, hi in enumerate(h):
            gate = _dot(hi, wg_ref[:, c0:c1])
            up = _dot(hi, wu_ref[:, c0:c1])
            act = (gate * _sigmoid(gate) * up).astype(BF16)
            part = _dot(act, wd_ref[c0:c1, :])
            down[r] = part if down[r] is None else down[r] + part
    for r, xi, di in zip(rows, x, down):
        o_ref[r, :] = xi + _rms(di, ffn_post_ref[...])


def _tail(x, ys, yf, kx, vx, fox_g, w_out, mix_post_g, xa_pre_g, wq, wo, xa_post_g,
          ffn_pre_g, wg, wu, wd, ffn_post_g, tm):
    b, s, d = x.shape
    m = kx.shape[1]
    row = lambda i, j: (i, j, 0)
    per_b = lambda i, j: (i, 0, 0)
    const = lambda i, j: (0, 0)

    def resident(a):
        return pl.BlockSpec(a.shape, const, pipeline_mode=pl.Buffered(1))

    gains = [g.reshape(1, -1) for g in (fox_g, mix_post_g, xa_pre_g, xa_post_g, ffn_pre_g, ffn_post_g)]
    fox_g, mix_post_g, xa_pre_g, xa_post_g, ffn_pre_g, ffn_post_g = gains
    args = (x, ys, yf, kx, vx, fox_g, w_out, mix_post_g, xa_pre_g, wq, wo, xa_post_g,
            ffn_pre_g, wg, wu, wd, ffn_post_g)
    in_specs = [
        pl.BlockSpec((None, tm, d), row),
        pl.BlockSpec((None, tm, SSM_WIDTH), row),
        pl.BlockSpec((None, tm, FOX_WIDTH), row),
        pl.BlockSpec((None, m, d), per_b),
        pl.BlockSpec((None, m, d), per_b),
    ] + [resident(a) for a in args[5:]]
    return pl.pallas_call(
        functools.partial(_tail_kernel, n_sub=tm // TAIL_SUB_ROWS),
        grid=(b, s // tm),
        in_specs=in_specs,
        out_specs=pl.BlockSpec((None, tm, d), row),
        out_shape=jax.ShapeDtypeStruct((b, s, d), F32),
        compiler_params=pltpu.CompilerParams(
            dimension_semantics=("arbitrary", "arbitrary"), vmem_limit_bytes=VMEM_LIMIT_BYTES),
        name="tail",
    )(*args)


def _pick(n, pref):
    t = min(n, pref)
    assert n % t == 0, (n, t)
    return t


def kernel(x, mem, mix_pre_g, w_in, ssm_a_re, ssm_a_im, ssm_log_dt, ssm_b_re, ssm_b_im, ssm_c_re, ssm_c_im, ssm_d, ssm_glu_w, ssm_glu_b, fox_f_bias, ssm_out_g, fox_out_g, w_out, mix_post_g, xa_pre_g, mem_g, xa_wq, xa_wkv, xa_wo, xa_post_g, ffn_pre_g, w_gate, w_up, w_down, ffn_post_g):
    b, s, d = x.shape
    assert d == D_MODEL and s % LANES == 0
    n_main = SSM_WIDTH + 3 * FOX_WIDTH

    kx, vx = _memkv(mem, mem_g, xa_wkv.astype(BF16))

    u, q, k, v, fcum = _inproj(
        x, mix_pre_g, w_in[:, :n_main].astype(BF16), w_in[:, n_main:].T.astype(BF16),
        fox_f_bias, tm=_pick(s, 1024))

    bblk, cblk, are, aim = _ssm_params(ssm_a_re, ssm_a_im, ssm_log_dt, ssm_b_re, ssm_b_im,
                                       ssm_c_re, ssm_c_im)
    y_ssm = _ssm(u, bblk, cblk, are, aim, ssm_d, ssm_glu_w.astype(BF16), ssm_glu_b, ssm_out_g,
                 tm=_pick(s, 128))

    y_fox = _fox(q, k, v, fcum.reshape(b, FOX_PAIRS, 2, s), t=_pick(s, 512))

    return _tail(x, y_ssm, y_fox, kx, vx, fox_out_g, w_out.astype(BF16), mix_post_g, xa_pre_g,
                 xa_wq.astype(BF16), xa_wo.astype(BF16), xa_post_g, ffn_pre_g,
                 w_gate.astype(BF16), w_up.astype(BF16), w_down.astype(BF16), ffn_post_g,
                 tm=_pick(s, 2 * TAIL_SUB_ROWS))
```

```python
import functools
import math

import jax
import jax.numpy as jnp
from jax import lax
from jax.experimental import pallas as pl
from jax.experimental.pallas import tpu as pltpu

F32 = jnp.float32
BF16 = jnp.bfloat16

LANES = 128
SUBLANES = 8
VMEM_LIMIT_BYTES = 56 * 1024 * 1024

D_MODEL = 1024
SSM_WIDTH = 512
SSM_GROUP_CH = 16
SSM_GROUPS = 32
SSM_STATE = 64
FOX_WIDTH = 512
FOX_HEAD_DIM = 64
FOX_HEADS = 8
FOX_PAIRS = FOX_HEADS * FOX_HEAD_DIM // LANES
XA_HEADS = 4
XA_HEAD_DIM = 256
RMS_EPS = 1e-6
NEG_INF = -1e30
LOG2E = math.log2(math.e)

STATE_TILES = 2 * SSM_GROUPS * SSM_STATE // LANES
RE_TILES = STATE_TILES // 2
U_TILES = SSM_WIDTH // LANES


def _rms(x, g):
    return x * lax.rsqrt(jnp.mean(x * x, axis=-1, keepdims=True) + RMS_EPS) * g


def _dot(a, b):
    return jnp.dot(a, b, preferred_element_type=F32)


def _dot_nt(a, b):
    return lax.dot_general(a, b, (((1,), (1,)), ((), ())), preferred_element_type=F32)


def _sigmoid(z):
    return 1.0 / (1.0 + jnp.exp(-z))


def _memkv_kernel(mem_ref, g_ref, w_ref, k_ref, v_ref):
    mn = _rms(mem_ref[...], g_ref[...]).astype(BF16)
    kv = _dot(mn, w_ref[...])
    k_ref[...] = (kv[:, :D_MODEL] * (1.0 / math.sqrt(XA_HEAD_DIM))).astype(BF16)
    v_ref[...] = kv[:, D_MODEL:].astype(BF16)


def _memkv(mem, mem_g, wkv):
    b, m, d = mem.shape
    return pl.pallas_call(
        _memkv_kernel,
        grid=(b,),
        in_specs=[
            pl.BlockSpec((None, m, d), lambda i: (i, 0, 0)),
            pl.BlockSpec((1, d), lambda i: (0, 0)),
            pl.BlockSpec((d, 2 * d), lambda i: (0, 0)),
        ],
        out_specs=[
            pl.BlockSpec((None, m, d), lambda i: (i, 0, 0)),
            pl.BlockSpec((None, m, d), lambda i: (i, 0, 0)),
        ],
        out_shape=[jax.ShapeDtypeStruct((b, m, d), BF16)] * 2,
        compiler_params=pltpu.CompilerParams(
            dimension_semantics=("arbitrary",), vmem_limit_bytes=VMEM_LIMIT_BYTES),
        name="memkv",
    )(mem, mem_g.reshape(1, d), wkv)


CUMSUM_SEGMENT = 512


def _inproj_kernel(x_ref, g_ref, w_ref, wf_ref, fb_ref, tri_ref,
                   u_ref, q_ref, k_ref, v_ref, f_ref, carry_ref):
    @pl.when(pl.program_id(1) == 0)
    def _():
        carry_ref[...] = jnp.zeros_like(carry_ref)

    hb = _rms(x_ref[...], g_ref[...]).astype(BF16)
    proj = _dot(hb, w_ref[...])
    u_ref[...] = proj[:, :SSM_WIDTH].astype(BF16)
    o = SSM_WIDTH
    q_ref[...] = (proj[:, o:o + FOX_WIDTH] * (LOG2E / math.sqrt(FOX_HEAD_DIM))).astype(BF16)
    k_ref[...] = proj[:, o + FOX_WIDTH:o + 2 * FOX_WIDTH].astype(BF16)
    v_ref[...] = proj[:, o + 2 * FOX_WIDTH:].astype(BF16)

    z = _dot_nt(wf_ref[...], hb) + fb_ref[...]
    logf = jnp.minimum(z, 0.0) - jnp.log1p(jnp.exp(-jnp.abs(z)))
    p1 = logf.astype(BF16)
    r1 = logf - p1.astype(F32)
    p2 = r1.astype(BF16)
    p3 = (r1 - p2.astype(F32)).astype(BF16)
    tri = tri_ref[...]
    seg = tri.shape[0]
    run = carry_ref[...][:, :1]
    parts = []
    for c0 in range(0, logf.shape[1], seg):
        sl = slice(c0, c0 + seg)
        parts.append(_dot(p1[:, sl], tri) + _dot(p2[:, sl], tri) + _dot(p3[:, sl], tri) + run)
        run = parts[-1][:, -1:]
    f = jnp.concatenate(parts, axis=1)
    f_ref[...] = f * LOG2E
    carry_ref[...] = jnp.broadcast_to(f[:, -1:], carry_ref.shape)


def _inproj(x, mix_pre_g, w_main, wf_t, f_bias, tm):
    b, s, d = x.shape
    nt = s // tm
    seg = min(tm, CUMSUM_SEGMENT)
    tri = jnp.triu(jnp.ones((seg, seg), F32)).astype(BF16)
    row = lambda i, j: (i, j, 0)
    const2 = lambda i, j: (0, 0)
    return pl.pallas_call(
        _inproj_kernel,
        grid=(b, nt),
        in_specs=[
            pl.BlockSpec((None, tm, d), row),
            pl.BlockSpec((1, d), const2),
            pl.BlockSpec(w_main.shape, const2),
            pl.BlockSpec(wf_t.shape, const2),
            pl.BlockSpec((FOX_HEADS, 1), const2),
            pl.BlockSpec((seg, seg), const2),
        ],
        out_specs=[
            pl.BlockSpec((None, tm, SSM_WIDTH), row),
            pl.BlockSpec((None, tm, FOX_WIDTH), row),
            pl.BlockSpec((None, tm, FOX_WIDTH), row),
            pl.BlockSpec((None, tm, FOX_WIDTH), row),
            pl.BlockSpec((None, FOX_HEADS, tm), lambda i, j: (i, 0, j)),
        ],
        out_shape=[
            jax.ShapeDtypeStruct((b, s, SSM_WIDTH), BF16),
            jax.ShapeDtypeStruct((b, s, FOX_WIDTH), BF16),
            jax.ShapeDtypeStruct((b, s, FOX_WIDTH), BF16),
            jax.ShapeDtypeStruct((b, s, FOX_WIDTH), BF16),
            jax.ShapeDtypeStruct((b, FOX_HEADS, s), F32),
        ],
        scratch_shapes=[pltpu.VMEM((FOX_HEADS, LANES), F32)],
        compiler_params=pltpu.CompilerParams(
            dimension_semantics=("arbitrary", "arbitrary"), vmem_limit_bytes=VMEM_LIMIT_BYTES),
        name="inproj",
    )(x, mix_pre_g.reshape(1, d), w_main, wf_t, f_bias.reshape(FOX_HEADS, 1), tri)


def _gelu_tanh(x):
    c = math.sqrt(2.0 / math.pi)
    return 0.5 * x * (1.0 + jnp.tanh(c * (x + 0.044715 * (x * x * x))))


TILE_PITCH = 20
TIME_PITCH = 324
SCAN_UNROLL = 8


def _ssm_kernel(u_ref, bblk_ref, cblk_ref, are_ref, aim_ref, d_ref, gw_ref, gb_ref, g_ref,
                o_ref, s_ref, xc_ref, *, nb, tm):
    @pl.when(pl.program_id(0) == 0)
    def _():
        xc_ref[...] = jnp.zeros_like(xc_ref)

    def rows_of_tile(b, part, k):
        return pl.ds(TILE_PITCH * k + 2 * b + part, tm, stride=TIME_PITCH)

    for ut in range(U_TILES):
        ub = u_ref[:, :, ut * LANES:(ut + 1) * LANES].reshape(nb * tm, LANES)
        res = _dot(ub, bblk_ref[ut])
        for b in range(nb):
            for j in range(2 * U_TILES):
                part, k = divmod(j, U_TILES)
                s_ref[rows_of_tile(b, part, U_TILES * ut + k), :] = (
                    res[b * tm:(b + 1) * tm, j * LANES:(j + 1) * LANES])

    ar = are_ref[...]
    ai = aim_ref[...]

    def step(t, carry):
        new = []
        for b in range(nb):
            xr, xi = carry[2 * b], carry[2 * b + 1]
            ire = pl.ds(TIME_PITCH * t + 2 * b, RE_TILES, stride=TILE_PITCH)
            iim = pl.ds(TIME_PITCH * t + 2 * b + 1, RE_TILES, stride=TILE_PITCH)
            nr = ar * xr - ai * xi + s_ref[ire, :]
            ni = ar * xi + ai * xr + s_ref[iim, :]
            s_ref[ire, :] = nr
            s_ref[iim, :] = ni
            new += [nr, ni]
        return tuple(new)

    init = []
    for b in range(nb):
        init.append(xc_ref[b * STATE_TILES:b * STATE_TILES + RE_TILES, :])
        init.append(xc_ref[b * STATE_TILES + RE_TILES:(b + 1) * STATE_TILES, :])
    fin = lax.fori_loop(0, tm, step, tuple(init), unroll=SCAN_UNROLL)
    for b in range(nb):
        xc_ref[b * STATE_TILES:b * STATE_TILES + RE_TILES, :] = fin[2 * b]
        xc_ref[b * STATE_TILES + RE_TILES:(b + 1) * STATE_TILES, :] = fin[2 * b + 1]

    ys = []
    for ot in range(U_TILES):
        per_seq = []
        for b in range(nb):
            tiles = [s_ref[rows_of_tile(b, part, U_TILES * ot + k), :].astype(BF16)
                     for part in range(2) for k in range(U_TILES)]
            per_seq.append(jnp.concatenate(tiles, axis=1))
        ys.append(_dot(jnp.concatenate(per_seq, axis=0), cblk_ref[ot]))
    u_all = u_ref[...].reshape(nb * tm, SSM_WIDTH).astype(F32)
    y = jnp.concatenate(ys, axis=1) + d_ref[...] * u_all
    g = _gelu_tanh(y)
    out = g * _sigmoid(_dot(g.astype(BF16), gw_ref[...]) + gb_ref[...])
    o_ref[...] = _rms(out, g_ref[...]).astype(BF16).reshape(nb, tm, SSM_WIDTH)


def _ssm(u, bblk, cblk, a_re, a_im, d_skip, glu_w, glu_b, out_g, tm):
    b, s, w = u.shape
    assert 2 * b <= TILE_PITCH - 4 and TIME_PITCH >= TILE_PITCH * (RE_TILES - 1) + 2 * b
    kern = functools.partial(_ssm_kernel, nb=b, tm=tm)
    c2 = lambda t: (0, 0)
    c3 = lambda t: (0, 0, 0)
    return pl.pallas_call(
        kern,
        grid=(s // tm,),
        in_specs=[
            pl.BlockSpec((b, tm, w), lambda t: (0, t, 0)),
            pl.BlockSpec(bblk.shape, c3),
            pl.BlockSpec(cblk.shape, c3),
            pl.BlockSpec(a_re.shape, c2),
            pl.BlockSpec(a_im.shape, c2),
            pl.BlockSpec((1, w), c2),
            pl.BlockSpec(glu_w.shape, c2),
            pl.BlockSpec((1, w), c2),
            pl.BlockSpec((1, w), c2),
        ],
        out_specs=pl.BlockSpec((b, tm, w), lambda t: (0, t, 0)),
        out_shape=jax.ShapeDtypeStruct((b, s, w), BF16),
        scratch_shapes=[
            pltpu.VMEM((tm * TIME_PITCH, LANES), F32),
            pltpu.VMEM((b * STATE_TILES, LANES), F32),
        ],
        compiler_params=pltpu.CompilerParams(
            dimension_semantics=("arbitrary",), vmem_limit_bytes=VMEM_LIMIT_BYTES),
        name="ssm",
    )(u, bblk, cblk, a_re, a_im, d_skip.reshape(1, w), glu_w, glu_b.reshape(1, w),
      out_g.reshape(1, w))


def _ssm_params(a_re, a_im, log_dt, b_re, b_im, c_re, c_im):
    a = lax.complex(a_re.astype(F32), a_im.astype(F32))
    dt = jnp.exp(log_dt.astype(F32))[:, None]
    a_bar = jnp.exp(a * dt)
    b_bar = ((a_bar - 1.0) / a)[..., None] * lax.complex(b_re.astype(F32), b_im.astype(F32))
    gpt = LANES // SSM_GROUP_CH
    eye = jnp.eye(gpt, dtype=F32)

    def bmat(part):
        p = part.reshape(U_TILES, gpt, SSM_STATE, SSM_GROUP_CH)
        m = jnp.einsum('ugni,gh->ugihn', p, eye)
        return m.reshape(U_TILES, LANES, gpt * SSM_STATE)

    def cmat(part):
        p = part.reshape(U_TILES, gpt, SSM_GROUP_CH, SSM_STATE)
        m = jnp.einsum('ugin,gh->ugnhi', p, eye)
        return m.reshape(U_TILES, gpt * SSM_STATE, LANES)

    bblk = jnp.concatenate([bmat(jnp.real(b_bar)), bmat(jnp.imag(b_bar))], axis=2).astype(BF16)
    cblk = jnp.concatenate([cmat(c_re.astype(F32)), cmat(-c_im.astype(F32))], axis=1).astype(BF16)
    are = jnp.real(a_bar).reshape(RE_TILES, LANES)
    aim = jnp.imag(a_bar).reshape(RE_TILES, LANES)
    return bblk, cblk, are, aim


BIAS_LANES = 8


def _split3(f):
    p1 = f.astype(BF16).astype(F32)
    r = f - p1
    p2 = r.astype(BF16).astype(F32)
    p3 = (r - p2).astype(BF16).astype(F32)
    return p1, p2, p3


def _bias_cols(f):
    n = f.shape[1]
    one = jnp.ones((1, n), F32)
    zero = jnp.zeros((2, n), F32)
    pieces = _split3(f)
    rows = []
    for key_side in (True, False):
        for h in range(2):
            ph = [p[h:h + 1, :] for p in pieces]
            rows += [one] * 3 + [-p for p in ph] if key_side else ph + [one] * 3
            rows.append(zero)
    pt = jnp.concatenate(rows, axis=0).astype(BF16)
    r = lax.broadcasted_iota(jnp.int32, (4 * BIAS_LANES, 2 * LANES), 0)
    c = lax.broadcasted_iota(jnp.int32, (4 * BIAS_LANES, 2 * LANES), 1)
    place = (c == jnp.where(r < 2 * BIAS_LANES, r, r + LANES - 2 * BIAS_LANES)).astype(BF16)
    cols = lax.dot_general(pt, place, (((0,), (0,)), ((), ())), preferred_element_type=F32)
    return cols[:, :LANES].astype(BF16), cols[:, LANES:].astype(BF16)


V_ROWS = 144
ONES_ROW = 2 * FOX_HEAD_DIM
QUERY_SLAB = 256
SKIP_EXPONENT = 152.0


def _fox_kernel(q_ref, k_ref, v_ref, frow_ref, o_ref, kaug_ref, qaug_ref, vt_ref, j0_ref,
                lhs_ref, s_ref, m_ref, acc_ref, *, t):
    i = pl.program_id(2)
    nq = pl.num_programs(2)
    seq = k_ref.shape[0]
    lane = lax.broadcasted_iota(jnp.int32, (t, LANES), 1)
    first = lane < FOX_HEAD_DIM

    def eye(rows, cols):
        return (lax.broadcasted_iota(jnp.int32, (rows, cols), 0) ==
                lax.broadcasted_iota(jnp.int32, (rows, cols), 1)).astype(BF16)

    def store_lhs(qi):
        r0 = pl.multiple_of(qi * t, t)
        q = q_ref[pl.ds(r0, t), :]
        bias = qaug_ref[pl.ds(r0, t), :]
        zero = jnp.zeros_like(q)
        fzero = jnp.zeros_like(bias)
        slot = qi & 1
        lhs_ref[slot, :t, :] = jnp.concatenate(
            [jnp.where(first, q, zero), jnp.where(lane < BIAS_LANES, bias, fzero)], axis=1)
        lhs_ref[slot, t:, :] = jnp.concatenate(
            [jnp.where(first, zero, q), jnp.where(lane < BIAS_LANES, fzero, bias)], axis=1)

    def reset_stats():
        m_ref[...] = jnp.full_like(m_ref, NEG_INF)
        acc_ref[...] = jnp.zeros_like(acc_ref)

    def head_stats():
        qk = jnp.concatenate([q_ref[...], k_ref[...]], axis=1)
        qkt = _dot_nt(eye(2 * LANES, 2 * LANES), qk)
        qt, kt = qkt[:LANES], qkt[LANES:]

        def per_head(x):
            return jnp.concatenate([jnp.sum(x[:FOX_HEAD_DIM], axis=0, keepdims=True),
                                    jnp.sum(x[FOX_HEAD_DIM:], axis=0, keepdims=True)], axis=0)
        return per_head(qt * qt), per_head(kt * kt), per_head(qt * kt)

    def first_block(qi, qn2, kmax2, diag):
        r0 = qi * t
        f = frow_ref[...]
        rows = jnp.sqrt(qn2[:, r0:r0 + t] * kmax2) + f[:, r0:r0 + t] - diag[:, r0:r0 + t]
        tau = jnp.max(rows, axis=1, keepdims=True) + SKIP_EXPONENT
        pos = lax.broadcasted_iota(jnp.int32, (1, seq), 1)
        far = (f[0:1, :] > tau[0:1, :]) & (f[1:2, :] > tau[1:2, :])
        ends = ((pos & (t - 1)) == t - 1) & (pos < r0)
        return jnp.sum((far & ends).astype(jnp.int32))

    def step(jc, jp, causal, qp=None):
        slot = (i if qp is None else qp) & 1
        if jc is not None:
            vt = vt_ref[jc]
        if jp is not None:
            k0 = pl.multiple_of(jp * t, t)
            rk = jnp.concatenate([k_ref[pl.ds(k0, t), :], kaug_ref[pl.ds(k0, t), :]], axis=1)
        for c0 in range(0, 2 * t, QUERY_SLAB):
            sl = slice(c0, c0 + QUERY_SLAB)
            if jc is not None:
                s = s_ref[:, sl]
                m_old = m_ref[:, sl]
                m_new = jnp.maximum(m_old, jnp.max(s, axis=0, keepdims=True))
                alpha = jnp.exp2(m_old - m_new)
                p = jnp.exp2(s - m_new).astype(BF16)
                acc_ref[:, sl] = alpha * acc_ref[:, sl] + _dot(vt, p)
                m_ref[:, sl] = m_new
            if jp is not None:
                s = _dot_nt(rk, lhs_ref[slot, sl, :])
                if causal is not False:
                    ahead = 0 if causal is True else causal
                    key = lax.broadcasted_iota(jnp.int32, s.shape, 0)
                    qry = (lax.broadcasted_iota(jnp.int32, s.shape, 1) + c0) & (t - 1)
                    s = jnp.where(qry + ahead >= key, s, NEG_INF)
                s_ref[:, sl] = s

    @pl.when(i == 0)
    def _():
        kaug_ref[...], qaug_ref[...] = _bias_cols(frow_ref[...])
        qn2, kn2, diag = head_stats()
        kmax2 = jnp.max(kn2, axis=1, keepdims=True)
        for qi in range(seq // t):
            j0_ref[qi] = first_block(qi, qn2, kmax2, diag)
        vt = _dot_nt(eye(V_ROWS, LANES), v_ref[...])
        row = lax.broadcasted_iota(jnp.int32, vt.shape, 0)
        vt = jnp.where(row == ONES_ROW, 1.0, vt).astype(BF16)
        for jb in range(seq // t):
            vt_ref[jb] = vt[:, jb * t:(jb + 1) * t]
        store_lhs(0)
        reset_stats()
        step(None, 0, True)

    j0 = j0_ref[i]
    n_steady = jnp.maximum(i - 1 - j0, 0)

    def steady_pair(jj, _):
        j = j0 + 2 * jj
        step(j, j + 1, False)
        step(j + 1, j + 2, False)
        return 0
    lax.fori_loop(0, n_steady >> 1, steady_pair, 0)

    def finish():
        nxt = jnp.minimum(i + 1, nq - 1)
        j0_nxt = j0_ref[nxt]
        store_lhs(nxt)
        step(i, j0_nxt, (nxt - j0_nxt) * t, nxt)

        acc = acc_ref[...]
        inv_l = 1.0 / acc[ONES_ROW:ONES_ROW + 1, :]
        out_t = jnp.concatenate([acc[:FOX_HEAD_DIM, :t] * inv_l[:, :t],
                                 acc[FOX_HEAD_DIM:ONES_ROW, t:] * inv_l[:, t:]], axis=0)
        o_ref[...] = lax.dot_general(out_t.astype(BF16), eye(LANES, LANES), (((0,), (0,)), ((), ())),
                                     preferred_element_type=F32).astype(BF16)
        reset_stats()

    odd = (n_steady & 1) == 1

    @pl.when(odd)
    def _():
        step(i - 2, i - 1, False)
        step(i - 1, i, True)
        finish()

    @pl.when((i > j0) & jnp.logical_not(odd))
    def _():
        step(i - 1, i, True)
        finish()

    @pl.when(i <= j0)
    def _():
        finish()


def _fox(q, k, v, frow, t):
    b, s, w = q.shape
    assert s % t == 0 and t & (t - 1) == 0
    kern = functools.partial(_fox_kernel, t=t)
    whole = lambda bi, p, i: (bi, 0, p)
    return pl.pallas_call(
        kern,
        grid=(b, FOX_PAIRS, s // t),
        in_specs=[
            pl.BlockSpec((None, s, LANES), whole),
            pl.BlockSpec((None, s, LANES), whole),
            pl.BlockSpec((None, s, LANES), whole),
            pl.BlockSpec((None, None, 2, s), lambda bi, p, i: (bi, p, 0, 0)),
        ],
        out_specs=pl.BlockSpec((None, t, LANES), lambda bi, p, i: (bi, i, p)),
        out_shape=jax.ShapeDtypeStruct((b, s, w), BF16),
        scratch_shapes=[
            pltpu.VMEM((s, LANES), BF16),
            pltpu.VMEM((s, LANES), BF16),
            pltpu.VMEM((s // t, V_ROWS, t), BF16),
            pltpu.SMEM((s // t,), jnp.int32),
            pltpu.VMEM((2, 2 * t, 2 * LANES), BF16),
            pltpu.VMEM((t, 2 * t), F32),
            pltpu.VMEM((1, 2 * t), F32),
            pltpu.VMEM((V_ROWS, 2 * t), F32),
        ],
        compiler_params=pltpu.CompilerParams(
            dimension_semantics=("arbitrary", "arbitrary", "arbitrary"),
            vmem_limit_bytes=VMEM_LIMIT_BYTES),
        name="fox",
    )(q, k, v, frow)


TAIL_SUB_ROWS = 256
FFN_CHUNK = 1024


def _tail_kernel(x_ref, ys_ref, yf_ref, kx_ref, vx_ref,
                 fox_g_ref, w_out_ref, mix_post_ref, xa_pre_ref, wq_ref, wo_ref, xa_post_ref,
                 ffn_pre_ref, wg_ref, wu_ref, wd_ref, ffn_post_ref, o_ref, *, n_sub):
    sub = x_ref.shape[0] // n_sub
    rows = [slice(r * sub, (r + 1) * sub) for r in range(n_sub)]
    each = lambda f, *lists: [f(*a) for a in zip(*lists)]

    x = [x_ref[r, :] for r in rows]
    yf = [_rms(yf_ref[r, :].astype(F32), fox_g_ref[...]).astype(BF16) for r in rows]
    mix = [_dot(ys_ref[r, :], w_out_ref[:SSM_WIDTH, :]) + _dot(f, w_out_ref[SSM_WIDTH:, :])
           for r, f in zip(rows, yf)]
    x = each(lambda xi, mi: xi + _rms(mi, mix_post_ref[...]), x, mix)

    h = [_rms(xi, xa_pre_ref[...]).astype(BF16) for xi in x]
    q = [_dot(hi, wq_ref[...]).astype(BF16) for hi in h]
    heads = [[] for _ in rows]
    for hd in range(XA_HEADS):
        sl = slice(hd * XA_HEAD_DIM, (hd + 1) * XA_HEAD_DIM)
        for qi, out in zip(q, heads):
            s = _dot_nt(qi[:, sl], kx_ref[:, sl])
            p = jnp.exp(s - jnp.max(s, axis=-1, keepdims=True))
            p = p / jnp.sum(p, axis=-1, keepdims=True)
            out.append(_dot(p.astype(BF16), vx_ref[:, sl]).astype(BF16))
    o = [jnp.concatenate(hs, axis=1) for hs in heads]
    x = each(lambda xi, oi: xi + _rms(_dot(oi, wo_ref[...]), xa_post_ref[...]), x, o)

    h = [_rms(xi, ffn_pre_ref[...]).astype(BF16) for xi in x]
    hidden = wg_ref.shape[1]
    down = [None] * n_sub
    for c0 in range(0, hidden, FFN_CHUNK):
        c1 = min(c0 + FFN_CHUNK, hidden)
        for r, hi in enumerate(h):
            gate = _dot(hi, wg_ref[:, c0:c1])
            up = _dot(hi, wu_ref[:, c0:c1])
            act = (gate * _sigmoid(gate) * up).astype(BF16)
            part = _dot(act, wd_ref[c0:c1, :])
            down[r] = part if down[r] is None else down[r] + part
    for r, xi, di in zip(rows, x, down):
        o_ref[r, :] = xi + _rms(di, ffn_post_ref[...])


def _tail(x, ys, yf, kx, vx, fox_g, w_out, mix_post_g, xa_pre_g, wq, wo, xa_post_g,
          ffn_pre_g, wg, wu, wd, ffn_post_g, tm):
    b, s, d = x.shape
    m = kx.shape[1]
    row = lambda i, j: (i, j, 0)
    per_b = lambda i, j: (i, 0, 0)
    const = lambda i, j: (0, 0)

    def resident(a):
        return pl.BlockSpec(a.shape, const, pipeline_mode=pl.Buffered(1))

    gains = [g.reshape(1, -1) for g in (fox_g, mix_post_g, xa_pre_g, xa_post_g, ffn_pre_g, ffn_post_g)]
    fox_g, mix_post_g, xa_pre_g, xa_post_g, ffn_pre_g, ffn_post_g = gains
    args = (x, ys, yf, kx, vx, fox_g, w_out, mix_post_g, xa_pre_g, wq, wo, xa_post_g,
            ffn_pre_g, wg, wu, wd, ffn_post_g)
    in_specs = [
        pl.BlockSpec((None, tm, d), row),
        pl.BlockSpec((None, tm, SSM_WIDTH), row),
        pl.BlockSpec((None, tm, FOX_WIDTH), row),
        pl.BlockSpec((None, m, d), per_b),
        pl.BlockSpec((None, m, d), per_b),
    ] + [resident(a) for a in args[5:]]
    return pl.pallas_call(
        functools.partial(_tail_kernel, n_sub=tm // TAIL_SUB_ROWS),
        grid=(b, s // tm),
        in_specs=in_specs,
        out_specs=pl.BlockSpec((None, tm, d), row),
        out_shape=jax.ShapeDtypeStruct((b, s, d), F32),
        compiler_params=pltpu.CompilerParams(
            dimension_semantics=("arbitrary", "arbitrary"), vmem_limit_bytes=VMEM_LIMIT_BYTES),
        name="tail",
    )(*args)


def _pick(n, pref):
    t = min(n, pref)
    assert n % t == 0, (n, t)
    return t


def kernel(x, mem, mix_pre_g, w_in, ssm_a_re, ssm_a_im, ssm_log_dt, ssm_b_re, ssm_b_im, ssm_c_re, ssm_c_im, ssm_d, ssm_glu_w, ssm_glu_b, fox_f_bias, ssm_out_g, fox_out_g, w_out, mix_post_g, xa_pre_g, mem_g, xa_wq, xa_wkv, xa_wo, xa_post_g, ffn_pre_g, w_gate, w_up, w_down, ffn_post_g):
    b, s, d = x.shape
    assert d == D_MODEL and s % LANES == 0
    n_main = SSM_WIDTH + 3 * FOX_WIDTH

    kx, vx = _memkv(mem, mem_g, xa_wkv.astype(BF16))

    u, q, k, v, fcum = _inproj(
        x, mix_pre_g, w_in[:, :n_main].astype(BF16), w_in[:, n_main:].T.astype(BF16),
        fox_f_bias, tm=_pick(s, 1024))

    bblk, cblk, are, aim = _ssm_params(ssm_a_re, ssm_a_im, ssm_log_dt, ssm_b_re, ssm_b_im,
                                       ssm_c_re, ssm_c_im)
    y_ssm = _ssm(u, bblk, cblk, are, aim, ssm_d, ssm_glu_w.astype(BF16), ssm_glu_b, ssm_out_g,
                 tm=_pick(s, 128))

    y_fox = _fox(q, k, v, fcum.reshape(b, FOX_PAIRS, 2, s), t=_pick(s, 512))

    return _tail(x, y_ssm, y_fox, kx, vx, fox_out_g, w_out.astype(BF16), mix_post_g, xa_pre_g,
                 xa_wq.astype(BF16), xa_wo.astype(BF16), xa_post_g, ffn_pre_g,
                 w_gate.astype(BF16), w_up.astype(BF16), w_down.astype(BF16), ffn_post_g,
                 tm=_pick(s, 2 * TAIL_SUB_ROWS))
```

```python
import functools
import math

import jax
import jax.numpy as jnp
from jax import lax
from jax.experimental import pallas as pl
from jax.experimental.pallas import tpu as pltpu

F32 = jnp.float32
BF16 = jnp.bfloat16

LANES = 128
SUBLANES = 8
VMEM_LIMIT_BYTES = 56 * 1024 * 1024

D_MODEL = 1024
SSM_WIDTH = 512
SSM_GROUP_CH = 16
SSM_GROUPS = 32
SSM_STATE = 64
FOX_WIDTH = 512
FOX_HEAD_DIM = 64
FOX_HEADS = 8
FOX_PAIRS = FOX_HEADS * FOX_HEAD_DIM // LANES
XA_HEADS = 4
XA_HEAD_DIM = 256
RMS_EPS = 1e-6
NEG_INF = -1e30
LOG2E = math.log2(math.e)

STATE_TILES = 2 * SSM_GROUPS * SSM_STATE // LANES
RE_TILES = STATE_TILES // 2
U_TILES = SSM_WIDTH // LANES


def _rms(x, g):
    return x * lax.rsqrt(jnp.mean(x * x, axis=-1, keepdims=True) + RMS_EPS) * g


def _dot(a, b):
    return jnp.dot(a, b, preferred_element_type=F32)


def _dot_nt(a, b):
    return lax.dot_general(a, b, (((1,), (1,)), ((), ())), preferred_element_type=F32)


def _sigmoid(z):
    return 1.0 / (1.0 + jnp.exp(-z))


def _memkv_kernel(mem_ref, g_ref, wkv_ref, wq_ref, wo_ref, wqk_ref, vo_ref):
    m = mem_ref.shape[0]
    mn = _rms(mem_ref[...], g_ref[...]).astype(BF16)
    kv = _dot(mn, wkv_ref[...])
    for hd in range(XA_HEADS):
        sl = slice(hd * XA_HEAD_DIM, (hd + 1) * XA_HEAD_DIM)
        k_h = (kv[:, sl] * (1.0 / math.sqrt(XA_HEAD_DIM))).astype(BF16)
        v_h = kv[:, D_MODEL + hd * XA_HEAD_DIM:D_MODEL + (hd + 1) * XA_HEAD_DIM].astype(BF16)
        wqk_ref[:, hd * m:(hd + 1) * m] = _dot_nt(wq_ref[:, sl], k_h).astype(BF16)
        vo_ref[hd * m:(hd + 1) * m, :] = _dot(v_h, wo_ref[sl, :]).astype(BF16)


def _memkv(mem, mem_g, wkv, wq, wo):
    b, m, d = mem.shape
    c2 = lambda i: (0, 0)
    return pl.pallas_call(
        _memkv_kernel,
        grid=(b,),
        in_specs=[
            pl.BlockSpec((None, m, d), lambda i: (i, 0, 0)),
            pl.BlockSpec((1, d), c2),
            pl.BlockSpec((d, 2 * d), c2),
            pl.BlockSpec((d, d), c2),
            pl.BlockSpec((d, d), c2),
        ],
        out_specs=[
            pl.BlockSpec((None, d, XA_HEADS * m), lambda i: (i, 0, 0)),
            pl.BlockSpec((None, XA_HEADS * m, d), lambda i: (i, 0, 0)),
        ],
        out_shape=[jax.ShapeDtypeStruct((b, d, XA_HEADS * m), BF16),
                   jax.ShapeDtypeStruct((b, XA_HEADS * m, d), BF16)],
        compiler_params=pltpu.CompilerParams(
            dimension_semantics=("arbitrary",), vmem_limit_bytes=VMEM_LIMIT_BYTES),
        name="memkv",
    )(mem, mem_g.reshape(1, d), wkv, wq, wo)


CUMSUM_SEGMENT = 512


def _inproj_kernel(x_ref, g_ref, w_ref, wf_ref, fb_ref, tri_ref,
                   u_ref, q_ref, k_ref, v_ref, f_ref, carry_ref):
    @pl.when(pl.program_id(1) == 0)
    def _():
        carry_ref[...] = jnp.zeros_like(carry_ref)

    hb = _rms(x_ref[...], g_ref[...]).astype(BF16)
    proj = _dot(hb, w_ref[...])
    u_ref[...] = proj[:, :SSM_WIDTH].astype(BF16)
    o = SSM_WIDTH
    q_ref[...] = (proj[:, o:o + FOX_WIDTH] * (LOG2E / math.sqrt(FOX_HEAD_DIM))).astype(BF16)
    k_ref[...] = proj[:, o + FOX_WIDTH:o + 2 * FOX_WIDTH].astype(BF16)
    v_ref[...] = proj[:, o + 2 * FOX_WIDTH:].astype(BF16)

    z = _dot_nt(wf_ref[...], hb) + fb_ref[...]
    logf = jnp.minimum(z, 0.0) - jnp.log1p(jnp.exp(-jnp.abs(z)))
    p1 = logf.astype(BF16)
    r1 = logf - p1.astype(F32)
    p2 = r1.astype(BF16)
    p3 = (r1 - p2.astype(F32)).astype(BF16)
    tri = tri_ref[...]
    seg = tri.shape[0]
    run = carry_ref[...][:, :1]
    parts = []
    for c0 in range(0, logf.shape[1], seg):
        sl = slice(c0, c0 + seg)
        parts.append(_dot(p1[:, sl], tri) + _dot(p2[:, sl], tri) + _dot(p3[:, sl], tri) + run)
        run = parts[-1][:, -1:]
    f = jnp.concatenate(parts, axis=1)
    f_ref[...] = f * LOG2E
    carry_ref[...] = jnp.broadcast_to(f[:, -1:], carry_ref.shape)


def _inproj(x, mix_pre_g, w_main, wf_t, f_bias, tm):
    b, s, d = x.shape
    nt = s // tm
    seg = min(tm, CUMSUM_SEGMENT)
    tri = jnp.triu(jnp.ones((seg, seg), F32)).astype(BF16)
    row = lambda i, j: (i, j, 0)
    const2 = lambda i, j: (0, 0)
    return pl.pallas_call(
        _inproj_kernel,
        grid=(b, nt),
        in_specs=[
            pl.BlockSpec((None, tm, d), row),
            pl.BlockSpec((1, d), const2),
            pl.BlockSpec(w_main.shape, const2),
            pl.BlockSpec(wf_t.shape, const2),
            pl.BlockSpec((FOX_HEADS, 1), const2),
            pl.BlockSpec((seg, seg), const2),
        ],
        out_specs=[
            pl.BlockSpec((None, tm, SSM_WIDTH), row),
            pl.BlockSpec((None, tm, FOX_WIDTH), row),
            pl.BlockSpec((None, tm, FOX_WIDTH), row),
            pl.BlockSpec((None, tm, FOX_WIDTH), row),
            pl.BlockSpec((None, FOX_HEADS, tm), lambda i, j: (i, 0, j)),
        ],
        out_shape=[
            jax.ShapeDtypeStruct((b, s, SSM_WIDTH), BF16),
            jax.ShapeDtypeStruct((b, s, FOX_WIDTH), BF16),
            jax.ShapeDtypeStruct((b, s, FOX_WIDTH), BF16),
            jax.ShapeDtypeStruct((b, s, FOX_WIDTH), BF16),
            jax.ShapeDtypeStruct((b, FOX_HEADS, s), F32),
        ],
        scratch_shapes=[pltpu.VMEM((FOX_HEADS, LANES), F32)],
        compiler_params=pltpu.CompilerParams(
            dimension_semantics=("arbitrary", "arbitrary"), vmem_limit_bytes=VMEM_LIMIT_BYTES),
        name="inproj",
    )(x, mix_pre_g.reshape(1, d), w_main, wf_t, f_bias.reshape(FOX_HEADS, 1), tri)


def _gelu_tanh(x):
    c = math.sqrt(2.0 / math.pi)
    return 0.5 * x * (1.0 + jnp.tanh(c * (x + 0.044715 * (x * x * x))))


TILE_PITCH = 20
TIME_PITCH = 324
SCAN_UNROLL = 8


def _ssm_kernel(u_ref, bblk_ref, cblk_ref, are_ref, aim_ref, d_ref, gw_ref, gb_ref, g_ref,
                o_ref, s_ref, xc_ref, *, nb, tm):
    @pl.when(pl.program_id(0) == 0)
    def _():
        xc_ref[...] = jnp.zeros_like(xc_ref)

    def rows_of_tile(b, part, k):
        return pl.ds(TILE_PITCH * k + 2 * b + part, tm, stride=TIME_PITCH)

    for ut in range(U_TILES):
        ub = u_ref[:, :, ut * LANES:(ut + 1) * LANES].reshape(nb * tm, LANES)
        res = _dot(ub, bblk_ref[ut])
        for b in range(nb):
            for j in range(2 * U_TILES):
                part, k = divmod(j, U_TILES)
                s_ref[rows_of_tile(b, part, U_TILES * ut + k), :] = (
                    res[b * tm:(b + 1) * tm, j * LANES:(j + 1) * LANES])

    ar = are_ref[...]
    ai = aim_ref[...]

    def step(t, carry):
        new = []
        for b in range(nb):
            xr, xi = carry[2 * b], carry[2 * b + 1]
            ire = pl.ds(TIME_PITCH * t + 2 * b, RE_TILES, stride=TILE_PITCH)
            iim = pl.ds(TIME_PITCH * t + 2 * b + 1, RE_TILES, stride=TILE_PITCH)
            nr = ar * xr - ai * xi + s_ref[ire, :]
            ni = ar * xi + ai * xr + s_ref[iim, :]
            s_ref[ire, :] = nr
            s_ref[iim, :] = ni
            new += [nr, ni]
        return tuple(new)

    init = []
    for b in range(nb):
        init.append(xc_ref[b * STATE_TILES:b * STATE_TILES + RE_TILES, :])
        init.append(xc_ref[b * STATE_TILES + RE_TILES:(b + 1) * STATE_TILES, :])
    fin = lax.fori_loop(0, tm, step, tuple(init), unroll=SCAN_UNROLL)
    for b in range(nb):
        xc_ref[b * STATE_TILES:b * STATE_TILES + RE_TILES, :] = fin[2 * b]
        xc_ref[b * STATE_TILES + RE_TILES:(b + 1) * STATE_TILES, :] = fin[2 * b + 1]

    ys = []
    for ot in range(U_TILES):
        per_seq = []
        for b in range(nb):
            tiles = [s_ref[rows_of_tile(b, part, U_TILES * ot + k), :].astype(BF16)
                     for part in range(2) for k in range(U_TILES)]
            per_seq.append(jnp.concatenate(tiles, axis=1))
        ys.append(_dot(jnp.concatenate(per_seq, axis=0), cblk_ref[ot]))
    u_all = u_ref[...].reshape(nb * tm, SSM_WIDTH).astype(F32)
    y = jnp.concatenate(ys, axis=1) + d_ref[...] * u_all
    g = _gelu_tanh(y)
    out = g * _sigmoid(_dot(g.astype(BF16), gw_ref[...]) + gb_ref[...])
    o_ref[...] = _rms(out, g_ref[...]).astype(BF16).reshape(nb, tm, SSM_WIDTH)


def _ssm(u, bblk, cblk, a_re, a_im, d_skip, glu_w, glu_b, out_g, tm):
    b, s, w = u.shape
    assert 2 * b <= TILE_PITCH - 4 and TIME_PITCH >= TILE_PITCH * (RE_TILES - 1) + 2 * b
    kern = functools.partial(_ssm_kernel, nb=b, tm=tm)
    c2 = lambda t: (0, 0)
    c3 = lambda t: (0, 0, 0)
    return pl.pallas_call(
        kern,
        grid=(s // tm,),
        in_specs=[
            pl.BlockSpec((b, tm, w), lambda t: (0, t, 0)),
            pl.BlockSpec(bblk.shape, c3),
            pl.BlockSpec(cblk.shape, c3),
            pl.BlockSpec(a_re.shape, c2),
            pl.BlockSpec(a_im.shape, c2),
            pl.BlockSpec((1, w), c2),
            pl.BlockSpec(glu_w.shape, c2),
            pl.BlockSpec((1, w), c2),
            pl.BlockSpec((1, w), c2),
        ],
        out_specs=pl.BlockSpec((b, tm, w), lambda t: (0, t, 0)),
        out_shape=jax.ShapeDtypeStruct((b, s, w), BF16),
        scratch_shapes=[
            pltpu.VMEM((tm * TIME_PITCH, LANES), F32),
            pltpu.VMEM((b * STATE_TILES, LANES), F32),
        ],
        compiler_params=pltpu.CompilerParams(
            dimension_semantics=("arbitrary",), vmem_limit_bytes=VMEM_LIMIT_BYTES),
        name="ssm",
    )(u, bblk, cblk, a_re, a_im, d_skip.reshape(1, w), glu_w, glu_b.reshape(1, w),
      out_g.reshape(1, w))


def _ssm_params(a_re, a_im, log_dt, b_re, b_im, c_re, c_im):
    a = lax.complex(a_re.astype(F32), a_im.astype(F32))
    dt = jnp.exp(log_dt.astype(F32))[:, None]
    a_bar = jnp.exp(a * dt)
    b_bar = ((a_bar - 1.0) / a)[..., None] * lax.complex(b_re.astype(F32), b_im.astype(F32))
    gpt = LANES // SSM_GROUP_CH
    eye = jnp.eye(gpt, dtype=F32)

    def bmat(part):
        p = part.reshape(U_TILES, gpt, SSM_STATE, SSM_GROUP_CH)
        m = jnp.einsum('ugni,gh->ugihn', p, eye)
        return m.reshape(U_TILES, LANES, gpt * SSM_STATE)

    def cmat(part):
        p = part.reshape(U_TILES, gpt, SSM_GROUP_CH, SSM_STATE)
        m = jnp.einsum('ugin,gh->ugnhi', p, eye)
        return m.reshape(U_TILES, gpt * SSM_STATE, LANES)

    bblk = jnp.concatenate([bmat(jnp.real(b_bar)), bmat(jnp.imag(b_bar))], axis=2).astype(BF16)
    cblk = jnp.concatenate([cmat(c_re.astype(F32)), cmat(-c_im.astype(F32))], axis=1).astype(BF16)
    are = jnp.real(a_bar).reshape(RE_TILES, LANES)
    aim = jnp.imag(a_bar).reshape(RE_TILES, LANES)
    return bblk, cblk, are, aim


BIAS_LANES = 8


def _split3(f):
    p1 = f.astype(BF16).astype(F32)
    r = f - p1
    p2 = r.astype(BF16).astype(F32)
    p3 = (r - p2).astype(BF16).astype(F32)
    return p1, p2, p3


def _bias_cols(f):
    n = f.shape[1]
    one = jnp.ones((1, n), F32)
    zero = jnp.zeros((2, n), F32)
    pieces = _split3(f)
    rows = []
    for key_side in (True, False):
        for h in range(2):
            ph = [p[h:h + 1, :] for p in pieces]
            rows += [one] * 3 + [-p for p in ph] if key_side else ph + [one] * 3
            rows.append(zero)
    pt = jnp.concatenate(rows, axis=0).astype(BF16)
    r = lax.broadcasted_iota(jnp.int32, (4 * BIAS_LANES, 2 * LANES), 0)
    c = lax.broadcasted_iota(jnp.int32, (4 * BIAS_LANES, 2 * LANES), 1)
    place = (c == jnp.where(r < 2 * BIAS_LANES, r, r + LANES - 2 * BIAS_LANES)).astype(BF16)
    cols = lax.dot_general(pt, place, (((0,), (0,)), ((), ())), preferred_element_type=F32)
    return cols[:, :LANES].astype(BF16), cols[:, LANES:].astype(BF16)


V_ROWS = 144
ONES_ROW = 2 * FOX_HEAD_DIM
QUERY_SLAB = 256
SKIP_EXPONENT = 152.0


def _fox_kernel(q_ref, k_ref, v_ref, frow_ref, o_ref, kaug_ref, qaug_ref, vt_ref, j0_ref,
                lhs_ref, s_ref, m_ref, acc_ref, *, t):
    i = pl.program_id(2)
    nq = pl.num_programs(2)
    seq = k_ref.shape[0]
    lane = lax.broadcasted_iota(jnp.int32, (t, LANES), 1)
    first = lane < FOX_HEAD_DIM

    def eye(rows, cols):
        return (lax.broadcasted_iota(jnp.int32, (rows, cols), 0) ==
                lax.broadcasted_iota(jnp.int32, (rows, cols), 1)).astype(BF16)

    def store_lhs(qi):
        r0 = pl.multiple_of(qi * t, t)
        q = q_ref[pl.ds(r0, t), :]
        bias = qaug_ref[pl.ds(r0, t), :]
        zero = jnp.zeros_like(q)
        fzero = jnp.zeros_like(bias)
        slot = qi & 1
        lhs_ref[slot, :t, :] = jnp.concatenate(
            [jnp.where(first, q, zero), jnp.where(lane < BIAS_LANES, bias, fzero)], axis=1)
        lhs_ref[slot, t:, :] = jnp.concatenate(
            [jnp.where(first, zero, q), jnp.where(lane < BIAS_LANES, fzero, bias)], axis=1)

    def reset_stats():
        m_ref[...] = jnp.full_like(m_ref, NEG_INF)
        acc_ref[...] = jnp.zeros_like(acc_ref)

    def head_stats():
        qk = jnp.concatenate([q_ref[...], k_ref[...]], axis=1)
        qkt = _dot_nt(eye(2 * LANES, 2 * LANES), qk)
        qt, kt = qkt[:LANES], qkt[LANES:]

        def per_head(x):
            return jnp.concatenate([jnp.sum(x[:FOX_HEAD_DIM], axis=0, keepdims=True),
                                    jnp.sum(x[FOX_HEAD_DIM:], axis=0, keepdims=True)], axis=0)
        return per_head(qt * qt), per_head(kt * kt), per_head(qt * kt)

    def first_block(qi, qn2, kmax2, diag):
        r0 = qi * t
        f = frow_ref[...]
        rows = jnp.sqrt(qn2[:, r0:r0 + t] * kmax2) + f[:, r0:r0 + t] - diag[:, r0:r0 + t]
        tau = jnp.max(rows, axis=1, keepdims=True) + SKIP_EXPONENT
        pos = lax.broadcasted_iota(jnp.int32, (1, seq), 1)
        far = (f[0:1, :] > tau[0:1, :]) & (f[1:2, :] > tau[1:2, :])
        ends = ((pos & (t - 1)) == t - 1) & (pos < r0)
        return jnp.sum((far & ends).astype(jnp.int32))

    def step(jc, jp, causal, qp=None):
        slot = (i if qp is None else qp) & 1
        if jc is not None:
            vt = vt_ref[jc]
        if jp is not None:
            k0 = pl.multiple_of(jp * t, t)
            rk = jnp.concatenate([k_ref[pl.ds(k0, t), :], kaug_ref[pl.ds(k0, t), :]], axis=1)
        for c0 in range(0, 2 * t, QUERY_SLAB):
            sl = slice(c0, c0 + QUERY_SLAB)
            if jc is not None:
                s = s_ref[:, sl]
                m_old = m_ref[:, sl]
                m_new = jnp.maximum(m_old, jnp.max(s, axis=0, keepdims=True))
                alpha = jnp.exp2(m_old - m_new)
                p = jnp.exp2(s - m_new).astype(BF16)
                acc_ref[:, sl] = alpha * acc_ref[:, sl] + _dot(vt, p)
                m_ref[:, sl] = m_new
            if jp is not None:
                s = _dot_nt(rk, lhs_ref[slot, sl, :])
                if causal is not False:
                    ahead = 0 if causal is True else causal
                    key = lax.broadcasted_iota(jnp.int32, s.shape, 0)
                    qry = (lax.broadcasted_iota(jnp.int32, s.shape, 1) + c0) & (t - 1)
                    s = jnp.where(qry + ahead >= key, s, NEG_INF)
                s_ref[:, sl] = s

    @pl.when(i == 0)
    def _():
        kaug_ref[...], qaug_ref[...] = _bias_cols(frow_ref[...])
        qn2, kn2, diag = head_stats()
        kmax2 = jnp.max(kn2, axis=1, keepdims=True)
        for qi in range(seq // t):
            j0_ref[qi] = first_block(qi, qn2, kmax2, diag)
        vt = _dot_nt(eye(V_ROWS, LANES), v_ref[...])
        row = lax.broadcasted_iota(jnp.int32, vt.shape, 0)
        vt = jnp.where(row == ONES_ROW, 1.0, vt).astype(BF16)
        for jb in range(seq // t):
            vt_ref[jb] = vt[:, jb * t:(jb + 1) * t]
        store_lhs(0)
        reset_stats()
        step(None, 0, True)

    j0 = j0_ref[i]
    n_steady = jnp.maximum(i - 1 - j0, 0)

    def steady_pair(jj, _):
        j = j0 + 2 * jj
        step(j, j + 1, False)
        step(j + 1, j + 2, False)
        return 0
    lax.fori_loop(0, n_steady >> 1, steady_pair, 0)

    def finish():
        nxt = jnp.minimum(i + 1, nq - 1)
        j0_nxt = j0_ref[nxt]
        store_lhs(nxt)
        step(i, j0_nxt, (nxt - j0_nxt) * t, nxt)

        acc = acc_ref[...]
        inv_l = 1.0 / acc[ONES_ROW:ONES_ROW + 1, :]
        out_t = jnp.concatenate([acc[:FOX_HEAD_DIM, :t] * inv_l[:, :t],
                                 acc[FOX_HEAD_DIM:ONES_ROW, t:] * inv_l[:, t:]], axis=0)
        o_ref[...] = lax.dot_general(out_t.astype(BF16), eye(LANES, LANES), (((0,), (0,)), ((), ())),
                                     preferred_element_type=F32).astype(BF16)
        reset_stats()

    odd = (n_steady & 1) == 1

    @pl.when(odd)
    def _():
        step(i - 2, i - 1, False)
        step(i - 1, i, True)
        finish()

    @pl.when((i > j0) & jnp.logical_not(odd))
    def _():
        step(i - 1, i, True)
        finish()

    @pl.when(i <= j0)
    def _():
        finish()


def _fox(q, k, v, frow, t):
    b, s, w = q.shape
    assert s % t == 0 and t & (t - 1) == 0
    kern = functools.partial(_fox_kernel, t=t)
    whole = lambda bi, p, i: (bi, 0, p)
    return pl.pallas_call(
        kern,
        grid=(b, FOX_PAIRS, s // t),
        in_specs=[
            pl.BlockSpec((None, s, LANES), whole),
            pl.BlockSpec((None, s, LANES), whole),
            pl.BlockSpec((None, s, LANES), whole),
            pl.BlockSpec((None, None, 2, s), lambda bi, p, i: (bi, p, 0, 0)),
        ],
        out_specs=pl.BlockSpec((None, t, LANES), lambda bi, p, i: (bi, i, p)),
        out_shape=jax.ShapeDtypeStruct((b, s, w), BF16),
        scratch_shapes=[
            pltpu.VMEM((s, LANES), BF16),
            pltpu.VMEM((s, LANES), BF16),
            pltpu.VMEM((s // t, V_ROWS, t), BF16),
            pltpu.SMEM((s // t,), jnp.int32),
            pltpu.VMEM((2, 2 * t, 2 * LANES), BF16),
            pltpu.VMEM((t, 2 * t), F32),
            pltpu.VMEM((1, 2 * t), F32),
            pltpu.VMEM((V_ROWS, 2 * t), F32),
        ],
        compiler_params=pltpu.CompilerParams(
            dimension_semantics=("arbitrary", "arbitrary", "arbitrary"),
            vmem_limit_bytes=VMEM_LIMIT_BYTES),
        name="fox",
    )(q, k, v, frow)


TAIL_SUB_ROWS = 256
FFN_CHUNK = 1024


def _tail_kernel(x_ref, ys_ref, yf_ref, wqk_ref, vo_ref,
                 fox_g_ref, w_out_ref, mix_post_ref, xa_pre_ref, xa_post_ref,
                 ffn_pre_ref, wg_ref, wu_ref, wd_ref, ffn_post_ref, o_ref, *, n_sub):
    sub = x_ref.shape[0] // n_sub
    rows = [slice(r * sub, (r + 1) * sub) for r in range(n_sub)]
    each = lambda f, *lists: [f(*a) for a in zip(*lists)]

    x = [x_ref[r, :] for r in rows]
    yf = [_rms(yf_ref[r, :].astype(F32), fox_g_ref[...]).astype(BF16) for r in rows]
    mix = [_dot(ys_ref[r, :], w_out_ref[:SSM_WIDTH, :]) + _dot(f, w_out_ref[SSM_WIDTH:, :])
           for r, f in zip(rows, yf)]
    x = each(lambda xi, mi: xi + _rms(mi, mix_post_ref[...]), x, mix)

    h = [_rms(xi, xa_pre_ref[...]).astype(BF16) for xi in x]
    scores = [_dot(hi, wqk_ref[...]) for hi in h]
    m_tok = wqk_ref.shape[1] // XA_HEADS
    probs = [[] for _ in rows]
    for hd in range(XA_HEADS):
        for si, out in zip(scores, probs):
            s = si[:, hd * m_tok:(hd + 1) * m_tok]
            p = jnp.exp(s - jnp.max(s, axis=-1, keepdims=True))
            out.append((p / jnp.sum(p, axis=-1, keepdims=True)).astype(BF16))
    p_all = [jnp.concatenate(ps, axis=1) for ps in probs]
    x = each(lambda xi, pi: xi + _rms(_dot(pi, vo_ref[...]), xa_post_ref[...]), x, p_all)

    h = [_rms(xi, ffn_pre_ref[...]).astype(BF16) for xi in x]
    hidden = wg_ref.shape[1]
    down = [None] * n_sub
    for c0 in range(0, hidden, FFN_CHUNK):
        c1 = min(c0 + FFN_CHUNK, hidden)
        for r, hi in enumerate(h):
            gate = _dot(hi, wg_ref[:, c0:c1])
            up = _dot(hi, wu_ref[:, c0:c1])
            act = (gate * _sigmoid(gate) * up).astype(BF16)
            part = _dot(act, wd_ref[c0:c1, :])
            down[r] = part if down[r] is None else down[r] + part
    for r, xi, di in zip(rows, x, down):
        o_ref[r, :] = xi + _rms(di, ffn_post_ref[...])


def _tail(x, ys, yf, wqk, vo, fox_g, w_out, mix_post_g, xa_pre_g, xa_post_g,
          ffn_pre_g, wg, wu, wd, ffn_post_g, tm):
    b, s, d = x.shape
    row = lambda i, j: (i, j, 0)
    per_b = lambda i, j: (i, 0, 0)
    const = lambda i, j: (0, 0)

    def resident(a):
        return pl.BlockSpec(a.shape, const, pipeline_mode=pl.Buffered(1))

    gains = [g.reshape(1, -1) for g in (fox_g, mix_post_g, xa_pre_g, xa_post_g, ffn_pre_g, ffn_post_g)]
    fox_g, mix_post_g, xa_pre_g, xa_post_g, ffn_pre_g, ffn_post_g = gains
    args = (x, ys, yf, wqk, vo, fox_g, w_out, mix_post_g, xa_pre_g, xa_post_g,
            ffn_pre_g, wg, wu, wd, ffn_post_g)
    in_specs = [
        pl.BlockSpec((None, tm, d), row),
        pl.BlockSpec((None, tm, SSM_WIDTH), row),
        pl.BlockSpec((None, tm, FOX_WIDTH), row),
        pl.BlockSpec((None,) + wqk.shape[1:], per_b),
        pl.BlockSpec((None,) + vo.shape[1:], per_b),
    ] + [resident(a) for a in args[5:]]
    return pl.pallas_call(
        functools.partial(_tail_kernel, n_sub=tm // TAIL_SUB_ROWS),
        grid=(b, s // tm),
        in_specs=in_specs,
        out_specs=pl.BlockSpec((None, tm, d), row),
        out_shape=jax.ShapeDtypeStruct((b, s, d), F32),
        compiler_params=pltpu.CompilerParams(
            dimension_semantics=("arbitrary", "arbitrary"), vmem_limit_bytes=VMEM_LIMIT_BYTES),
        name="tail",
    )(*args)


def _pick(n, pref):
    t = min(n, pref)
    assert n % t == 0, (n, t)
    return t


def kernel(x, mem, mix_pre_g, w_in, ssm_a_re, ssm_a_im, ssm_log_dt, ssm_b_re, ssm_b_im, ssm_c_re, ssm_c_im, ssm_d, ssm_glu_w, ssm_glu_b, fox_f_bias, ssm_out_g, fox_out_g, w_out, mix_post_g, xa_pre_g, mem_g, xa_wq, xa_wkv, xa_wo, xa_post_g, ffn_pre_g, w_gate, w_up, w_down, ffn_post_g):
    b, s, d = x.shape
    assert d == D_MODEL and s % LANES == 0
    n_main = SSM_WIDTH + 3 * FOX_WIDTH

    wqk, vo = _memkv(mem, mem_g, xa_wkv.astype(BF16), xa_wq.astype(BF16), xa_wo.astype(BF16))

    u, q, k, v, fcum = _inproj(
        x, mix_pre_g, w_in[:, :n_main].astype(BF16), w_in[:, n_main:].T.astype(BF16),
        fox_f_bias, tm=_pick(s, 1024))

    bblk, cblk, are, aim = _ssm_params(ssm_a_re, ssm_a_im, ssm_log_dt, ssm_b_re, ssm_b_im,
                                       ssm_c_re, ssm_c_im)
    y_ssm = _ssm(u, bblk, cblk, are, aim, ssm_d, ssm_glu_w.astype(BF16), ssm_glu_b, ssm_out_g,
                 tm=_pick(s, 128))

    y_fox = _fox(q, k, v, fcum.reshape(b, FOX_PAIRS, 2, s), t=_pick(s, 512))

    return _tail(x, y_ssm, y_fox, wqk, vo, fox_out_g, w_out.astype(BF16), mix_post_g, xa_pre_g,
                 xa_post_g, ffn_pre_g,
                 w_gate.astype(BF16), w_up.astype(BF16), w_down.astype(BF16), ffn_post_g,
                 tm=_pick(s, 2 * TAIL_SUB_ROWS))
```

```python
import functools
import math

import jax
import jax.numpy as jnp
from jax import lax
from jax.experimental import pallas as pl
from jax.experimental.pallas import tpu as pltpu

F32 = jnp.float32
BF16 = jnp.bfloat16

LANES = 128
SUBLANES = 8
VMEM_LIMIT_BYTES = 56 * 1024 * 1024

D_MODEL = 1024
SSM_WIDTH = 512
SSM_GROUP_CH = 16
SSM_GROUPS = 32
SSM_STATE = 64
FOX_WIDTH = 512
FOX_HEAD_DIM = 64
FOX_HEADS = 8
FOX_PAIRS = FOX_HEADS * FOX_HEAD_DIM // LANES
XA_HEADS = 4
XA_HEAD_DIM = 256
RMS_EPS = 1e-6
NEG_INF = -1e30
LOG2E = math.log2(math.e)

STATE_TILES = 2 * SSM_GROUPS * SSM_STATE // LANES
RE_TILES = STATE_TILES // 2
U_TILES = SSM_WIDTH // LANES


def _rms(x, g):
    return x * lax.rsqrt(jnp.mean(x * x, axis=-1, keepdims=True) + RMS_EPS) * g


def _dot(a, b):
    return jnp.dot(a, b, preferred_element_type=F32)


def _dot_nt(a, b):
    return lax.dot_general(a, b, (((1,), (1,)), ((), ())), preferred_element_type=F32)


def _sigmoid(z):
    return 1.0 / (1.0 + jnp.exp(-z))


def _memkv_kernel(mem_ref, g_ref, wkv_ref, wq_ref, wo_ref, wqk_ref, vo_ref):
    m = mem_ref.shape[0]
    mn = _rms(mem_ref[...], g_ref[...]).astype(BF16)
    kv = _dot(mn, wkv_ref[...])
    for hd in range(XA_HEADS):
        sl = slice(hd * XA_HEAD_DIM, (hd + 1) * XA_HEAD_DIM)
        k_h = (kv[:, sl] * (1.0 / math.sqrt(XA_HEAD_DIM))).astype(BF16)
        v_h = kv[:, D_MODEL + hd * XA_HEAD_DIM:D_MODEL + (hd + 1) * XA_HEAD_DIM].astype(BF16)
        wqk_ref[:, hd * m:(hd + 1) * m] = _dot_nt(wq_ref[:, sl], k_h).astype(BF16)
        vo_ref[hd * m:(hd + 1) * m, :] = _dot(v_h, wo_ref[sl, :]).astype(BF16)


def _memkv(mem, mem_g, wkv, wq, wo):
    b, m, d = mem.shape
    c2 = lambda i: (0, 0)
    return pl.pallas_call(
        _memkv_kernel,
        grid=(b,),
        in_specs=[
            pl.BlockSpec((None, m, d), lambda i: (i, 0, 0)),
            pl.BlockSpec((1, d), c2),
            pl.BlockSpec((d, 2 * d), c2),
            pl.BlockSpec((d, d), c2),
            pl.BlockSpec((d, d), c2),
        ],
        out_specs=[
            pl.BlockSpec((None, d, XA_HEADS * m), lambda i: (i, 0, 0)),
            pl.BlockSpec((None, XA_HEADS * m, d), lambda i: (i, 0, 0)),
        ],
        out_shape=[jax.ShapeDtypeStruct((b, d, XA_HEADS * m), BF16),
                   jax.ShapeDtypeStruct((b, XA_HEADS * m, d), BF16)],
        compiler_params=pltpu.CompilerParams(
            dimension_semantics=("arbitrary",), vmem_limit_bytes=VMEM_LIMIT_BYTES),
        name="memkv",
    )(mem, mem_g.reshape(1, d), wkv, wq, wo)


CUMSUM_SEGMENT = 512


def _inproj_kernel(x_ref, g_ref, w_ref, wf_ref, fb_ref, tri_ref,
                   u_ref, q_ref, k_ref, v_ref, f_ref, carry_ref):
    @pl.when(pl.program_id(1) == 0)
    def _():
        carry_ref[...] = jnp.zeros_like(carry_ref)

    hb = _rms(x_ref[...], g_ref[...]).astype(BF16)
    proj = _dot(hb, w_ref[...])
    u_ref[...] = proj[:, :SSM_WIDTH].astype(BF16)
    o = SSM_WIDTH
    q_ref[...] = (proj[:, o:o + FOX_WIDTH] * (LOG2E / math.sqrt(FOX_HEAD_DIM))).astype(BF16)
    k_ref[...] = proj[:, o + FOX_WIDTH:o + 2 * FOX_WIDTH].astype(BF16)
    v_ref[...] = proj[:, o + 2 * FOX_WIDTH:].astype(BF16)

    z = _dot_nt(wf_ref[...], hb) + fb_ref[...]
    logf = jnp.minimum(z, 0.0) - jnp.log1p(jnp.exp(-jnp.abs(z)))
    p1 = logf.astype(BF16)
    r1 = logf - p1.astype(F32)
    p2 = r1.astype(BF16)
    p3 = (r1 - p2.astype(F32)).astype(BF16)
    tri = tri_ref[...]
    seg = tri.shape[0]
    run = carry_ref[...][:, :1]
    parts = []
    for c0 in range(0, logf.shape[1], seg):
        sl = slice(c0, c0 + seg)
        parts.append(_dot(p1[:, sl], tri) + _dot(p2[:, sl], tri) + _dot(p3[:, sl], tri) + run)
        run = parts[-1][:, -1:]
    f = jnp.concatenate(parts, axis=1)
    f_ref[...] = f * LOG2E
    carry_ref[...] = jnp.broadcast_to(f[:, -1:], carry_ref.shape)


def _inproj(x, mix_pre_g, w_main, wf_t, f_bias, tm):
    b, s, d = x.shape
    nt = s // tm
    seg = min(tm, CUMSUM_SEGMENT)
    tri = jnp.triu(jnp.ones((seg, seg), F32)).astype(BF16)
    row = lambda i, j: (i, j, 0)
    const2 = lambda i, j: (0, 0)
    return pl.pallas_call(
        _inproj_kernel,
        grid=(b, nt),
        in_specs=[
            pl.BlockSpec((None, tm, d), row),
            pl.BlockSpec((1, d), const2),
            pl.BlockSpec(w_main.shape, const2),
            pl.BlockSpec(wf_t.shape, const2),
            pl.BlockSpec((FOX_HEADS, 1), const2),
            pl.BlockSpec((seg, seg), const2),
        ],
        out_specs=[
            pl.BlockSpec((None, tm, SSM_WIDTH), row),
            pl.BlockSpec((None, tm, FOX_WIDTH), row),
            pl.BlockSpec((None, tm, FOX_WIDTH), row),
            pl.BlockSpec((None, tm, FOX_WIDTH), row),
            pl.BlockSpec((None, FOX_HEADS, tm), lambda i, j: (i, 0, j)),
        ],
        out_shape=[
            jax.ShapeDtypeStruct((b, s, SSM_WIDTH), BF16),
            jax.ShapeDtypeStruct((b, s, FOX_WIDTH), BF16),
            jax.ShapeDtypeStruct((b, s, FOX_WIDTH), BF16),
            jax.ShapeDtypeStruct((b, s, FOX_WIDTH), BF16),
            jax.ShapeDtypeStruct((b, FOX_HEADS, s), F32),
        ],
        scratch_shapes=[pltpu.VMEM((FOX_HEADS, LANES), F32)],
        compiler_params=pltpu.CompilerParams(
            dimension_semantics=("arbitrary", "arbitrary"), vmem_limit_bytes=VMEM_LIMIT_BYTES),
        name="inproj",
    )(x, mix_pre_g.reshape(1, d), w_main, wf_t, f_bias.reshape(FOX_HEADS, 1), tri)


def _gelu_tanh(x):
    c = math.sqrt(2.0 / math.pi)
    return 0.5 * x * (1.0 + jnp.tanh(c * (x + 0.044715 * (x * x * x))))


TILE_PITCH = 20
TIME_PITCH = 324
SCAN_UNROLL = 8


def _ssm_kernel(u_ref, bblk_ref, cblk_ref, are_ref, aim_ref, d_ref, gw_ref, gb_ref, g_ref,
                o_ref, s_ref, xc_ref, *, nb, tm):
    @pl.when(pl.program_id(0) == 0)
    def _():
        xc_ref[...] = jnp.zeros_like(xc_ref)

    def rows_of_tile(b, part, k):
        return pl.ds(TILE_PITCH * k + 2 * b + part, tm, stride=TIME_PITCH)

    for ut in range(U_TILES):
        ub = u_ref[:, :, ut * LANES:(ut + 1) * LANES].reshape(nb * tm, LANES)
        res = _dot(ub, bblk_ref[ut])
        for b in range(nb):
            for j in range(2 * U_TILES):
                part, k = divmod(j, U_TILES)
                s_ref[rows_of_tile(b, part, U_TILES * ut + k), :] = (
                    res[b * tm:(b + 1) * tm, j * LANES:(j + 1) * LANES])

    ar = are_ref[...]
    ai = aim_ref[...]

    def step(t, carry):
        new = []
        for b in range(nb):
            xr, xi = carry[2 * b], carry[2 * b + 1]
            ire = pl.ds(TIME_PITCH * t + 2 * b, RE_TILES, stride=TILE_PITCH)
            iim = pl.ds(TIME_PITCH * t + 2 * b + 1, RE_TILES, stride=TILE_PITCH)
            nr = ar * xr - ai * xi + s_ref[ire, :]
            ni = ar * xi + ai * xr + s_ref[iim, :]
            s_ref[ire, :] = nr
            s_ref[iim, :] = ni
            new += [nr, ni]
        return tuple(new)

    init = []
    for b in range(nb):
        init.append(xc_ref[b * STATE_TILES:b * STATE_TILES + RE_TILES, :])
        init.append(xc_ref[b * STATE_TILES + RE_TILES:(b + 1) * STATE_TILES, :])
    fin = lax.fori_loop(0, tm, step, tuple(init), unroll=SCAN_UNROLL)
    for b in range(nb):
        xc_ref[b * STATE_TILES:b * STATE_TILES + RE_TILES, :] = fin[2 * b]
        xc_ref[b * STATE_TILES + RE_TILES:(b + 1) * STATE_TILES, :] = fin[2 * b + 1]

    ys = []
    for ot in range(U_TILES):
        per_seq = []
        for b in range(nb):
            tiles = [s_ref[rows_of_tile(b, part, U_TILES * ot + k), :].astype(BF16)
                     for part in range(2) for k in range(U_TILES)]
            per_seq.append(jnp.concatenate(tiles, axis=1))
        ys.append(_dot(jnp.concatenate(per_seq, axis=0), cblk_ref[ot]))
    u_all = u_ref[...].reshape(nb * tm, SSM_WIDTH).astype(F32)
    y = jnp.concatenate(ys, axis=1) + d_ref[...] * u_all
    g = _gelu_tanh(y)
    out = g * _sigmoid(_dot(g.astype(BF16), gw_ref[...]) + gb_ref[...])
    o_ref[...] = _rms(out, g_ref[...]).astype(BF16).reshape(nb, tm, SSM_WIDTH)


def _ssm(u, bblk, cblk, a_re, a_im, d_skip, glu_w, glu_b, out_g, tm):
    b, s, w = u.shape
    assert 2 * b <= TILE_PITCH - 4 and TIME_PITCH >= TILE_PITCH * (RE_TILES - 1) + 2 * b
    kern = functools.partial(_ssm_kernel, nb=b, tm=tm)
    c2 = lambda t: (0, 0)
    c3 = lambda t: (0, 0, 0)
    return pl.pallas_call(
        kern,
        grid=(s // tm,),
        in_specs=[
            pl.BlockSpec((b, tm, w), lambda t: (0, t, 0)),
            pl.BlockSpec(bblk.shape, c3),
            pl.BlockSpec(cblk.shape, c3),
            pl.BlockSpec(a_re.shape, c2),
            pl.BlockSpec(a_im.shape, c2),
            pl.BlockSpec((1, w), c2),
            pl.BlockSpec(glu_w.shape, c2),
            pl.BlockSpec((1, w), c2),
            pl.BlockSpec((1, w), c2),
        ],
        out_specs=pl.BlockSpec((b, tm, w), lambda t: (0, t, 0)),
        out_shape=jax.ShapeDtypeStruct((b, s, w), BF16),
        scratch_shapes=[
            pltpu.VMEM((tm * TIME_PITCH, LANES), F32),
            pltpu.VMEM((b * STATE_TILES, LANES), F32),
        ],
        compiler_params=pltpu.CompilerParams(
            dimension_semantics=("arbitrary",), vmem_limit_bytes=VMEM_LIMIT_BYTES),
        name="ssm",
    )(u, bblk, cblk, a_re, a_im, d_skip.reshape(1, w), glu_w, glu_b.reshape(1, w),
      out_g.reshape(1, w))


def _ssm_params(a_re, a_im, log_dt, b_re, b_im, c_re, c_im):
    a = lax.complex(a_re.astype(F32), a_im.astype(F32))
    dt = jnp.exp(log_dt.astype(F32))[:, None]
    a_bar = jnp.exp(a * dt)
    b_bar = ((a_bar - 1.0) / a)[..., None] * lax.complex(b_re.astype(F32), b_im.astype(F32))
    gpt = LANES // SSM_GROUP_CH
    eye = jnp.eye(gpt, dtype=F32)

    def bmat(part):
        p = part.reshape(U_TILES, gpt, SSM_STATE, SSM_GROUP_CH)
        m = jnp.einsum('ugni,gh->ugihn', p, eye)
        return m.reshape(U_TILES, LANES, gpt * SSM_STATE)

    def cmat(part):
        p = part.reshape(U_TILES, gpt, SSM_GROUP_CH, SSM_STATE)
        m = jnp.einsum('ugin,gh->ugnhi', p, eye)
        return m.reshape(U_TILES, gpt * SSM_STATE, LANES)

    bblk = jnp.concatenate([bmat(jnp.real(b_bar)), bmat(jnp.imag(b_bar))], axis=2).astype(BF16)
    cblk = jnp.concatenate([cmat(c_re.astype(F32)), cmat(-c_im.astype(F32))], axis=1).astype(BF16)
    are = jnp.real(a_bar).reshape(RE_TILES, LANES)
    aim = jnp.imag(a_bar).reshape(RE_TILES, LANES)
    return bblk, cblk, are, aim


BIAS_LANES = 8


def _split3(f):
    p1 = f.astype(BF16).astype(F32)
    r = f - p1
    p2 = r.astype(BF16).astype(F32)
    p3 = (r - p2).astype(BF16).astype(F32)
    return p1, p2, p3


def _bias_cols(f):
    n = f.shape[1]
    one = jnp.ones((1, n), F32)
    zero = jnp.zeros((2, n), F32)
    pieces = _split3(f)
    rows = []
    for key_side in (True, False):
        for h in range(2):
            ph = [p[h:h + 1, :] for p in pieces]
            rows += [one] * 3 + [-p for p in ph] if key_side else ph + [one] * 3
            rows.append(zero)
    pt = jnp.concatenate(rows, axis=0).astype(BF16)
    r = lax.broadcasted_iota(jnp.int32, (4 * BIAS_LANES, 2 * LANES), 0)
    c = lax.broadcasted_iota(jnp.int32, (4 * BIAS_LANES, 2 * LANES), 1)
    place = (c == jnp.where(r < 2 * BIAS_LANES, r, r + LANES - 2 * BIAS_LANES)).astype(BF16)
    cols = lax.dot_general(pt, place, (((0,), (0,)), ((), ())), preferred_element_type=F32)
    return cols[:, :LANES].astype(BF16), cols[:, LANES:].astype(BF16)


V_ROWS = 144
ONES_ROW = 2 * FOX_HEAD_DIM
QUERY_SLAB = 256
SKIP_EXPONENT = 1e30


def _fox_kernel(q_ref, k_ref, v_ref, frow_ref, o_ref, kaug_ref, qaug_ref, vt_ref, j0_ref,
                lhs_ref, s_ref, m_ref, acc_ref, *, t):
    i = pl.program_id(2)
    nq = pl.num_programs(2)
    seq = k_ref.shape[0]
    lane = lax.broadcasted_iota(jnp.int32, (t, LANES), 1)
    first = lane < FOX_HEAD_DIM

    def eye(rows, cols):
        return (lax.broadcasted_iota(jnp.int32, (rows, cols), 0) ==
                lax.broadcasted_iota(jnp.int32, (rows, cols), 1)).astype(BF16)

    def store_lhs(qi):
        r0 = pl.multiple_of(qi * t, t)
        q = q_ref[pl.ds(r0, t), :]
        bias = qaug_ref[pl.ds(r0, t), :]
        zero = jnp.zeros_like(q)
        fzero = jnp.zeros_like(bias)
        slot = qi & 1
        lhs_ref[slot, :t, :] = jnp.concatenate(
            [jnp.where(first, q, zero), jnp.where(lane < BIAS_LANES, bias, fzero)], axis=1)
        lhs_ref[slot, t:, :] = jnp.concatenate(
            [jnp.where(first, zero, q), jnp.where(lane < BIAS_LANES, fzero, bias)], axis=1)

    def reset_stats():
        m_ref[...] = jnp.full_like(m_ref, NEG_INF)
        acc_ref[...] = jnp.zeros_like(acc_ref)

    def head_stats():
        qk = jnp.concatenate([q_ref[...], k_ref[...]], axis=1)
        qkt = _dot_nt(eye(2 * LANES, 2 * LANES), qk)
        qt, kt = qkt[:LANES], qkt[LANES:]

        def per_head(x):
            return jnp.concatenate([jnp.sum(x[:FOX_HEAD_DIM], axis=0, keepdims=True),
                                    jnp.sum(x[FOX_HEAD_DIM:], axis=0, keepdims=True)], axis=0)
        return per_head(qt * qt), per_head(kt * kt), per_head(qt * kt)

    def first_block(qi, qn2, kmax2, diag):
        r0 = qi * t
        f = frow_ref[...]
        rows = jnp.sqrt(qn2[:, r0:r0 + t] * kmax2) + f[:, r0:r0 + t] - diag[:, r0:r0 + t]
        tau = jnp.max(rows, axis=1, keepdims=True) + SKIP_EXPONENT
        pos = lax.broadcasted_iota(jnp.int32, (1, seq), 1)
        far = (f[0:1, :] > tau[0:1, :]) & (f[1:2, :] > tau[1:2, :])
        ends = ((pos & (t - 1)) == t - 1) & (pos < r0)
        return jnp.sum((far & ends).astype(jnp.int32))

    def step(jc, jp, causal, qp=None):
        slot = (i if qp is None else qp) & 1
        if jc is not None:
            vt = vt_ref[jc]
        if jp is not None:
            k0 = pl.multiple_of(jp * t, t)
            rk = jnp.concatenate([k_ref[pl.ds(k0, t), :], kaug_ref[pl.ds(k0, t), :]], axis=1)
        for c0 in range(0, 2 * t, QUERY_SLAB):
            sl = slice(c0, c0 + QUERY_SLAB)
            if jc is not None:
                s = s_ref[:, sl]
                m_old = m_ref[:, sl]
                m_new = jnp.maximum(m_old, jnp.max(s, axis=0, keepdims=True))
                alpha = jnp.exp2(m_old - m_new)
                p = jnp.exp2(s - m_new).astype(BF16)
                acc_ref[:, sl] = alpha * acc_ref[:, sl] + _dot(vt, p)
                m_ref[:, sl] = m_new
            if jp is not None:
                s = _dot_nt(rk, lhs_ref[slot, sl, :])
                if causal is not False:
                    ahead = 0 if causal is True else causal
                    key = lax.broadcasted_iota(jnp.int32, s.shape, 0)
                    qry = (lax.broadcasted_iota(jnp.int32, s.shape, 1) + c0) & (t - 1)
                    s = jnp.where(qry + ahead >= key, s, NEG_INF)
                s_ref[:, sl] = s

    @pl.when(i == 0)
    def _():
        kaug_ref[...], qaug_ref[...] = _bias_cols(frow_ref[...])
        qn2, kn2, diag = head_stats()
        kmax2 = jnp.max(kn2, axis=1, keepdims=True)
        for qi in range(seq // t):
            j0_ref[qi] = first_block(qi, qn2, kmax2, diag)
        vt = _dot_nt(eye(V_ROWS, LANES), v_ref[...])
        row = lax.broadcasted_iota(jnp.int32, vt.shape, 0)
        vt = jnp.where(row == ONES_ROW, 1.0, vt).astype(BF16)
        for jb in range(seq // t):
            vt_ref[jb] = vt[:, jb * t:(jb + 1) * t]
        store_lhs(0)
        reset_stats()
        step(None, 0, True)

    j0 = j0_ref[i]
    n_steady = jnp.maximum(i - 1 - j0, 0)

    def steady_pair(jj, _):
        j = j0 + 2 * jj
        step(j, j + 1, False)
        step(j + 1, j + 2, False)
        return 0
    lax.fori_loop(0, n_steady >> 1, steady_pair, 0)

    def finish():
        nxt = jnp.minimum(i + 1, nq - 1)
        j0_nxt = j0_ref[nxt]
        store_lhs(nxt)
        step(i, j0_nxt, (nxt - j0_nxt) * t, nxt)

        acc = acc_ref[...]
        inv_l = 1.0 / acc[ONES_ROW:ONES_ROW + 1, :]
        out_t = jnp.concatenate([acc[:FOX_HEAD_DIM, :t] * inv_l[:, :t],
                                 acc[FOX_HEAD_DIM:ONES_ROW, t:] * inv_l[:, t:]], axis=0)
        o_ref[...] = lax.dot_general(out_t.astype(BF16), eye(LANES, LANES), (((0,), (0,)), ((), ())),
                                     preferred_element_type=F32).astype(BF16)
        reset_stats()

    odd = (n_steady & 1) == 1

    @pl.when(odd)
    def _():
        step(i - 2, i - 1, False)
        step(i - 1, i, True)
        finish()

    @pl.when((i > j0) & jnp.logical_not(odd))
    def _():
        step(i - 1, i, True)
        finish()

    @pl.when(i <= j0)
    def _():
        finish()


def _fox(q, k, v, frow, t):
    b, s, w = q.shape
    assert s % t == 0 and t & (t - 1) == 0
    kern = functools.partial(_fox_kernel, t=t)
    whole = lambda bi, p, i: (bi, 0, p)
    return pl.pallas_call(
        kern,
        grid=(b, FOX_PAIRS, s // t),
        in_specs=[
            pl.BlockSpec((None, s, LANES), whole),
            pl.BlockSpec((None, s, LANES), whole),
            pl.BlockSpec((None, s, LANES), whole),
            pl.BlockSpec((None, None, 2, s), lambda bi, p, i: (bi, p, 0, 0)),
        ],
        out_specs=pl.BlockSpec((None, t, LANES), lambda bi, p, i: (bi, i, p)),
        out_shape=jax.ShapeDtypeStruct((b, s, w), BF16),
        scratch_shapes=[
            pltpu.VMEM((s, LANES), BF16),
            pltpu.VMEM((s, LANES), BF16),
            pltpu.VMEM((s // t, V_ROWS, t), BF16),
            pltpu.SMEM((s // t,), jnp.int32),
            pltpu.VMEM((2, 2 * t, 2 * LANES), BF16),
            pltpu.VMEM((t, 2 * t), F32),
            pltpu.VMEM((1, 2 * t), F32),
            pltpu.VMEM((V_ROWS, 2 * t), F32),
        ],
        compiler_params=pltpu.CompilerParams(
            dimension_semantics=("arbitrary", "arbitrary", "arbitrary"),
            vmem_limit_bytes=VMEM_LIMIT_BYTES),
        name="fox",
    )(q, k, v, frow)


TAIL_SUB_ROWS = 256
FFN_CHUNK = 1024


def _tail_kernel(x_ref, ys_ref, yf_ref, wqk_ref, vo_ref,
                 fox_g_ref, w_out_ref, mix_post_ref, xa_pre_ref, xa_post_ref,
                 ffn_pre_ref, wg_ref, wu_ref, wd_ref, ffn_post_ref, o_ref, *, n_sub):
    sub = x_ref.shape[0] // n_sub
    rows = [slice(r * sub, (r + 1) * sub) for r in range(n_sub)]
    each = lambda f, *lists: [f(*a) for a in zip(*lists)]

    x = [x_ref[r, :] for r in rows]
    yf = [_rms(yf_ref[r, :].astype(F32), fox_g_ref[...]).astype(BF16) for r in rows]
    mix = [_dot(ys_ref[r, :], w_out_ref[:SSM_WIDTH, :]) + _dot(f, w_out_ref[SSM_WIDTH:, :])
           for r, f in zip(rows, yf)]
    x = each(lambda xi, mi: xi + _rms(mi, mix_post_ref[...]), x, mix)

    h = [_rms(xi, xa_pre_ref[...]).astype(BF16) for xi in x]
    scores = [_dot(hi, wqk_ref[...]) for hi in h]
    m_tok = wqk_ref.shape[1] // XA_HEADS
    probs = [[] for _ in rows]
    for hd in range(XA_HEADS):
        for si, out in zip(scores, probs):
            s = si[:, hd * m_tok:(hd + 1) * m_tok]
            p = jnp.exp(s - jnp.max(s, axis=-1, keepdims=True))
            out.append((p / jnp.sum(p, axis=-1, keepdims=True)).astype(BF16))
    p_all = [jnp.concatenate(ps, axis=1) for ps in probs]
    x = each(lambda xi, pi: xi + _rms(_dot(pi, vo_ref[...]), xa_post_ref[...]), x, p_all)

    h = [_rms(xi, ffn_pre_ref[...]).astype(BF16) for xi in x]
    hidden = wg_ref.shape[1]
    down = [None] * n_sub
    for c0 in range(0, hidden, FFN_CHUNK):
        c1 = min(c0 + FFN_CHUNK, hidden)
        for r, hi in enumerate(h):
            gate = _dot(hi, wg_ref[:, c0:c1])
            up = _dot(hi, wu_ref[:, c0:c1])
            act = (gate * _sigmoid(gate) * up).astype(BF16)
            part = _dot(act, wd_ref[c0:c1, :])
            down[r] = part if down[r] is None else down[r] + part
    for r, xi, di in zip(rows, x, down):
        o_ref[r, :] = xi + _rms(di, ffn_post_ref[...])


def _tail(x, ys, yf, wqk, vo, fox_g, w_out, mix_post_g, xa_pre_g, xa_post_g,
          ffn_pre_g, wg, wu, wd, ffn_post_g, tm):
    b, s, d = x.shape
    row = lambda i, j: (i, j, 0)
    per_b = lambda i, j: (i, 0, 0)
    const = lambda i, j: (0, 0)

    def resident(a):
        return pl.BlockSpec(a.shape, const, pipeline_mode=pl.Buffered(1))

    gains = [g.reshape(1, -1) for g in (fox_g, mix_post_g, xa_pre_g, xa_post_g, ffn_pre_g, ffn_post_g)]
    fox_g, mix_post_g, xa_pre_g, xa_post_g, ffn_pre_g, ffn_post_g = gains
    args = (x, ys, yf, wqk, vo, fox_g, w_out, mix_post_g, xa_pre_g, xa_post_g,
            ffn_pre_g, wg, wu, wd, ffn_post_g)
    in_specs = [
        pl.BlockSpec((None, tm, d), row),
        pl.BlockSpec((None, tm, SSM_WIDTH), row),
        pl.BlockSpec((None, tm, FOX_WIDTH), row),
        pl.BlockSpec((None,) + wqk.shape[1:], per_b),
        pl.BlockSpec((None,) + vo.shape[1:], per_b),
    ] + [resident(a) for a in args[5:]]
    return pl.pallas_call(
        functools.partial(_tail_kernel, n_sub=tm // TAIL_SUB_ROWS),
        grid=(b, s // tm),
        in_specs=in_specs,
        out_specs=pl.BlockSpec((None, tm, d), row),
        out_shape=jax.ShapeDtypeStruct((b, s, d), F32),
        compiler_params=pltpu.CompilerParams(
            dimension_semantics=("arbitrary", "arbitrary"), vmem_limit_bytes=VMEM_LIMIT_BYTES),
        name="tail",
    )(*args)


def _pick(n, pref):
    t = min(n, pref)
    assert n % t == 0, (n, t)
    return t


def kernel(x, mem, mix_pre_g, w_in, ssm_a_re, ssm_a_im, ssm_log_dt, ssm_b_re, ssm_b_im, ssm_c_re, ssm_c_im, ssm_d, ssm_glu_w, ssm_glu_b, fox_f_bias, ssm_out_g, fox_out_g, w_out, mix_post_g, xa_pre_g, mem_g, xa_wq, xa_wkv, xa_wo, xa_post_g, ffn_pre_g, w_gate, w_up, w_down, ffn_post_g):
    b, s, d = x.shape
    assert d == D_MODEL and s % LANES == 0
    n_main = SSM_WIDTH + 3 * FOX_WIDTH

    wqk, vo = _memkv(mem, mem_g, xa_wkv.astype(BF16), xa_wq.astype(BF16), xa_wo.astype(BF16))

    u, q, k, v, fcum = _inproj(
        x, mix_pre_g, w_in[:, :n_main].astype(BF16), w_in[:, n_main:].T.astype(BF16),
        fox_f_bias, tm=_pick(s, 1024))

    bblk, cblk, are, aim = _ssm_params(ssm_a_re, ssm_a_im, ssm_log_dt, ssm_b_re, ssm_b_im,
                                       ssm_c_re, ssm_c_im)
    y_ssm = _ssm(u, bblk, cblk, are, aim, ssm_d, ssm_glu_w.astype(BF16), ssm_glu_b, ssm_out_g,
                 tm=_pick(s, 128))

    y_fox = _fox(q, k, v, fcum.reshape(b, FOX_PAIRS, 2, s), t=_pick(s, 512))

    return _tail(x, y_ssm, y_fox, wqk, vo, fox_out_g, w_out.astype(BF16), mix_post_g, xa_pre_g,
                 xa_post_g, ffn_pre_g,
                 w_gate.astype(BF16), w_up.astype(BF16), w_down.astype(BF16), ffn_post_g,
                 tm=_pick(s, 2 * TAIL_SUB_ROWS))
```

```python
import functools
import math

import jax
import jax.numpy as jnp
from jax import lax
from jax.experimental import pallas as pl
from jax.experimental.pallas import tpu as pltpu

F32 = jnp.float32
BF16 = jnp.bfloat16

LANES = 128
VMEM_LIMIT_BYTES = 56 * 1024 * 1024

D_MODEL = 1024
SSM_WIDTH = 512
SSM_GROUP_CH = 16
SSM_GROUPS = 32
SSM_STATE = 64
FOX_WIDTH = 512
FOX_HEAD_DIM = 64
FOX_HEADS = 8
FOX_PAIRS = FOX_HEADS * FOX_HEAD_DIM // LANES
XA_HEADS = 4
XA_HEAD_DIM = 256
RMS_EPS = 1e-6
NEG_INF = -1e30
LOG2E = math.log2(math.e)

STATE_TILES = 2 * SSM_GROUPS * SSM_STATE // LANES
RE_TILES = STATE_TILES // 2
U_TILES = SSM_WIDTH // LANES


def _rms(x, g):
    return x * lax.rsqrt(jnp.mean(x * x, axis=-1, keepdims=True) + RMS_EPS) * g


def _dot(a, b):
    return jnp.dot(a, b, preferred_element_type=F32)


def _dot_nt(a, b):
    return lax.dot_general(a, b, (((1,), (1,)), ((), ())), preferred_element_type=F32)


def _sigmoid(z):
    return 1.0 / (1.0 + jnp.exp(-z))


def _memkv_kernel(mem_ref, g_ref, wkv_ref, wq_ref, wo_ref, wqk_ref, vo_ref):
    m = mem_ref.shape[0]
    mn = _rms(mem_ref[...], g_ref[...]).astype(BF16)
    kv = _dot(mn, wkv_ref[...])
    for hd in range(XA_HEADS):
        sl = slice(hd * XA_HEAD_DIM, (hd + 1) * XA_HEAD_DIM)
        k_h = (kv[:, sl] * (1.0 / math.sqrt(XA_HEAD_DIM))).astype(BF16)
        v_h = kv[:, D_MODEL + hd * XA_HEAD_DIM:D_MODEL + (hd + 1) * XA_HEAD_DIM].astype(BF16)
        wqk_ref[:, hd * m:(hd + 1) * m] = _dot_nt(wq_ref[:, sl], k_h).astype(BF16)
        vo_ref[hd * m:(hd + 1) * m, :] = _dot(v_h, wo_ref[sl, :]).astype(BF16)


def _memkv(mem, mem_g, wkv, wq, wo):
    b, m, d = mem.shape
    c2 = lambda i: (0, 0)
    return pl.pallas_call(
        _memkv_kernel,
        grid=(b,),
        in_specs=[
            pl.BlockSpec((None, m, d), lambda i: (i, 0, 0)),
            pl.BlockSpec((1, d), c2),
            pl.BlockSpec((d, 2 * d), c2),
            pl.BlockSpec((d, d), c2),
            pl.BlockSpec((d, d), c2),
        ],
        out_specs=[
            pl.BlockSpec((None, d, XA_HEADS * m), lambda i: (i, 0, 0)),
            pl.BlockSpec((None, XA_HEADS * m, d), lambda i: (i, 0, 0)),
        ],
        out_shape=[jax.ShapeDtypeStruct((b, d, XA_HEADS * m), BF16),
                   jax.ShapeDtypeStruct((b, XA_HEADS * m, d), BF16)],
        compiler_params=pltpu.CompilerParams(
            dimension_semantics=("arbitrary",), vmem_limit_bytes=VMEM_LIMIT_BYTES),
        name="memkv",
    )(mem, mem_g.reshape(1, d), wkv, wq, wo)


CUMSUM_SEGMENT = 512


def _inproj_kernel(x_ref, g_ref, w_ref, wf_ref, fb_ref, tri_ref,
                   u_ref, q_ref, k_ref, v_ref, f_ref, carry_ref):
    @pl.when(pl.program_id(1) == 0)
    def _():
        carry_ref[...] = jnp.zeros_like(carry_ref)

    hb = _rms(x_ref[...], g_ref[...]).astype(BF16)
    proj = _dot(hb, w_ref[...])
    u_ref[...] = proj[:, :SSM_WIDTH].astype(BF16)
    o = SSM_WIDTH
    q_ref[...] = (proj[:, o:o + FOX_WIDTH] * (LOG2E / math.sqrt(FOX_HEAD_DIM))).astype(BF16)
    k_ref[...] = proj[:, o + FOX_WIDTH:o + 2 * FOX_WIDTH].astype(BF16)
    v_ref[...] = proj[:, o + 2 * FOX_WIDTH:].astype(BF16)

    z = _dot_nt(wf_ref[...], hb) + fb_ref[...]
    logf = jnp.minimum(z, 0.0) - jnp.log1p(jnp.exp(-jnp.abs(z)))
    p1 = logf.astype(BF16)
    r1 = logf - p1.astype(F32)
    p2 = r1.astype(BF16)
    p3 = (r1 - p2.astype(F32)).astype(BF16)
    tri = tri_ref[...]
    seg = tri.shape[0]
    run = carry_ref[...][:, :1]
    parts = []
    for c0 in range(0, logf.shape[1], seg):
        sl = slice(c0, c0 + seg)
        parts.append(_dot(p1[:, sl], tri) + _dot(p2[:, sl], tri) + _dot(p3[:, sl], tri) + run)
        run = parts[-1][:, -1:]
    f = jnp.concatenate(parts, axis=1)
    f_ref[...] = f * LOG2E
    carry_ref[...] = jnp.broadcast_to(f[:, -1:], carry_ref.shape)


def _inproj(x, mix_pre_g, w_main, wf_t, f_bias, tm):
    b, s, d = x.shape
    nt = s // tm
    seg = min(tm, CUMSUM_SEGMENT)
    tri = jnp.triu(jnp.ones((seg, seg), F32)).astype(BF16)
    row = lambda i, j: (i, j, 0)
    const2 = lambda i, j: (0, 0)
    return pl.pallas_call(
        _inproj_kernel,
        grid=(b, nt),
        in_specs=[
            pl.BlockSpec((None, tm, d), row),
            pl.BlockSpec((1, d), const2),
            pl.BlockSpec(w_main.shape, const2),
            pl.BlockSpec(wf_t.shape, const2),
            pl.BlockSpec((FOX_HEADS, 1), const2),
            pl.BlockSpec((seg, seg), const2),
        ],
        out_specs=[
            pl.BlockSpec((None, tm, SSM_WIDTH), row),
            pl.BlockSpec((None, tm, FOX_WIDTH), row),
            pl.BlockSpec((None, tm, FOX_WIDTH), row),
            pl.BlockSpec((None, tm, FOX_WIDTH), row),
            pl.BlockSpec((None, FOX_HEADS, tm), lambda i, j: (i, 0, j)),
        ],
        out_shape=[
            jax.ShapeDtypeStruct((b, s, SSM_WIDTH), BF16),
            jax.ShapeDtypeStruct((b, s, FOX_WIDTH), BF16),
            jax.ShapeDtypeStruct((b, s, FOX_WIDTH), BF16),
            jax.ShapeDtypeStruct((b, s, FOX_WIDTH), BF16),
            jax.ShapeDtypeStruct((b, FOX_HEADS, s), F32),
        ],
        scratch_shapes=[pltpu.VMEM((FOX_HEADS, LANES), F32)],
        compiler_params=pltpu.CompilerParams(
            dimension_semantics=("arbitrary", "arbitrary"), vmem_limit_bytes=VMEM_LIMIT_BYTES),
        name="inproj",
    )(x, mix_pre_g.reshape(1, d), w_main, wf_t, f_bias.reshape(FOX_HEADS, 1), tri)


def _gelu_tanh(x):
    c = math.sqrt(2.0 / math.pi)
    return 0.5 * x * (1.0 + jnp.tanh(c * (x + 0.044715 * (x * x * x))))


TILE_PITCH = 20
TIME_PITCH = 324
SCAN_UNROLL = 8


def _ssm_kernel(u_ref, bblk_ref, cblk_ref, are_ref, aim_ref, d_ref, gw_ref, gb_ref, g_ref,
                o_ref, s_ref, xc_ref, *, nb, tm):
    @pl.when(pl.program_id(0) == 0)
    def _():
        xc_ref[...] = jnp.zeros_like(xc_ref)

    def rows_of_tile(b, part, k):
        return pl.ds(TILE_PITCH * k + 2 * b + part, tm, stride=TIME_PITCH)

    for ut in range(U_TILES):
        ub = u_ref[:, :, ut * LANES:(ut + 1) * LANES].reshape(nb * tm, LANES)
        res = _dot(ub, bblk_ref[ut])
        for b in range(nb):
            for j in range(2 * U_TILES):
                part, k = divmod(j, U_TILES)
                s_ref[rows_of_tile(b, part, U_TILES * ut + k), :] = (
                    res[b * tm:(b + 1) * tm, j * LANES:(j + 1) * LANES])

    ar = are_ref[...]
    ai = aim_ref[...]

    def step(t, carry):
        new = []
        for b in range(nb):
            xr, xi = carry[2 * b], carry[2 * b + 1]
            ire = pl.ds(TIME_PITCH * t + 2 * b, RE_TILES, stride=TILE_PITCH)
            iim = pl.ds(TIME_PITCH * t + 2 * b + 1, RE_TILES, stride=TILE_PITCH)
            nr = ar * xr - ai * xi + s_ref[ire, :]
            ni = ar * xi + ai * xr + s_ref[iim, :]
            s_ref[ire, :] = nr
            s_ref[iim, :] = ni
            new += [nr, ni]
        return tuple(new)

    init = []
    for b in range(nb):
        init.append(xc_ref[b * STATE_TILES:b * STATE_TILES + RE_TILES, :])
        init.append(xc_ref[b * STATE_TILES + RE_TILES:(b + 1) * STATE_TILES, :])
    fin = lax.fori_loop(0, tm, step, tuple(init), unroll=SCAN_UNROLL)
    for b in range(nb):
        xc_ref[b * STATE_TILES:b * STATE_TILES + RE_TILES, :] = fin[2 * b]
        xc_ref[b * STATE_TILES + RE_TILES:(b + 1) * STATE_TILES, :] = fin[2 * b + 1]

    ys = []
    for ot in range(U_TILES):
        per_seq = []
        for b in range(nb):
            tiles = [s_ref[rows_of_tile(b, part, U_TILES * ot + k), :].astype(BF16)
                     for part in range(2) for k in range(U_TILES)]
            per_seq.append(jnp.concatenate(tiles, axis=1))
        ys.append(_dot(jnp.concatenate(per_seq, axis=0), cblk_ref[ot]))
    u_all = u_ref[...].reshape(nb * tm, SSM_WIDTH).astype(F32)
    y = jnp.concatenate(ys, axis=1) + d_ref[...] * u_all
    g = _gelu_tanh(y)
    out = g * _sigmoid(_dot(g.astype(BF16), gw_ref[...]) + gb_ref[...])
    o_ref[...] = _rms(out, g_ref[...]).astype(BF16).reshape(nb, tm, SSM_WIDTH)


def _ssm(u, bblk, cblk, a_re, a_im, d_skip, glu_w, glu_b, out_g, tm):
    b, s, w = u.shape
    assert 2 * b <= TILE_PITCH - 4 and TIME_PITCH >= TILE_PITCH * (RE_TILES - 1) + 2 * b
    kern = functools.partial(_ssm_kernel, nb=b, tm=tm)
    c2 = lambda t: (0, 0)
    c3 = lambda t: (0, 0, 0)
    return pl.pallas_call(
        kern,
        grid=(s // tm,),
        in_specs=[
            pl.BlockSpec((b, tm, w), lambda t: (0, t, 0)),
            pl.BlockSpec(bblk.shape, c3),
            pl.BlockSpec(cblk.shape, c3),
            pl.BlockSpec(a_re.shape, c2),
            pl.BlockSpec(a_im.shape, c2),
            pl.BlockSpec((1, w), c2),
            pl.BlockSpec(glu_w.shape, c2),
            pl.BlockSpec((1, w), c2),
            pl.BlockSpec((1, w), c2),
        ],
        out_specs=pl.BlockSpec((b, tm, w), lambda t: (0, t, 0)),
        out_shape=jax.ShapeDtypeStruct((b, s, w), BF16),
        scratch_shapes=[
            pltpu.VMEM((tm * TIME_PITCH, LANES), F32),
            pltpu.VMEM((b * STATE_TILES, LANES), F32),
        ],
        compiler_params=pltpu.CompilerParams(
            dimension_semantics=("arbitrary",), vmem_limit_bytes=VMEM_LIMIT_BYTES),
        name="ssm",
    )(u, bblk, cblk, a_re, a_im, d_skip.reshape(1, w), glu_w, glu_b.reshape(1, w),
      out_g.reshape(1, w))


def _ssm_params(a_re, a_im, log_dt, b_re, b_im, c_re, c_im):
    a = lax.complex(a_re.astype(F32), a_im.astype(F32))
    dt = jnp.exp(log_dt.astype(F32))[:, None]
    a_bar = jnp.exp(a * dt)
    b_bar = ((a_bar - 1.0) / a)[..., None] * lax.complex(b_re.astype(F32), b_im.astype(F32))
    gpt = LANES // SSM_GROUP_CH
    eye = jnp.eye(gpt, dtype=F32)

    def bmat(part):
        p = part.reshape(U_TILES, gpt, SSM_STATE, SSM_GROUP_CH)
        m = jnp.einsum('ugni,gh->ugihn', p, eye)
        return m.reshape(U_TILES, LANES, gpt * SSM_STATE)

    def cmat(part):
        p = part.reshape(U_TILES, gpt, SSM_GROUP_CH, SSM_STATE)
        m = jnp.einsum('ugin,gh->ugnhi', p, eye)
        return m.reshape(U_TILES, gpt * SSM_STATE, LANES)

    bblk = jnp.concatenate([bmat(jnp.real(b_bar)), bmat(jnp.imag(b_bar))], axis=2).astype(BF16)
    cblk = jnp.concatenate([cmat(c_re.astype(F32)), cmat(-c_im.astype(F32))], axis=1).astype(BF16)
    are = jnp.real(a_bar).reshape(RE_TILES, LANES)
    aim = jnp.imag(a_bar).reshape(RE_TILES, LANES)
    return bblk, cblk, are, aim


BIAS_LANES = 8


def _split3(f):
    p1 = f.astype(BF16).astype(F32)
    r = f - p1
    p2 = r.astype(BF16).astype(F32)
    p3 = (r - p2).astype(BF16).astype(F32)
    return p1, p2, p3


def _bias_cols(f):
    n = f.shape[1]
    one = jnp.ones((1, n), F32)
    zero = jnp.zeros((2, n), F32)
    pieces = _split3(f)
    rows = []
    for key_side in (True, False):
        for h in range(2):
            ph = [p[h:h + 1, :] for p in pieces]
            rows += [one] * 3 + [-p for p in ph] if key_side else ph + [one] * 3
            rows.append(zero)
    pt = jnp.concatenate(rows, axis=0).astype(BF16)
    r = lax.broadcasted_iota(jnp.int32, (4 * BIAS_LANES, 2 * LANES), 0)
    c = lax.broadcasted_iota(jnp.int32, (4 * BIAS_LANES, 2 * LANES), 1)
    place = (c == jnp.where(r < 2 * BIAS_LANES, r, r + LANES - 2 * BIAS_LANES)).astype(BF16)
    cols = lax.dot_general(pt, place, (((0,), (0,)), ((), ())), preferred_element_type=F32)
    return cols[:, :LANES].astype(BF16), cols[:, LANES:].astype(BF16)


V_ROWS = 144
ONES_ROW = 2 * FOX_HEAD_DIM
QUERY_SLAB = 256
SKIP_EXPONENT = 152.0


def _fox_kernel(q_ref, k_ref, v_ref, frow_ref, o_ref, kaug_ref, qaug_ref, vt_ref, j0_ref,
                lhs_ref, s_ref, m_ref, acc_ref, *, t):
    i = pl.program_id(2)
    nq = pl.num_programs(2)
    seq = k_ref.shape[0]
    lane = lax.broadcasted_iota(jnp.int32, (t, LANES), 1)
    first = lane < FOX_HEAD_DIM

    def eye(rows, cols):
        return (lax.broadcasted_iota(jnp.int32, (rows, cols), 0) ==
                lax.broadcasted_iota(jnp.int32, (rows, cols), 1)).astype(BF16)

    def store_lhs(qi):
        r0 = pl.multiple_of(qi * t, t)
        q = q_ref[pl.ds(r0, t), :]
        bias = qaug_ref[pl.ds(r0, t), :]
        zero = jnp.zeros_like(q)
        fzero = jnp.zeros_like(bias)
        slot = qi & 1
        lhs_ref[slot, :t, :] = jnp.concatenate(
            [jnp.where(first, q, zero), jnp.where(lane < BIAS_LANES, bias, fzero)], axis=1)
        lhs_ref[slot, t:, :] = jnp.concatenate(
            [jnp.where(first, zero, q), jnp.where(lane < BIAS_LANES, fzero, bias)], axis=1)

    def reset_stats():
        m_ref[...] = jnp.full_like(m_ref, NEG_INF)
        acc_ref[...] = jnp.zeros_like(acc_ref)

    def head_stats():
        qk = jnp.concatenate([q_ref[...], k_ref[...]], axis=1)
        qkt = _dot_nt(eye(2 * LANES, 2 * LANES), qk)
        qt, kt = qkt[:LANES], qkt[LANES:]

        def per_head(x):
            return jnp.concatenate([jnp.sum(x[:FOX_HEAD_DIM], axis=0, keepdims=True),
                                    jnp.sum(x[FOX_HEAD_DIM:], axis=0, keepdims=True)], axis=0)
        return per_head(qt * qt), per_head(kt * kt), per_head(qt * kt)

    def first_block(qi, qn2, kmax2, diag):
        r0 = qi * t
        f = frow_ref[...]
        rows = jnp.sqrt(qn2[:, r0:r0 + t] * kmax2) + f[:, r0:r0 + t] - diag[:, r0:r0 + t]
        tau = jnp.max(rows, axis=1, keepdims=True) + SKIP_EXPONENT
        pos = lax.broadcasted_iota(jnp.int32, (1, seq), 1)
        far = (f[0:1, :] > tau[0:1, :]) & (f[1:2, :] > tau[1:2, :])
        ends = ((pos & (t - 1)) == t - 1) & (pos < r0)
        return jnp.sum((far & ends).astype(jnp.int32))

    def step(jc, jp, causal, qp=None):
        slot = (i if qp is None else qp) & 1
        if jc is not None:
            vt = vt_ref[jc]
        if jp is not None:
            k0 = pl.multiple_of(jp * t, t)
            rk = jnp.concatenate([k_ref[pl.ds(k0, t), :], kaug_ref[pl.ds(k0, t), :]], axis=1)
        for c0 in range(0, 2 * t, QUERY_SLAB):
            sl = slice(c0, c0 + QUERY_SLAB)
            if jc is not None:
                s = s_ref[:, sl]
                m_old = m_ref[:, sl]
                m_new = jnp.maximum(m_old, jnp.max(s, axis=0, keepdims=True))
                alpha = jnp.exp2(m_old - m_new)
                p = jnp.exp2(s - m_new).astype(BF16)
                acc_ref[:, sl] = alpha * acc_ref[:, sl] + _dot(vt, p)
                m_ref[:, sl] = m_new
            if jp is not None:
                s = _dot_nt(rk, lhs_ref[slot, sl, :])
                if causal is not False:
                    ahead = 0 if causal is True else causal
                    key = lax.broadcasted_iota(jnp.int32, s.shape, 0)
                    qry = (lax.broadcasted_iota(jnp.int32, s.shape, 1) + c0) & (t - 1)
                    s = jnp.where(qry + ahead >= key, s, NEG_INF)
                s_ref[:, sl] = s

    @pl.when(i == 0)
    def _():
        kaug_ref[...], qaug_ref[...] = _bias_cols(frow_ref[...])
        qn2, kn2, diag = head_stats()
        kmax2 = jnp.max(kn2, axis=1, keepdims=True)
        for qi in range(seq // t):
            j0_ref[qi] = first_block(qi, qn2, kmax2, diag)
        vt = _dot_nt(eye(V_ROWS, LANES), v_ref[...])
        row = lax.broadcasted_iota(jnp.int32, vt.shape, 0)
        vt = jnp.where(row == ONES_ROW, 1.0, vt).astype(BF16)
        for jb in range(seq // t):
            vt_ref[jb] = vt[:, jb * t:(jb + 1) * t]
        store_lhs(0)
        reset_stats()
        step(None, 0, True)

    j0 = j0_ref[i]
    n_steady = jnp.maximum(i - 1 - j0, 0)

    def steady_pair(jj, _):
        j = j0 + 2 * jj
        step(j, j + 1, False)
        step(j + 1, j + 2, False)
        return 0
    lax.fori_loop(0, n_steady >> 1, steady_pair, 0)

    def finish():
        nxt = jnp.minimum(i + 1, nq - 1)
        j0_nxt = j0_ref[nxt]
        store_lhs(nxt)
        step(i, j0_nxt, (nxt - j0_nxt) * t, nxt)

        acc = acc_ref[...]
        inv_l = 1.0 / acc[ONES_ROW:ONES_ROW + 1, :]
        out_t = jnp.concatenate([acc[:FOX_HEAD_DIM, :t] * inv_l[:, :t],
                                 acc[FOX_HEAD_DIM:ONES_ROW, t:] * inv_l[:, t:]], axis=0)
        o_ref[...] = lax.dot_general(out_t.astype(BF16), eye(LANES, LANES), (((0,), (0,)), ((), ())),
                                     preferred_element_type=F32).astype(BF16)
        reset_stats()

    odd = (n_steady & 1) == 1

    @pl.when(odd)
    def _():
        step(i - 2, i - 1, False)
        step(i - 1, i, True)
        finish()

    @pl.when((i > j0) & jnp.logical_not(odd))
    def _():
        step(i - 1, i, True)
        finish()

    @pl.when(i <= j0)
    def _():
        finish()


def _fox(q, k, v, frow, t):
    b, s, w = q.shape
    assert s % t == 0 and t & (t - 1) == 0
    kern = functools.partial(_fox_kernel, t=t)
    whole = lambda bi, p, i: (bi, 0, p)
    return pl.pallas_call(
        kern,
        grid=(b, FOX_PAIRS, s // t),
        in_specs=[
            pl.BlockSpec((None, s, LANES), whole),
            pl.BlockSpec((None, s, LANES), whole),
            pl.BlockSpec((None, s, LANES), whole),
            pl.BlockSpec((None, None, 2, s), lambda bi, p, i: (bi, p, 0, 0)),
        ],
        out_specs=pl.BlockSpec((None, t, LANES), lambda bi, p, i: (bi, i, p)),
        out_shape=jax.ShapeDtypeStruct((b, s, w), BF16),
        scratch_shapes=[
            pltpu.VMEM((s, LANES), BF16),
            pltpu.VMEM((s, LANES), BF16),
            pltpu.VMEM((s // t, V_ROWS, t), BF16),
            pltpu.SMEM((s // t,), jnp.int32),
            pltpu.VMEM((2, 2 * t, 2 * LANES), BF16),
            pltpu.VMEM((t, 2 * t), F32),
            pltpu.VMEM((1, 2 * t), F32),
            pltpu.VMEM((V_ROWS, 2 * t), F32),
        ],
        compiler_params=pltpu.CompilerParams(
            dimension_semantics=("arbitrary", "arbitrary", "arbitrary"),
            vmem_limit_bytes=VMEM_LIMIT_BYTES),
        name="fox",
    )(q, k, v, frow)


TAIL_SUB_ROWS = 256
FFN_CHUNK = 1024


def _tail_kernel(x_ref, ys_ref, yf_ref, wqk_ref, vo_ref,
                 fox_g_ref, w_out_ref, mix_post_ref, xa_pre_ref, xa_post_ref,
                 ffn_pre_ref, wg_ref, wu_ref, wd_ref, ffn_post_ref, o_ref, *, n_sub):
    sub = x_ref.shape[0] // n_sub
    rows = [slice(r * sub, (r + 1) * sub) for r in range(n_sub)]
    each = lambda f, *lists: [f(*a) for a in zip(*lists)]

    x = [x_ref[r, :] for r in rows]
    yf = [_rms(yf_ref[r, :].astype(F32), fox_g_ref[...]).astype(BF16) for r in rows]
    mix = [_dot(ys_ref[r, :], w_out_ref[:SSM_WIDTH, :]) + _dot(f, w_out_ref[SSM_WIDTH:, :])
           for r, f in zip(rows, yf)]
    x = each(lambda xi, mi: xi + _rms(mi, mix_post_ref[...]), x, mix)

    h = [_rms(xi, xa_pre_ref[...]).astype(BF16) for xi in x]
    scores = [_dot(hi, wqk_ref[...]) for hi in h]
    m_tok = wqk_ref.shape[1] // XA_HEADS
    probs = [[] for _ in rows]
    for hd in range(XA_HEADS):
        for si, out in zip(scores, probs):
            s = si[:, hd * m_tok:(hd + 1) * m_tok]
            p = jnp.exp(s - jnp.max(s, axis=-1, keepdims=True))
            out.append((p / jnp.sum(p, axis=-1, keepdims=True)).astype(BF16))
    p_all = [jnp.concatenate(ps, axis=1) for ps in probs]
    x = each(lambda xi, pi: xi + _rms(_dot(pi, vo_ref[...]), xa_post_ref[...]), x, p_all)

    h = [_rms(xi, ffn_pre_ref[...]).astype(BF16) for xi in x]
    hidden = wg_ref.shape[1]
    down = [None] * n_sub
    for c0 in range(0, hidden, FFN_CHUNK):
        c1 = min(c0 + FFN_CHUNK, hidden)
        for r, hi in enumerate(h):
            gate = _dot(hi, wg_ref[:, c0:c1])
            up = _dot(hi, wu_ref[:, c0:c1])
            act = (gate * _sigmoid(gate) * up).astype(BF16)
            part = _dot(act, wd_ref[c0:c1, :])
            down[r] = part if down[r] is None else down[r] + part
    for r, xi, di in zip(rows, x, down):
        o_ref[r, :] = xi + _rms(di, ffn_post_ref[...])


def _tail(x, ys, yf, wqk, vo, fox_g, w_out, mix_post_g, xa_pre_g, xa_post_g,
          ffn_pre_g, wg, wu, wd, ffn_post_g, tm):
    b, s, d = x.shape
    row = lambda i, j: (i, j, 0)
    per_b = lambda i, j: (i, 0, 0)
    const = lambda i, j: (0, 0)

    def resident(a):
        return pl.BlockSpec(a.shape, const, pipeline_mode=pl.Buffered(1))

    gains = [g.reshape(1, -1) for g in (fox_g, mix_post_g, xa_pre_g, xa_post_g, ffn_pre_g, ffn_post_g)]
    fox_g, mix_post_g, xa_pre_g, xa_post_g, ffn_pre_g, ffn_post_g = gains
    args = (x, ys, yf, wqk, vo, fox_g, w_out, mix_post_g, xa_pre_g, xa_post_g,
            ffn_pre_g, wg, wu, wd, ffn_post_g)
    in_specs = [
        pl.BlockSpec((None, tm, d), row),
        pl.BlockSpec((None, tm, SSM_WIDTH), row),
        pl.BlockSpec((None, tm, FOX_WIDTH), row),
        pl.BlockSpec((None,) + wqk.shape[1:], per_b),
        pl.BlockSpec((None,) + vo.shape[1:], per_b),
    ] + [resident(a) for a in args[5:]]
    return pl.pallas_call(
        functools.partial(_tail_kernel, n_sub=tm // TAIL_SUB_ROWS),
        grid=(b, s // tm),
        in_specs=in_specs,
        out_specs=pl.BlockSpec((None, tm, d), row),
        out_shape=jax.ShapeDtypeStruct((b, s, d), F32),
        compiler_params=pltpu.CompilerParams(
            dimension_semantics=("arbitrary", "arbitrary"), vmem_limit_bytes=VMEM_LIMIT_BYTES),
        name="tail",
    )(*args)


def _pick(n, pref):
    t = min(n, pref)
    assert n % t == 0, (n, t)
    return t


def kernel(x, mem, mix_pre_g, w_in, ssm_a_re, ssm_a_im, ssm_log_dt, ssm_b_re, ssm_b_im, ssm_c_re, ssm_c_im, ssm_d, ssm_glu_w, ssm_glu_b, fox_f_bias, ssm_out_g, fox_out_g, w_out, mix_post_g, xa_pre_g, mem_g, xa_wq, xa_wkv, xa_wo, xa_post_g, ffn_pre_g, w_gate, w_up, w_down, ffn_post_g):
    b, s, d = x.shape
    assert d == D_MODEL and s % LANES == 0
    n_main = SSM_WIDTH + 3 * FOX_WIDTH

    wqk, vo = _memkv(mem, mem_g, xa_wkv.astype(BF16), xa_wq.astype(BF16), xa_wo.astype(BF16))

    u, q, k, v, fcum = _inproj(
        x, mix_pre_g, w_in[:, :n_main].astype(BF16), w_in[:, n_main:].T.astype(BF16),
        fox_f_bias, tm=_pick(s, 1024))

    bblk, cblk, are, aim = _ssm_params(ssm_a_re, ssm_a_im, ssm_log_dt, ssm_b_re, ssm_b_im,
                                       ssm_c_re, ssm_c_im)
    y_ssm = _ssm(u, bblk, cblk, are, aim, ssm_d, ssm_glu_w.astype(BF16), ssm_glu_b, ssm_out_g,
                 tm=_pick(s, 128))

    y_fox = _fox(q, k, v, fcum.reshape(b, FOX_PAIRS, 2, s), t=_pick(s, 512))

    return _tail(x, y_ssm, y_fox, wqk, vo, fox_out_g, w_out.astype(BF16), mix_post_g, xa_pre_g,
                 xa_post_g, ffn_pre_g,
                 w_gate.astype(BF16), w_up.astype(BF16), w_down.astype(BF16), ffn_post_g,
                 tm=_pick(s, 2 * TAIL_SUB_ROWS))
```

```python
import functools
import math

import jax
import jax.numpy as jnp
from jax import lax
from jax.experimental import pallas as pl
from jax.experimental.pallas import tpu as pltpu

F32 = jnp.float32
BF16 = jnp.bfloat16

LANES = 128
VMEM_LIMIT_BYTES = 56 * 1024 * 1024

D_MODEL = 1024
SSM_WIDTH = 512
SSM_GROUP_CH = 16
SSM_GROUPS = 32
SSM_STATE = 64
FOX_WIDTH = 512
FOX_HEAD_DIM = 64
FOX_HEADS = 8
FOX_PAIRS = FOX_HEADS * FOX_HEAD_DIM // LANES
XA_HEADS = 4
XA_HEAD_DIM = 256
RMS_EPS = 1e-6
NEG_INF = -1e30
LOG2E = math.log2(math.e)

STATE_TILES = 2 * SSM_GROUPS * SSM_STATE // LANES
RE_TILES = STATE_TILES // 2
U_TILES = SSM_WIDTH // LANES


def _rms(x, g):
    return x * lax.rsqrt(jnp.mean(x * x, axis=-1, keepdims=True) + RMS_EPS) * g


def _dot(a, b):
    return jnp.dot(a, b, preferred_element_type=F32)


def _dot_nt(a, b):
    return lax.dot_general(a, b, (((1,), (1,)), ((), ())), preferred_element_type=F32)


def _sigmoid(z):
    return 1.0 / (1.0 + jnp.exp(-z))


def _memkv_kernel(mem_ref, g_ref, wkv_ref, wq_ref, wo_ref, wqk_ref, vo_ref):
    m = mem_ref.shape[0]
    mn = _rms(mem_ref[...], g_ref[...]).astype(BF16)
    kv = _dot(mn, wkv_ref[...])
    for hd in range(XA_HEADS):
        sl = slice(hd * XA_HEAD_DIM, (hd + 1) * XA_HEAD_DIM)
        k_h = (kv[:, sl] * (1.0 / math.sqrt(XA_HEAD_DIM))).astype(BF16)
        v_h = kv[:, D_MODEL + hd * XA_HEAD_DIM:D_MODEL + (hd + 1) * XA_HEAD_DIM].astype(BF16)
        wqk_ref[:, hd * m:(hd + 1) * m] = _dot_nt(wq_ref[:, sl], k_h).astype(BF16)
        vo_ref[hd * m:(hd + 1) * m, :] = _dot(v_h, wo_ref[sl, :]).astype(BF16)


def _memkv(mem, mem_g, wkv, wq, wo):
    b, m, d = mem.shape
    c2 = lambda i: (0, 0)
    return pl.pallas_call(
        _memkv_kernel,
        grid=(b,),
        in_specs=[
            pl.BlockSpec((None, m, d), lambda i: (i, 0, 0)),
            pl.BlockSpec((1, d), c2),
            pl.BlockSpec((d, 2 * d), c2),
            pl.BlockSpec((d, d), c2),
            pl.BlockSpec((d, d), c2),
        ],
        out_specs=[
            pl.BlockSpec((None, d, XA_HEADS * m), lambda i: (i, 0, 0)),
            pl.BlockSpec((None, XA_HEADS * m, d), lambda i: (i, 0, 0)),
        ],
        out_shape=[jax.ShapeDtypeStruct((b, d, XA_HEADS * m), BF16),
                   jax.ShapeDtypeStruct((b, XA_HEADS * m, d), BF16)],
        compiler_params=pltpu.CompilerParams(
            dimension_semantics=("arbitrary",), vmem_limit_bytes=VMEM_LIMIT_BYTES),
        name="memkv",
    )(mem, mem_g.reshape(1, d), wkv, wq, wo)


CUMSUM_SEGMENT = 512


def _inproj_kernel(x_ref, g_ref, w_ref, wf_ref, fb_ref, tri_ref,
                   u_ref, q_ref, k_ref, v_ref, f_ref, carry_ref):
    @pl.when(pl.program_id(1) == 0)
    def _():
        carry_ref[...] = jnp.zeros_like(carry_ref)

    hb = _rms(x_ref[...], g_ref[...]).astype(BF16)
    proj = _dot(hb, w_ref[...])
    u_ref[...] = proj[:, :SSM_WIDTH].astype(BF16)
    o = SSM_WIDTH
    q_ref[...] = (proj[:, o:o + FOX_WIDTH] * (LOG2E / math.sqrt(FOX_HEAD_DIM))).astype(BF16)
    k_ref[...] = proj[:, o + FOX_WIDTH:o + 2 * FOX_WIDTH].astype(BF16)
    v_ref[...] = proj[:, o + 2 * FOX_WIDTH:].astype(BF16)

    z = _dot_nt(wf_ref[...], hb) + fb_ref[...]
    logf = jnp.minimum(z, 0.0) - jnp.log1p(jnp.exp(-jnp.abs(z)))
    p1 = logf.astype(BF16)
    r1 = logf - p1.astype(F32)
    p2 = r1.astype(BF16)
    p3 = (r1 - p2.astype(F32)).astype(BF16)
    tri = tri_ref[...]
    seg = tri.shape[0]
    run = carry_ref[...][:, :1]
    parts = []
    for c0 in range(0, logf.shape[1], seg):
        sl = slice(c0, c0 + seg)
        parts.append(_dot(p1[:, sl], tri) + _dot(p2[:, sl], tri) + _dot(p3[:, sl], tri) + run)
        run = parts[-1][:, -1:]
    f = jnp.concatenate(parts, axis=1)
    f_ref[...] = f * LOG2E
    carry_ref[...] = jnp.broadcast_to(f[:, -1:], carry_ref.shape)


def _inproj(x, mix_pre_g, w_main, wf_t, f_bias, tm):
    b, s, d = x.shape
    nt = s // tm
    seg = min(tm, CUMSUM_SEGMENT)
    tri = jnp.triu(jnp.ones((seg, seg), F32)).astype(BF16)
    row = lambda i, j: (i, j, 0)
    const2 = lambda i, j: (0, 0)
    return pl.pallas_call(
        _inproj_kernel,
        grid=(b, nt),
        in_specs=[
            pl.BlockSpec((None, tm, d), row),
            pl.BlockSpec((1, d), const2),
            pl.BlockSpec(w_main.shape, const2),
            pl.BlockSpec(wf_t.shape, const2),
            pl.BlockSpec((FOX_HEADS, 1), const2),
            pl.BlockSpec((seg, seg), const2),
        ],
        out_specs=[
            pl.BlockSpec((None, tm, SSM_WIDTH), row),
            pl.BlockSpec((None, tm, FOX_WIDTH), row),
            pl.BlockSpec((None, tm, FOX_WIDTH), row),
            pl.BlockSpec((None, tm, FOX_WIDTH), row),
            pl.BlockSpec((None, FOX_HEADS, tm), lambda i, j: (i, 0, j)),
        ],
        out_shape=[
            jax.ShapeDtypeStruct((b, s, SSM_WIDTH), BF16),
            jax.ShapeDtypeStruct((b, s, FOX_WIDTH), BF16),
            jax.ShapeDtypeStruct((b, s, FOX_WIDTH), BF16),
            jax.ShapeDtypeStruct((b, s, FOX_WIDTH), BF16),
            jax.ShapeDtypeStruct((b, FOX_HEADS, s), F32),
        ],
        scratch_shapes=[pltpu.VMEM((FOX_HEADS, LANES), F32)],
        compiler_params=pltpu.CompilerParams(
            dimension_semantics=("arbitrary", "arbitrary"), vmem_limit_bytes=VMEM_LIMIT_BYTES),
        name="inproj",
    )(x, mix_pre_g.reshape(1, d), w_main, wf_t, f_bias.reshape(FOX_HEADS, 1), tri)


def _gelu_tanh(x):
    c = math.sqrt(2.0 / math.pi)
    return 0.5 * x * (1.0 + jnp.tanh(c * (x + 0.044715 * (x * x * x))))


TILE_PITCH = 20
TIME_PITCH = 324
SCAN_UNROLL = 8


def _ssm_kernel(u_ref, bblk_ref, cblk_ref, are_ref, aim_ref, d_ref, gw_ref, gb_ref, g_ref,
                o_ref, s_ref, xc_ref, *, nb, tm):
    @pl.when(pl.program_id(0) == 0)
    def _():
        xc_ref[...] = jnp.zeros_like(xc_ref)

    def rows_of_tile(b, part, k):
        return pl.ds(TILE_PITCH * k + 2 * b + part, tm, stride=TIME_PITCH)

    for ut in range(U_TILES):
        ub = u_ref[:, :, ut * LANES:(ut + 1) * LANES].reshape(nb * tm, LANES)
        res = _dot(ub, bblk_ref[ut])
        for b in range(nb):
            for j in range(2 * U_TILES):
                part, k = divmod(j, U_TILES)
                s_ref[rows_of_tile(b, part, U_TILES * ut + k), :] = (
                    res[b * tm:(b + 1) * tm, j * LANES:(j + 1) * LANES])

    ar = are_ref[...]
    ai = aim_ref[...]

    def step(t, carry):
        new = []
        for b in range(nb):
            xr, xi = carry[2 * b], carry[2 * b + 1]
            ire = pl.ds(TIME_PITCH * t + 2 * b, RE_TILES, stride=TILE_PITCH)
            iim = pl.ds(TIME_PITCH * t + 2 * b + 1, RE_TILES, stride=TILE_PITCH)
            nr = ar * xr - ai * xi + s_ref[ire, :]
            ni = ar * xi + ai * xr + s_ref[iim, :]
            s_ref[ire, :] = nr
            s_ref[iim, :] = ni
            new += [nr, ni]
        return tuple(new)

    init = []
    for b in range(nb):
        init.append(xc_ref[b * STATE_TILES:b * STATE_TILES + RE_TILES, :])
        init.append(xc_ref[b * STATE_TILES + RE_TILES:(b + 1) * STATE_TILES, :])
    fin = lax.fori_loop(0, tm, step, tuple(init), unroll=SCAN_UNROLL)
    for b in range(nb):
        xc_ref[b * STATE_TILES:b * STATE_TILES + RE_TILES, :] = fin[2 * b]
        xc_ref[b * STATE_TILES + RE_TILES:(b + 1) * STATE_TILES, :] = fin[2 * b + 1]

    ys = []
    for ot in range(U_TILES):
        per_seq = []
        for b in range(nb):
            tiles = [s_ref[rows_of_tile(b, part, U_TILES * ot + k), :].astype(BF16)
                     for part in range(2) for k in range(U_TILES)]
            per_seq.append(jnp.concatenate(tiles, axis=1))
        ys.append(_dot(jnp.concatenate(per_seq, axis=0), cblk_ref[ot]))
    u_all = u_ref[...].reshape(nb * tm, SSM_WIDTH).astype(F32)
    y = jnp.concatenate(ys, axis=1) + d_ref[...] * u_all
    g = _gelu_tanh(y)
    out = g * _sigmoid(_dot(g.astype(BF16), gw_ref[...]) + gb_ref[...])
    o_ref[...] = _rms(out, g_ref[...]).astype(BF16).reshape(nb, tm, SSM_WIDTH)


def _ssm(u, bblk, cblk, a_re, a_im, d_skip, glu_w, glu_b, out_g, tm):
    b, s, w = u.shape
    assert 2 * b <= TILE_PITCH - 4 and TIME_PITCH >= TILE_PITCH * (RE_TILES - 1) + 2 * b
    kern = functools.partial(_ssm_kernel, nb=b, tm=tm)
    c2 = lambda t: (0, 0)
    c3 = lambda t: (0, 0, 0)
    return pl.pallas_call(
        kern,
        grid=(s // tm,),
        in_specs=[
            pl.BlockSpec((b, tm, w), lambda t: (0, t, 0)),
            pl.BlockSpec(bblk.shape, c3),
            pl.BlockSpec(cblk.shape, c3),
            pl.BlockSpec(a_re.shape, c2),
            pl.BlockSpec(a_im.shape, c2),
            pl.BlockSpec((1, w), c2),
            pl.BlockSpec(glu_w.shape, c2),
            pl.BlockSpec((1, w), c2),
            pl.BlockSpec((1, w), c2),
        ],
        out_specs=pl.BlockSpec((b, tm, w), lambda t: (0, t, 0)),
        out_shape=jax.ShapeDtypeStruct((b, s, w), BF16),
        scratch_shapes=[
            pltpu.VMEM((tm * TIME_PITCH, LANES), F32),
            pltpu.VMEM((b * STATE_TILES, LANES), F32),
        ],
        compiler_params=pltpu.CompilerParams(
            dimension_semantics=("arbitrary",), vmem_limit_bytes=VMEM_LIMIT_BYTES),
        name="ssm",
    )(u, bblk, cblk, a_re, a_im, d_skip.reshape(1, w), glu_w, glu_b.reshape(1, w),
      out_g.reshape(1, w))


def _ssm_params(a_re, a_im, log_dt, b_re, b_im, c_re, c_im):
    ar, ai = a_re.astype(F32), a_im.astype(F32)
    dt = jnp.exp(log_dt.astype(F32))[:, None]
    mag = jnp.exp(ar * dt)
    abr, abi = mag * jnp.cos(ai * dt), mag * jnp.sin(ai * dt)
    den = ar * ar + ai * ai
    cr = (((abr - 1.0) * ar + abi * ai) / den)[..., None]
    ci = ((abi * ar - (abr - 1.0) * ai) / den)[..., None]
    br, bi = b_re.astype(F32), b_im.astype(F32)
    bbr, bbi = cr * br - ci * bi, cr * bi + ci * br
    gpt = LANES // SSM_GROUP_CH
    eye = jnp.eye(gpt, dtype=F32)[None, :, None, :, None]

    def bmat(part):
        p = jnp.swapaxes(part.reshape(U_TILES, gpt, SSM_STATE, SSM_GROUP_CH), 2, 3)
        return (p[:, :, :, None, :] * eye).reshape(U_TILES, LANES, gpt * SSM_STATE)

    def cmat(part):
        p = jnp.swapaxes(part.reshape(U_TILES, gpt, SSM_GROUP_CH, SSM_STATE), 2, 3)
        return (p[:, :, :, None, :] * eye).reshape(U_TILES, gpt * SSM_STATE, LANES)

    bblk = jnp.concatenate([bmat(bbr), bmat(bbi)], axis=2).astype(BF16)
    cblk = jnp.concatenate([cmat(c_re.astype(F32)), cmat(-c_im.astype(F32))], axis=1).astype(BF16)
    return bblk, cblk, abr.reshape(RE_TILES, LANES), abi.reshape(RE_TILES, LANES)


BIAS_LANES = 8


def _split3(f):
    p1 = f.astype(BF16).astype(F32)
    r = f - p1
    p2 = r.astype(BF16).astype(F32)
    p3 = (r - p2).astype(BF16).astype(F32)
    return p1, p2, p3


def _bias_cols(f):
    n = f.shape[1]
    one = jnp.ones((1, n), F32)
    zero = jnp.zeros((2, n), F32)
    pieces = _split3(f)
    rows = []
    for key_side in (True, False):
        for h in range(2):
            ph = [p[h:h + 1, :] for p in pieces]
            rows += [one] * 3 + [-p for p in ph] if key_side else ph + [one] * 3
            rows.append(zero)
    pt = jnp.concatenate(rows, axis=0).astype(BF16)
    r = lax.broadcasted_iota(jnp.int32, (4 * BIAS_LANES, 2 * LANES), 0)
    c = lax.broadcasted_iota(jnp.int32, (4 * BIAS_LANES, 2 * LANES), 1)
    place = (c == jnp.where(r < 2 * BIAS_LANES, r, r + LANES - 2 * BIAS_LANES)).astype(BF16)
    cols = lax.dot_general(pt, place, (((0,), (0,)), ((), ())), preferred_element_type=F32)
    return cols[:, :LANES].astype(BF16), cols[:, LANES:].astype(BF16)


V_ROWS = 144
ONES_ROW = 2 * FOX_HEAD_DIM
QUERY_SLAB = 256
SKIP_EXPONENT = 152.0


def _fox_kernel(q_ref, k_ref, v_ref, frow_ref, o_ref, kaug_ref, qaug_ref, vt_ref, j0_ref,
                lhs_ref, s_ref, m_ref, acc_ref, *, t):
    i = pl.program_id(2)
    nq = pl.num_programs(2)
    seq = k_ref.shape[0]
    lane = lax.broadcasted_iota(jnp.int32, (t, LANES), 1)
    first = lane < FOX_HEAD_DIM

    def eye(rows, cols):
        return (lax.broadcasted_iota(jnp.int32, (rows, cols), 0) ==
                lax.broadcasted_iota(jnp.int32, (rows, cols), 1)).astype(BF16)

    def store_lhs(qi):
        r0 = pl.multiple_of(qi * t, t)
        q = q_ref[pl.ds(r0, t), :]
        bias = qaug_ref[pl.ds(r0, t), :]
        zero = jnp.zeros_like(q)
        fzero = jnp.zeros_like(bias)
        slot = qi & 1
        lhs_ref[slot, :t, :] = jnp.concatenate(
            [jnp.where(first, q, zero), jnp.where(lane < BIAS_LANES, bias, fzero)], axis=1)
        lhs_ref[slot, t:, :] = jnp.concatenate(
            [jnp.where(first, zero, q), jnp.where(lane < BIAS_LANES, fzero, bias)], axis=1)

    def reset_stats():
        m_ref[...] = jnp.full_like(m_ref, NEG_INF)
        acc_ref[...] = jnp.zeros_like(acc_ref)

    def head_stats():
        qk = jnp.concatenate([q_ref[...], k_ref[...]], axis=1)
        qkt = _dot_nt(eye(2 * LANES, 2 * LANES), qk)
        qt, kt = qkt[:LANES], qkt[LANES:]

        def per_head(x):
            return jnp.concatenate([jnp.sum(x[:FOX_HEAD_DIM], axis=0, keepdims=True),
                                    jnp.sum(x[FOX_HEAD_DIM:], axis=0, keepdims=True)], axis=0)
        return per_head(qt * qt), per_head(kt * kt), per_head(qt * kt)

    def first_block(qi, qn2, kmax2, diag):
        r0 = qi * t
        f = frow_ref[...]
        rows = jnp.sqrt(qn2[:, r0:r0 + t] * kmax2) + f[:, r0:r0 + t] - diag[:, r0:r0 + t]
        tau = jnp.max(rows, axis=1, keepdims=True) + SKIP_EXPONENT
        pos = lax.broadcasted_iota(jnp.int32, (1, seq), 1)
        far = (f[0:1, :] > tau[0:1, :]) & (f[1:2, :] > tau[1:2, :])
        ends = ((pos & (t - 1)) == t - 1) & (pos < r0)
        return jnp.sum((far & ends).astype(jnp.int32))

    def step(jc, jp, causal, qp=None):
        slot = (i if qp is None else qp) & 1
        if jc is not None:
            vt = vt_ref[jc]
        if jp is not None:
            k0 = pl.multiple_of(jp * t, t)
            rk = jnp.concatenate([k_ref[pl.ds(k0, t), :], kaug_ref[pl.ds(k0, t), :]], axis=1)
        for c0 in range(0, 2 * t, QUERY_SLAB):
            sl = slice(c0, c0 + QUERY_SLAB)
            if jc is not None:
                s = s_ref[:, sl]
                m_old = m_ref[:, sl]
                m_new = jnp.maximum(m_old, jnp.max(s, axis=0, keepdims=True))
                alpha = jnp.exp2(m_old - m_new)
                p = jnp.exp2(s - m_new).astype(BF16)
                acc_ref[:, sl] = alpha * acc_ref[:, sl] + _dot(vt, p)
                m_ref[:, sl] = m_new
            if jp is not None:
                s = _dot_nt(rk, lhs_ref[slot, sl, :])
                if causal is not False:
                    ahead = 0 if causal is True else causal
                    key = lax.broadcasted_iota(jnp.int32, s.shape, 0)
                    qry = (lax.broadcasted_iota(jnp.int32, s.shape, 1) + c0) & (t - 1)
                    s = jnp.where(qry + ahead >= key, s, NEG_INF)
                s_ref[:, sl] = s

    @pl.when(i == 0)
    def _():
        kaug_ref[...], qaug_ref[...] = _bias_cols(frow_ref[...])
        qn2, kn2, diag = head_stats()
        kmax2 = jnp.max(kn2, axis=1, keepdims=True)
        for qi in range(seq // t):
            j0_ref[qi] = first_block(qi, qn2, kmax2, diag)
        vt = _dot_nt(eye(V_ROWS, LANES), v_ref[...])
        row = lax.broadcasted_iota(jnp.int32, vt.shape, 0)
        vt = jnp.where(row == ONES_ROW, 1.0, vt).astype(BF16)
        for jb in range(seq // t):
            vt_ref[jb] = vt[:, jb * t:(jb + 1) * t]
        store_lhs(0)
        reset_stats()
        step(None, 0, True)

    j0 = j0_ref[i]
    n_steady = jnp.maximum(i - 1 - j0, 0)

    def steady_pair(jj, _):
        j = j0 + 2 * jj
        step(j, j + 1, False)
        step(j + 1, j + 2, False)
        return 0
    lax.fori_loop(0, n_steady >> 1, steady_pair, 0)

    def finish():
        nxt = jnp.minimum(i + 1, nq - 1)
        j0_nxt = j0_ref[nxt]
        store_lhs(nxt)
        step(i, j0_nxt, (nxt - j0_nxt) * t, nxt)

        acc = acc_ref[...]
        inv_l = 1.0 / acc[ONES_ROW:ONES_ROW + 1, :]
        out_t = jnp.concatenate([acc[:FOX_HEAD_DIM, :t] * inv_l[:, :t],
                                 acc[FOX_HEAD_DIM:ONES_ROW, t:] * inv_l[:, t:]], axis=0)
        o_ref[...] = lax.dot_general(out_t.astype(BF16), eye(LANES, LANES), (((0,), (0,)), ((), ())),
                                     preferred_element_type=F32).astype(BF16)
        reset_stats()

    odd = (n_steady & 1) == 1

    @pl.when(odd)
    def _():
        step(i - 2, i - 1, False)
        step(i - 1, i, True)
        finish()

    @pl.when((i > j0) & jnp.logical_not(odd))
    def _():
        step(i - 1, i, True)
        finish()

    @pl.when(i <= j0)
    def _():
        finish()


def _fox(q, k, v, frow, t):
    b, s, w = q.shape
    assert s % t == 0 and t & (t - 1) == 0
    kern = functools.partial(_fox_kernel, t=t)
    whole = lambda bi, p, i: (bi, 0, p)
    return pl.pallas_call(
        kern,
        grid=(b, FOX_PAIRS, s // t),
        in_specs=[
            pl.BlockSpec((None, s, LANES), whole),
            pl.BlockSpec((None, s, LANES), whole),
            pl.BlockSpec((None, s, LANES), whole),
            pl.BlockSpec((None, None, 2, s), lambda bi, p, i: (bi, p, 0, 0)),
        ],
        out_specs=pl.BlockSpec((None, t, LANES), lambda bi, p, i: (bi, i, p)),
        out_shape=jax.ShapeDtypeStruct((b, s, w), BF16),
        scratch_shapes=[
            pltpu.VMEM((s, LANES), BF16),
            pltpu.VMEM((s, LANES), BF16),
            pltpu.VMEM((s // t, V_ROWS, t), BF16),
            pltpu.SMEM((s // t,), jnp.int32),
            pltpu.VMEM((2, 2 * t, 2 * LANES), BF16),
            pltpu.VMEM((t, 2 * t), F32),
            pltpu.VMEM((1, 2 * t), F32),
            pltpu.VMEM((V_ROWS, 2 * t), F32),
        ],
        compiler_params=pltpu.CompilerParams(
            dimension_semantics=("arbitrary", "arbitrary", "arbitrary"),
            vmem_limit_bytes=VMEM_LIMIT_BYTES),
        name="fox",
    )(q, k, v, frow)


TAIL_SUB_ROWS = 256
FFN_CHUNK = 1024


def _tail_kernel(x_ref, ys_ref, yf_ref, wqk_ref, vo_ref,
                 fox_g_ref, w_out_ref, mix_post_ref, xa_pre_ref, xa_post_ref,
                 ffn_pre_ref, wg_ref, wu_ref, wd_ref, ffn_post_ref, o_ref, *, n_sub):
    sub = x_ref.shape[0] // n_sub
    rows = [slice(r * sub, (r + 1) * sub) for r in range(n_sub)]
    each = lambda f, *lists: [f(*a) for a in zip(*lists)]

    x = [x_ref[r, :] for r in rows]
    yf = [_rms(yf_ref[r, :].astype(F32), fox_g_ref[...]).astype(BF16) for r in rows]
    mix = [_dot(ys_ref[r, :], w_out_ref[:SSM_WIDTH, :]) + _dot(f, w_out_ref[SSM_WIDTH:, :])
           for r, f in zip(rows, yf)]
    x = each(lambda xi, mi: xi + _rms(mi, mix_post_ref[...]), x, mix)

    h = [_rms(xi, xa_pre_ref[...]).astype(BF16) for xi in x]
    scores = [_dot(hi, wqk_ref[...]) for hi in h]
    m_tok = wqk_ref.shape[1] // XA_HEADS
    probs = [[] for _ in rows]
    for hd in range(XA_HEADS):
        for si, out in zip(scores, probs):
            s = si[:, hd * m_tok:(hd + 1) * m_tok]
            p = jnp.exp(s - jnp.max(s, axis=-1, keepdims=True))
            out.append((p / jnp.sum(p, axis=-1, keepdims=True)).astype(BF16))
    p_all = [jnp.concatenate(ps, axis=1) for ps in probs]
    x = each(lambda xi, pi: xi + _rms(_dot(pi, vo_ref[...]), xa_post_ref[...]), x, p_all)

    h = [_rms(xi, ffn_pre_ref[...]).astype(BF16) for xi in x]
    hidden = wg_ref.shape[1]
    down = [None] * n_sub
    for c0 in range(0, hidden, FFN_CHUNK):
        c1 = min(c0 + FFN_CHUNK, hidden)
        for r, hi in enumerate(h):
            gate = _dot(hi, wg_ref[:, c0:c1])
            up = _dot(hi, wu_ref[:, c0:c1])
            act = (gate * _sigmoid(gate) * up).astype(BF16)
            part = _dot(act, wd_ref[c0:c1, :])
            down[r] = part if down[r] is None else down[r] + part
    for r, xi, di in zip(rows, x, down):
        o_ref[r, :] = xi + _rms(di, ffn_post_ref[...])


def _tail(x, ys, yf, wqk, vo, fox_g, w_out, mix_post_g, xa_pre_g, xa_post_g,
          ffn_pre_g, wg, wu, wd, ffn_post_g, tm):
    b, s, d = x.shape
    row = lambda i, j: (i, j, 0)
    per_b = lambda i, j: (i, 0, 0)
    const = lambda i, j: (0, 0)

    def resident(a):
        return pl.BlockSpec(a.shape, const, pipeline_mode=pl.Buffered(1))

    gains = [g.reshape(1, -1) for g in (fox_g, mix_post_g, xa_pre_g, xa_post_g, ffn_pre_g, ffn_post_g)]
    fox_g, mix_post_g, xa_pre_g, xa_post_g, ffn_pre_g, ffn_post_g = gains
    args = (x, ys, yf, wqk, vo, fox_g, w_out, mix_post_g, xa_pre_g, xa_post_g,
            ffn_pre_g, wg, wu, wd, ffn_post_g)
    in_specs = [
        pl.BlockSpec((None, tm, d), row),
        pl.BlockSpec((None, tm, SSM_WIDTH), row),
        pl.BlockSpec((None, tm, FOX_WIDTH), row),
        pl.BlockSpec((None,) + wqk.shape[1:], per_b),
        pl.BlockSpec((None,) + vo.shape[1:], per_b),
    ] + [resident(a) for a in args[5:]]
    return pl.pallas_call(
        functools.partial(_tail_kernel, n_sub=tm // TAIL_SUB_ROWS),
        grid=(b, s // tm),
        in_specs=in_specs,
        out_specs=pl.BlockSpec((None, tm, d), row),
        out_shape=jax.ShapeDtypeStruct((b, s, d), F32),
        compiler_params=pltpu.CompilerParams(
            dimension_semantics=("arbitrary", "arbitrary"), vmem_limit_bytes=VMEM_LIMIT_BYTES),
        name="tail",
    )(*args)


def _pick(n, pref):
    t = min(n, pref)
    assert n % t == 0, (n, t)
    return t


def kernel(x, mem, mix_pre_g, w_in, ssm_a_re, ssm_a_im, ssm_log_dt, ssm_b_re, ssm_b_im, ssm_c_re, ssm_c_im, ssm_d, ssm_glu_w, ssm_glu_b, fox_f_bias, ssm_out_g, fox_out_g, w_out, mix_post_g, xa_pre_g, mem_g, xa_wq, xa_wkv, xa_wo, xa_post_g, ffn_pre_g, w_gate, w_up, w_down, ffn_post_g):
    b, s, d = x.shape
    assert d == D_MODEL and s % LANES == 0
    n_main = SSM_WIDTH + 3 * FOX_WIDTH

    wqk, vo = _memkv(mem, mem_g, xa_wkv.astype(BF16), xa_wq.astype(BF16), xa_wo.astype(BF16))

    u, q, k, v, fcum = _inproj(
        x, mix_pre_g, w_in[:, :n_main].astype(BF16), w_in[:, n_main:].T.astype(BF16),
        fox_f_bias, tm=_pick(s, 1024))

    bblk, cblk, are, aim = _ssm_params(ssm_a_re, ssm_a_im, ssm_log_dt, ssm_b_re, ssm_b_im,
                                       ssm_c_re, ssm_c_im)
    y_ssm = _ssm(u, bblk, cblk, are, aim, ssm_d, ssm_glu_w.astype(BF16), ssm_glu_b, ssm_out_g,
                 tm=_pick(s, 128))

    y_fox = _fox(q, k, v, fcum.reshape(b, FOX_PAIRS, 2, s), t=_pick(s, 512))

    return _tail(x, y_ssm, y_fox, wqk, vo, fox_out_g, w_out.astype(BF16), mix_post_g, xa_pre_g,
                 xa_post_g, ffn_pre_g,
                 w_gate.astype(BF16), w_up.astype(BF16), w_down.astype(BF16), ffn_post_g,
                 tm=_pick(s, 2 * TAIL_SUB_ROWS))
```

```python
import functools
import math

import jax
import jax.numpy as jnp
from jax import lax
from jax.experimental import pallas as pl
from jax.experimental.pallas import tpu as pltpu

F32 = jnp.float32
BF16 = jnp.bfloat16

LANES = 128
VMEM_LIMIT_BYTES = 56 * 1024 * 1024

D_MODEL = 1024
SSM_WIDTH = 512
SSM_GROUP_CH = 16
SSM_GROUPS = 32
SSM_STATE = 64
FOX_WIDTH = 512
FOX_HEAD_DIM = 64
FOX_HEADS = 8
FOX_PAIRS = FOX_HEADS * FOX_HEAD_DIM // LANES
XA_HEADS = 4
XA_HEAD_DIM = 256
RMS_EPS = 1e-6
NEG_INF = -1e30
LOG2E = math.log2(math.e)

STATE_TILES = 2 * SSM_GROUPS * SSM_STATE // LANES
RE_TILES = STATE_TILES // 2
U_TILES = SSM_WIDTH // LANES


def _rms(x, g):
    return x * lax.rsqrt(jnp.mean(x * x, axis=-1, keepdims=True) + RMS_EPS) * g


def _dot(a, b):
    return jnp.dot(a, b, preferred_element_type=F32)


def _dot_nt(a, b):
    return lax.dot_general(a, b, (((1,), (1,)), ((), ())), preferred_element_type=F32)


def _sigmoid(z):
    return 1.0 / (1.0 + jnp.exp(-z))


def _memkv_kernel(mem_ref, g_ref, wkv_ref, wq_ref, wo_ref, wqk_ref, vo_ref):
    m = mem_ref.shape[0]
    mn = _rms(mem_ref[...], g_ref[...]).astype(BF16)
    kv = _dot(mn, wkv_ref[...])
    for hd in range(XA_HEADS):
        sl = slice(hd * XA_HEAD_DIM, (hd + 1) * XA_HEAD_DIM)
        k_h = (kv[:, sl] * (1.0 / math.sqrt(XA_HEAD_DIM))).astype(BF16)
        v_h = kv[:, D_MODEL + hd * XA_HEAD_DIM:D_MODEL + (hd + 1) * XA_HEAD_DIM].astype(BF16)
        wqk_ref[:, hd * m:(hd + 1) * m] = _dot_nt(wq_ref[:, sl], k_h).astype(BF16)
        vo_ref[hd * m:(hd + 1) * m, :] = _dot(v_h, wo_ref[sl, :]).astype(BF16)


def _memkv(mem, mem_g, wkv, wq, wo):
    b, m, d = mem.shape
    c2 = lambda i: (0, 0)
    return pl.pallas_call(
        _memkv_kernel,
        grid=(b,),
        in_specs=[
            pl.BlockSpec((None, m, d), lambda i: (i, 0, 0)),
            pl.BlockSpec((1, d), c2),
            pl.BlockSpec((d, 2 * d), c2),
            pl.BlockSpec((d, d), c2),
            pl.BlockSpec((d, d), c2),
        ],
        out_specs=[
            pl.BlockSpec((None, d, XA_HEADS * m), lambda i: (i, 0, 0)),
            pl.BlockSpec((None, XA_HEADS * m, d), lambda i: (i, 0, 0)),
        ],
        out_shape=[jax.ShapeDtypeStruct((b, d, XA_HEADS * m), BF16),
                   jax.ShapeDtypeStruct((b, XA_HEADS * m, d), BF16)],
        compiler_params=pltpu.CompilerParams(
            dimension_semantics=("arbitrary",), vmem_limit_bytes=VMEM_LIMIT_BYTES),
        name="memkv",
    )(mem, mem_g.reshape(1, d), wkv, wq, wo)


CUMSUM_SEGMENT = 512


def _inproj_kernel(x_ref, g_ref, w_ref, wf_ref, fb_ref, tri_ref,
                   u_ref, q_ref, k_ref, v_ref, f_ref, carry_ref):
    @pl.when(pl.program_id(1) == 0)
    def _():
        carry_ref[...] = jnp.zeros_like(carry_ref)

    hb = _rms(x_ref[...], g_ref[...]).astype(BF16)
    proj = _dot(hb, w_ref[...])
    u_ref[...] = proj[:, :SSM_WIDTH].astype(BF16)
    o = SSM_WIDTH
    q_ref[...] = (proj[:, o:o + FOX_WIDTH] * (LOG2E / math.sqrt(FOX_HEAD_DIM))).astype(BF16)
    k_ref[...] = proj[:, o + FOX_WIDTH:o + 2 * FOX_WIDTH].astype(BF16)
    v_ref[...] = proj[:, o + 2 * FOX_WIDTH:].astype(BF16)

    z = _dot_nt(wf_ref[...], hb) + fb_ref[...]
    logf = jnp.minimum(z, 0.0) - jnp.log1p(jnp.exp(-jnp.abs(z)))
    p1 = logf.astype(BF16)
    r1 = logf - p1.astype(F32)
    p2 = r1.astype(BF16)
    p3 = (r1 - p2.astype(F32)).astype(BF16)
    tri = tri_ref[...]
    seg = tri.shape[0]
    run = carry_ref[...][:, :1]
    parts = []
    for c0 in range(0, logf.shape[1], seg):
        sl = slice(c0, c0 + seg)
        parts.append(_dot(p1[:, sl], tri) + _dot(p2[:, sl], tri) + _dot(p3[:, sl], tri) + run)
        run = parts[-1][:, -1:]
    f = jnp.concatenate(parts, axis=1)
    f_ref[...] = f * LOG2E
    carry_ref[...] = jnp.broadcast_to(f[:, -1:], carry_ref.shape)


def _inproj(x, mix_pre_g, w_main, wf_t, f_bias, tm):
    b, s, d = x.shape
    nt = s // tm
    seg = min(tm, CUMSUM_SEGMENT)
    tri = jnp.triu(jnp.ones((seg, seg), F32)).astype(BF16)
    row = lambda i, j: (i, j, 0)
    const2 = lambda i, j: (0, 0)
    return pl.pallas_call(
        _inproj_kernel,
        grid=(b, nt),
        in_specs=[
            pl.BlockSpec((None, tm, d), row),
            pl.BlockSpec((1, d), const2),
            pl.BlockSpec(w_main.shape, const2),
            pl.BlockSpec(wf_t.shape, const2),
            pl.BlockSpec((FOX_HEADS, 1), const2),
            pl.BlockSpec((seg, seg), const2),
        ],
        out_specs=[
            pl.BlockSpec((None, tm, SSM_WIDTH), row),
            pl.BlockSpec((None, tm, FOX_WIDTH), row),
            pl.BlockSpec((None, tm, FOX_WIDTH), row),
            pl.BlockSpec((None, tm, FOX_WIDTH), row),
            pl.BlockSpec((None, FOX_HEADS, tm), lambda i, j: (i, 0, j)),
        ],
        out_shape=[
            jax.ShapeDtypeStruct((b, s, SSM_WIDTH), BF16),
            jax.ShapeDtypeStruct((b, s, FOX_WIDTH), BF16),
            jax.ShapeDtypeStruct((b, s, FOX_WIDTH), BF16),
            jax.ShapeDtypeStruct((b, s, FOX_WIDTH), BF16),
            jax.ShapeDtypeStruct((b, FOX_HEADS, s), F32),
        ],
        scratch_shapes=[pltpu.VMEM((FOX_HEADS, LANES), F32)],
        compiler_params=pltpu.CompilerParams(
            dimension_semantics=("arbitrary", "arbitrary"), vmem_limit_bytes=VMEM_LIMIT_BYTES),
        name="inproj",
    )(x, mix_pre_g.reshape(1, d), w_main, wf_t, f_bias.reshape(FOX_HEADS, 1), tri)


def _gelu_tanh(x):
    c = math.sqrt(2.0 / math.pi)
    return 0.5 * x * (1.0 + jnp.tanh(c * (x + 0.044715 * (x * x * x))))


TILE_PITCH = 20
TIME_PITCH = 324
SCAN_UNROLL = 8


def _ssm_kernel(u_ref, bblk_ref, cblk_ref, are_ref, aim_ref, d_ref, gw_ref, gb_ref, g_ref,
                o_ref, s_ref, xc_ref, *, nb, tm):
    @pl.when(pl.program_id(0) == 0)
    def _():
        xc_ref[...] = jnp.zeros_like(xc_ref)

    def rows_of_tile(b, part, k):
        return pl.ds(TILE_PITCH * k + 2 * b + part, tm, stride=TIME_PITCH)

    for ut in range(U_TILES):
        ub = u_ref[:, :, ut * LANES:(ut + 1) * LANES].reshape(nb * tm, LANES)
        res = _dot(ub, bblk_ref[ut])
        for b in range(nb):
            for j in range(2 * U_TILES):
                part, k = divmod(j, U_TILES)
                s_ref[rows_of_tile(b, part, U_TILES * ut + k), :] = (
                    res[b * tm:(b + 1) * tm, j * LANES:(j + 1) * LANES])

    ar = are_ref[...]
    ai = aim_ref[...]

    def step(t, carry):
        new = []
        for b in range(nb):
            xr, xi = carry[2 * b], carry[2 * b + 1]
            ire = pl.ds(TIME_PITCH * t + 2 * b, RE_TILES, stride=TILE_PITCH)
            iim = pl.ds(TIME_PITCH * t + 2 * b + 1, RE_TILES, stride=TILE_PITCH)
            nr = ar * xr - ai * xi + s_ref[ire, :]
            ni = ar * xi + ai * xr + s_ref[iim, :]
            s_ref[ire, :] = nr
            s_ref[iim, :] = ni
            new += [nr, ni]
        return tuple(new)

    init = []
    for b in range(nb):
        init.append(xc_ref[b * STATE_TILES:b * STATE_TILES + RE_TILES, :])
        init.append(xc_ref[b * STATE_TILES + RE_TILES:(b + 1) * STATE_TILES, :])
    fin = lax.fori_loop(0, tm, step, tuple(init), unroll=SCAN_UNROLL)
    for b in range(nb):
        xc_ref[b * STATE_TILES:b * STATE_TILES + RE_TILES, :] = fin[2 * b]
        xc_ref[b * STATE_TILES + RE_TILES:(b + 1) * STATE_TILES, :] = fin[2 * b + 1]

    ys = []
    for ot in range(U_TILES):
        per_seq = []
        for b in range(nb):
            tiles = [s_ref[rows_of_tile(b, part, U_TILES * ot + k), :].astype(BF16)
                     for part in range(2) for k in range(U_TILES)]
            per_seq.append(jnp.concatenate(tiles, axis=1))
        ys.append(_dot(jnp.concatenate(per_seq, axis=0), cblk_ref[ot]))
    u_all = u_ref[...].reshape(nb * tm, SSM_WIDTH).astype(F32)
    y = jnp.concatenate(ys, axis=1) + d_ref[...] * u_all
    g = _gelu_tanh(y)
    out = g * _sigmoid(_dot(g.astype(BF16), gw_ref[...]) + gb_ref[...])
    o_ref[...] = _rms(out, g_ref[...]).astype(BF16).reshape(nb, tm, SSM_WIDTH)


def _ssm(u, bblk, cblk, a_re, a_im, d_skip, glu_w, glu_b, out_g, tm):
    b, s, w = u.shape
    assert 2 * b <= TILE_PITCH - 4 and TIME_PITCH >= TILE_PITCH * (RE_TILES - 1) + 2 * b
    kern = functools.partial(_ssm_kernel, nb=b, tm=tm)
    c2 = lambda t: (0, 0)
    c3 = lambda t: (0, 0, 0)
    return pl.pallas_call(
        kern,
        grid=(s // tm,),
        in_specs=[
            pl.BlockSpec((b, tm, w), lambda t: (0, t, 0)),
            pl.BlockSpec(bblk.shape, c3),
            pl.BlockSpec(cblk.shape, c3),
            pl.BlockSpec(a_re.shape, c2),
            pl.BlockSpec(a_im.shape, c2),
            pl.BlockSpec((1, w), c2),
            pl.BlockSpec(glu_w.shape, c2),
            pl.BlockSpec((1, w), c2),
            pl.BlockSpec((1, w), c2),
        ],
        out_specs=pl.BlockSpec((b, tm, w), lambda t: (0, t, 0)),
        out_shape=jax.ShapeDtypeStruct((b, s, w), BF16),
        scratch_shapes=[
            pltpu.VMEM((tm * TIME_PITCH, LANES), F32),
            pltpu.VMEM((b * STATE_TILES, LANES), F32),
        ],
        compiler_params=pltpu.CompilerParams(
            dimension_semantics=("arbitrary",), vmem_limit_bytes=VMEM_LIMIT_BYTES),
        name="ssm",
    )(u, bblk, cblk, a_re, a_im, d_skip.reshape(1, w), glu_w, glu_b.reshape(1, w),
      out_g.reshape(1, w))


def _ssm_params(a_re, a_im, log_dt, b_re, b_im, c_re, c_im):
    ar, ai = a_re.astype(F32), a_im.astype(F32)
    dt = jnp.exp(log_dt.astype(F32))[:, None]
    mag = jnp.exp(ar * dt)
    abr, abi = mag * jnp.cos(ai * dt), mag * jnp.sin(ai * dt)
    den = ar * ar + ai * ai
    cr = (((abr - 1.0) * ar + abi * ai) / den)[..., None]
    ci = ((abi * ar - (abr - 1.0) * ai) / den)[..., None]
    br, bi = b_re.astype(F32), b_im.astype(F32)
    bbr, bbi = cr * br - ci * bi, cr * bi + ci * br
    gpt = LANES // SSM_GROUP_CH
    eye = jnp.eye(gpt, dtype=F32)[None, :, None, :, None]

    def bmat(part):
        p = jnp.swapaxes(part.reshape(U_TILES, gpt, SSM_STATE, SSM_GROUP_CH), 2, 3)
        return (p[:, :, :, None, :] * eye).reshape(U_TILES, LANES, gpt * SSM_STATE)

    def cmat(part):
        p = jnp.swapaxes(part.reshape(U_TILES, gpt, SSM_GROUP_CH, SSM_STATE), 2, 3)
        return (p[:, :, :, None, :] * eye).reshape(U_TILES, gpt * SSM_STATE, LANES)

    bblk = jnp.concatenate([bmat(bbr), bmat(bbi)], axis=2).astype(BF16)
    cblk = jnp.concatenate([cmat(c_re.astype(F32)), cmat(-c_im.astype(F32))], axis=1).astype(BF16)
    return bblk, cblk, abr.reshape(RE_TILES, LANES), abi.reshape(RE_TILES, LANES)


BIAS_LANES = 8


def _split3(f):
    p1 = f.astype(BF16).astype(F32)
    r = f - p1
    p2 = r.astype(BF16).astype(F32)
    p3 = (r - p2).astype(BF16).astype(F32)
    return p1, p2, p3


def _bias_cols(f):
    n = f.shape[1]
    one = jnp.ones((1, n), F32)
    zero = jnp.zeros((2, n), F32)
    pieces = _split3(f)
    rows = []
    for key_side in (True, False):
        for h in range(2):
            ph = [p[h:h + 1, :] for p in pieces]
            rows += [one] * 3 + [-p for p in ph] if key_side else ph + [one] * 3
            rows.append(zero)
    pt = jnp.concatenate(rows, axis=0).astype(BF16)
    r = lax.broadcasted_iota(jnp.int32, (4 * BIAS_LANES, 2 * LANES), 0)
    c = lax.broadcasted_iota(jnp.int32, (4 * BIAS_LANES, 2 * LANES), 1)
    place = (c == jnp.where(r < 2 * BIAS_LANES, r, r + LANES - 2 * BIAS_LANES)).astype(BF16)
    cols = lax.dot_general(pt, place, (((0,), (0,)), ((), ())), preferred_element_type=F32)
    return cols[:, :LANES].astype(BF16), cols[:, LANES:].astype(BF16)


V_ROWS = 144
ONES_ROW = 2 * FOX_HEAD_DIM
QUERY_SLAB = 256
SKIP_EXPONENT = 152.0


def _fox_kernel(q_ref, k_ref, v_ref, frow_ref, o_ref, kaug_ref, qaug_ref, vt_ref, j0_ref,
                lhs_ref, s_ref, m_ref, acc_ref, *, t):
    i = pl.program_id(2)
    nq = pl.num_programs(2)
    seq = k_ref.shape[0]
    lane = lax.broadcasted_iota(jnp.int32, (t, LANES), 1)
    first = lane < FOX_HEAD_DIM

    def eye(rows, cols):
        return (lax.broadcasted_iota(jnp.int32, (rows, cols), 0) ==
                lax.broadcasted_iota(jnp.int32, (rows, cols), 1)).astype(BF16)

    def store_lhs(qi):
        r0 = pl.multiple_of(qi * t, t)
        q = q_ref[pl.ds(r0, t), :]
        bias = qaug_ref[pl.ds(r0, t), :]
        zero = jnp.zeros_like(q)
        fzero = jnp.zeros_like(bias)
        slot = qi & 1
        lhs_ref[slot, :t, :] = jnp.concatenate(
            [jnp.where(first, q, zero), jnp.where(lane < BIAS_LANES, bias, fzero)], axis=1)
        lhs_ref[slot, t:, :] = jnp.concatenate(
            [jnp.where(first, zero, q), jnp.where(lane < BIAS_LANES, fzero, bias)], axis=1)

    def reset_stats():
        m_ref[...] = jnp.full_like(m_ref, NEG_INF)
        acc_ref[...] = jnp.zeros_like(acc_ref)

    def head_stats():
        qk = jnp.concatenate([q_ref[...], k_ref[...]], axis=1)
        qkt = _dot_nt(eye(2 * LANES, 2 * LANES), qk)
        qt, kt = qkt[:LANES], qkt[LANES:]

        def per_head(x):
            return jnp.concatenate([jnp.sum(x[:FOX_HEAD_DIM], axis=0, keepdims=True),
                                    jnp.sum(x[FOX_HEAD_DIM:], axis=0, keepdims=True)], axis=0)
        return per_head(qt * qt), per_head(kt * kt), per_head(qt * kt)

    def first_block(qi, qn2, kmax2, diag):
        r0 = qi * t
        f = frow_ref[...]
        rows = jnp.sqrt(qn2[:, r0:r0 + t] * kmax2) + f[:, r0:r0 + t] - diag[:, r0:r0 + t]
        tau = jnp.max(rows, axis=1, keepdims=True) + SKIP_EXPONENT
        pos = lax.broadcasted_iota(jnp.int32, (1, seq), 1)
        far = (f[0:1, :] > tau[0:1, :]) & (f[1:2, :] > tau[1:2, :])
        ends = ((pos & (t - 1)) == t - 1) & (pos < r0)
        return jnp.sum((far & ends).astype(jnp.int32))

    def step(jc, jp, causal, qp=None):
        slot = (i if qp is None else qp) & 1
        if jc is not None:
            vt = vt_ref[jc]
        if jp is not None:
            k0 = pl.multiple_of(jp * t, t)
            rk = jnp.concatenate([k_ref[pl.ds(k0, t), :], kaug_ref[pl.ds(k0, t), :]], axis=1)
        for c0 in range(0, 2 * t, QUERY_SLAB):
            sl = slice(c0, c0 + QUERY_SLAB)
            if jc is not None:
                s = s_ref[:, sl]
                m_old = m_ref[:, sl]
                m_new = jnp.maximum(m_old, jnp.max(s, axis=0, keepdims=True))
                alpha = jnp.exp2(m_old - m_new)
                p = jnp.exp2(s - m_new).astype(BF16)
                acc_ref[:, sl] = alpha * acc_ref[:, sl] + _dot(vt, p)
                m_ref[:, sl] = m_new
            if jp is not None:
                s = _dot_nt(rk, lhs_ref[slot, sl, :])
                if causal is not False:
                    ahead = 0 if causal is True else causal
                    key = lax.broadcasted_iota(jnp.int32, s.shape, 0)
                    qry = (lax.broadcasted_iota(jnp.int32, s.shape, 1) + c0) & (t - 1)
                    s = jnp.where(qry + ahead >= key, s, NEG_INF)
                s_ref[:, sl] = s

    @pl.when(i == 0)
    def _():
        kaug_ref[...], qaug_ref[...] = _bias_cols(frow_ref[...])
        qn2, kn2, diag = head_stats()
        kmax2 = jnp.max(kn2, axis=1, keepdims=True)
        for qi in range(seq // t):
            j0_ref[qi] = first_block(qi, qn2, kmax2, diag)
        vt = _dot_nt(eye(V_ROWS, LANES), v_ref[...])
        row = lax.broadcasted_iota(jnp.int32, vt.shape, 0)
        vt = jnp.where(row == ONES_ROW, 1.0, vt).astype(BF16)
        for jb in range(seq // t):
            vt_ref[jb] = vt[:, jb * t:(jb + 1) * t]
        store_lhs(0)
        reset_stats()
        step(None, 0, True)

    j0 = j0_ref[i]
    n_steady = jnp.maximum(i - 1 - j0, 0)

    def steady_pair(jj, _):
        j = j0 + 2 * jj
        step(j, j + 1, False)
        step(j + 1, j + 2, False)
        return 0
    lax.fori_loop(0, n_steady >> 1, steady_pair, 0)

    def finish():
        nxt = jnp.minimum(i + 1, nq - 1)
        j0_nxt = j0_ref[nxt]
        store_lhs(nxt)
        step(i, j0_nxt, (nxt - j0_nxt) * t, nxt)

        acc = acc_ref[...]
        inv_l = 1.0 / acc[ONES_ROW:ONES_ROW + 1, :]
        o_ref[...] = jnp.concatenate([acc[:FOX_HEAD_DIM, :t] * inv_l[:, :t],
                                      acc[FOX_HEAD_DIM:ONES_ROW, t:] * inv_l[:, t:]], axis=0).astype(BF16)
        reset_stats()

    odd = (n_steady & 1) == 1

    @pl.when(odd)
    def _():
        step(i - 2, i - 1, False)
        step(i - 1, i, True)
        finish()

    @pl.when((i > j0) & jnp.logical_not(odd))
    def _():
        step(i - 1, i, True)
        finish()

    @pl.when(i <= j0)
    def _():
        finish()


def _fox(q, k, v, frow, t):
    b, s, w = q.shape
    assert s % t == 0 and t & (t - 1) == 0
    kern = functools.partial(_fox_kernel, t=t)
    whole = lambda bi, p, i: (bi, 0, p)
    return pl.pallas_call(
        kern,
        grid=(b, FOX_PAIRS, s // t),
        in_specs=[
            pl.BlockSpec((None, s, LANES), whole),
            pl.BlockSpec((None, s, LANES), whole),
            pl.BlockSpec((None, s, LANES), whole),
            pl.BlockSpec((None, None, 2, s), lambda bi, p, i: (bi, p, 0, 0)),
        ],
        out_specs=pl.BlockSpec((None, LANES, t), lambda bi, p, i: (bi, p, i)),
        out_shape=jax.ShapeDtypeStruct((b, w, s), BF16),
        scratch_shapes=[
            pltpu.VMEM((s, LANES), BF16),
            pltpu.VMEM((s, LANES), BF16),
            pltpu.VMEM((s // t, V_ROWS, t), BF16),
            pltpu.SMEM((s // t,), jnp.int32),
            pltpu.VMEM((2, 2 * t, 2 * LANES), BF16),
            pltpu.VMEM((t, 2 * t), F32),
            pltpu.VMEM((1, 2 * t), F32),
            pltpu.VMEM((V_ROWS, 2 * t), F32),
        ],
        compiler_params=pltpu.CompilerParams(
            dimension_semantics=("arbitrary", "arbitrary", "arbitrary"),
            vmem_limit_bytes=VMEM_LIMIT_BYTES),
        name="fox",
    )(q, k, v, frow)


TAIL_SUB_ROWS = 256
FFN_CHUNK = 1024


def _tail_kernel(x_ref, ys_ref, yf_ref, wqk_ref, vo_ref,
                 fox_g_ref, w_out_ref, mix_post_ref, xa_pre_ref, xa_post_ref,
                 ffn_pre_ref, wg_ref, wu_ref, wd_ref, ffn_post_ref, o_ref, *, n_sub):
    sub = x_ref.shape[0] // n_sub
    rows = [slice(r * sub, (r + 1) * sub) for r in range(n_sub)]
    each = lambda f, *lists: [f(*a) for a in zip(*lists)]

    x = [x_ref[r, :] for r in rows]
    yft = [yf_ref[:, r].astype(F32) for r in rows]
    yft = [(y * lax.rsqrt(jnp.mean(y * y, axis=0, keepdims=True) + RMS_EPS) * fox_g_ref[...]).astype(BF16)
           for y in yft]
    mix = [_dot(ys_ref[r, :], w_out_ref[:SSM_WIDTH, :]) +
           lax.dot_general(f, w_out_ref[SSM_WIDTH:, :], (((0,), (0,)), ((), ())), preferred_element_type=F32)
           for r, f in zip(rows, yft)]
    x = each(lambda xi, mi: xi + _rms(mi, mix_post_ref[...]), x, mix)

    h = [_rms(xi, xa_pre_ref[...]).astype(BF16) for xi in x]
    scores = [_dot(hi, wqk_ref[...]) for hi in h]
    m_tok = wqk_ref.shape[1] // XA_HEADS
    probs = [[] for _ in rows]
    for hd in range(XA_HEADS):
        for si, out in zip(scores, probs):
            s = si[:, hd * m_tok:(hd + 1) * m_tok]
            p = jnp.exp(s - jnp.max(s, axis=-1, keepdims=True))
            out.append((p / jnp.sum(p, axis=-1, keepdims=True)).astype(BF16))
    p_all = [jnp.concatenate(ps, axis=1) for ps in probs]
    x = each(lambda xi, pi: xi + _rms(_dot(pi, vo_ref[...]), xa_post_ref[...]), x, p_all)

    h = [_rms(xi, ffn_pre_ref[...]).astype(BF16) for xi in x]
    hidden = wg_ref.shape[1]
    down = [None] * n_sub
    for c0 in range(0, hidden, FFN_CHUNK):
        c1 = min(c0 + FFN_CHUNK, hidden)
        for r, hi in enumerate(h):
            gate = _dot(hi, wg_ref[:, c0:c1])
            up = _dot(hi, wu_ref[:, c0:c1])
            act = (gate * _sigmoid(gate) * up).astype(BF16)
            part = _dot(act, wd_ref[c0:c1, :])
            down[r] = part if down[r] is None else down[r] + part
    for r, xi, di in zip(rows, x, down):
        o_ref[r, :] = xi + _rms(di, ffn_post_ref[...])


def _tail(x, ys, yf, wqk, vo, fox_g, w_out, mix_post_g, xa_pre_g, xa_post_g,
          ffn_pre_g, wg, wu, wd, ffn_post_g, tm):
    b, s, d = x.shape
    row = lambda i, j: (i, j, 0)
    per_b = lambda i, j: (i, 0, 0)
    const = lambda i, j: (0, 0)

    def resident(a):
        return pl.BlockSpec(a.shape, const, pipeline_mode=pl.Buffered(1))

    gains = [g.reshape(1, -1) for g in (mix_post_g, xa_pre_g, xa_post_g, ffn_pre_g, ffn_post_g)]
    mix_post_g, xa_pre_g, xa_post_g, ffn_pre_g, ffn_post_g = gains
    fox_g = fox_g.reshape(-1, 1)
    args = (x, ys, yf, wqk, vo, fox_g, w_out, mix_post_g, xa_pre_g, xa_post_g,
            ffn_pre_g, wg, wu, wd, ffn_post_g)
    in_specs = [
        pl.BlockSpec((None, tm, d), row),
        pl.BlockSpec((None, tm, SSM_WIDTH), row),
        pl.BlockSpec((None, FOX_WIDTH, tm), lambda i, j: (i, 0, j)),
        pl.BlockSpec((None,) + wqk.shape[1:], per_b),
        pl.BlockSpec((None,) + vo.shape[1:], per_b),
    ] + [resident(a) for a in args[5:]]
    return pl.pallas_call(
        functools.partial(_tail_kernel, n_sub=tm // TAIL_SUB_ROWS),
        grid=(b, s // tm),
        in_specs=in_specs,
        out_specs=pl.BlockSpec((None, tm, d), row),
        out_shape=jax.ShapeDtypeStruct((b, s, d), F32),
        compiler_params=pltpu.CompilerParams(
            dimension_semantics=("arbitrary", "arbitrary"), vmem_limit_bytes=VMEM_LIMIT_BYTES),
        name="tail",
    )(*args)


def _pick(n, pref):
    t = min(n, pref)
    assert n % t == 0, (n, t)
    return t


def kernel(x, mem, mix_pre_g, w_in, ssm_a_re, ssm_a_im, ssm_log_dt, ssm_b_re, ssm_b_im, ssm_c_re, ssm_c_im, ssm_d, ssm_glu_w, ssm_glu_b, fox_f_bias, ssm_out_g, fox_out_g, w_out, mix_post_g, xa_pre_g, mem_g, xa_wq, xa_wkv, xa_wo, xa_post_g, ffn_pre_g, w_gate, w_up, w_down, ffn_post_g):
    b, s, d = x.shape
    assert d == D_MODEL and s % LANES == 0
    n_main = SSM_WIDTH + 3 * FOX_WIDTH

    wqk, vo = _memkv(mem, mem_g, xa_wkv.astype(BF16), xa_wq.astype(BF16), xa_wo.astype(BF16))

    u, q, k, v, fcum = _inproj(
        x, mix_pre_g, w_in[:, :n_main].astype(BF16), w_in[:, n_main:].T.astype(BF16),
        fox_f_bias, tm=_pick(s, 1024))

    bblk, cblk, are, aim = _ssm_params(ssm_a_re, ssm_a_im, ssm_log_dt, ssm_b_re, ssm_b_im,
                                       ssm_c_re, ssm_c_im)
    y_ssm = _ssm(u, bblk, cblk, are, aim, ssm_d, ssm_glu_w.astype(BF16), ssm_glu_b, ssm_out_g,
                 tm=_pick(s, 128))

    y_fox = _fox(q, k, v, fcum.reshape(b, FOX_PAIRS, 2, s), t=_pick(s, 512))

    return _tail(x, y_ssm, y_fox, wqk, vo, fox_out_g, w_out.astype(BF16), mix_post_g, xa_pre_g,
                 xa_post_g, ffn_pre_g,
                 w_gate.astype(BF16), w_up.astype(BF16), w_down.astype(BF16), ffn_post_g,
                 tm=_pick(s, 2 * TAIL_SUB_ROWS))
```

```python
import functools
import math

import jax
import jax.numpy as jnp
from jax import lax
from jax.experimental import pallas as pl
from jax.experimental.pallas import tpu as pltpu

F32 = jnp.float32
BF16 = jnp.bfloat16

LANES = 128
VMEM_LIMIT_BYTES = 56 * 1024 * 1024

D_MODEL = 1024
SSM_WIDTH = 512
SSM_GROUP_CH = 16
SSM_GROUPS = 32
SSM_STATE = 64
FOX_WIDTH = 512
FOX_HEAD_DIM = 64
FOX_HEADS = 8
FOX_PAIRS = FOX_HEADS * FOX_HEAD_DIM // LANES
XA_HEADS = 4
XA_HEAD_DIM = 256
RMS_EPS = 1e-6
NEG_INF = -1e30
LOG2E = math.log2(math.e)

STATE_TILES = 2 * SSM_GROUPS * SSM_STATE // LANES
RE_TILES = STATE_TILES // 2
U_TILES = SSM_WIDTH // LANES


def _rms(x, g):
    return x * lax.rsqrt(jnp.mean(x * x, axis=-1, keepdims=True) + RMS_EPS) * g


def _dot(a, b):
    return jnp.dot(a, b, preferred_element_type=F32)


def _dot_nt(a, b):
    return lax.dot_general(a, b, (((1,), (1,)), ((), ())), preferred_element_type=F32)


def _sigmoid(z):
    return 1.0 / (1.0 + jnp.exp(-z))


def _memkv_kernel(mem_ref, g_ref, wkv_ref, wq_ref, wo_ref, wqk_ref, vo_ref):
    m = mem_ref.shape[0]
    mn = _rms(mem_ref[...], g_ref[...]).astype(BF16)
    kv = _dot(mn, wkv_ref[...])
    for hd in range(XA_HEADS):
        sl = slice(hd * XA_HEAD_DIM, (hd + 1) * XA_HEAD_DIM)
        k_h = (kv[:, sl] * (1.0 / math.sqrt(XA_HEAD_DIM))).astype(BF16)
        v_h = kv[:, D_MODEL + hd * XA_HEAD_DIM:D_MODEL + (hd + 1) * XA_HEAD_DIM].astype(BF16)
        wqk_ref[:, hd * m:(hd + 1) * m] = _dot_nt(wq_ref[:, sl], k_h).astype(BF16)
        vo_ref[hd * m:(hd + 1) * m, :] = _dot(v_h, wo_ref[sl, :]).astype(BF16)


def _memkv(mem, mem_g, wkv, wq, wo):
    b, m, d = mem.shape
    c2 = lambda i: (0, 0)
    return pl.pallas_call(
        _memkv_kernel,
        grid=(b,),
        in_specs=[
            pl.BlockSpec((None, m, d), lambda i: (i, 0, 0)),
            pl.BlockSpec((1, d), c2),
            pl.BlockSpec((d, 2 * d), c2),
            pl.BlockSpec((d, d), c2),
            pl.BlockSpec((d, d), c2),
        ],
        out_specs=[
            pl.BlockSpec((None, d, XA_HEADS * m), lambda i: (i, 0, 0)),
            pl.BlockSpec((None, XA_HEADS * m, d), lambda i: (i, 0, 0)),
        ],
        out_shape=[jax.ShapeDtypeStruct((b, d, XA_HEADS * m), BF16),
                   jax.ShapeDtypeStruct((b, XA_HEADS * m, d), BF16)],
        compiler_params=pltpu.CompilerParams(
            dimension_semantics=("arbitrary",), vmem_limit_bytes=VMEM_LIMIT_BYTES),
        name="memkv",
    )(mem, mem_g.reshape(1, d), wkv, wq, wo)


CUMSUM_SEGMENT = 512


def _inproj_kernel(x_ref, g_ref, w_ref, wf_ref, fb_ref, tri_ref,
                   u_ref, q_ref, k_ref, v_ref, f_ref, carry_ref):
    @pl.when(pl.program_id(1) == 0)
    def _():
        carry_ref[...] = jnp.zeros_like(carry_ref)

    hb = _rms(x_ref[...], g_ref[...]).astype(BF16)
    proj = _dot(hb, w_ref[...])
    u_ref[...] = proj[:, :SSM_WIDTH].astype(BF16)
    o = SSM_WIDTH
    q_ref[...] = (proj[:, o:o + FOX_WIDTH] * (LOG2E / math.sqrt(FOX_HEAD_DIM))).astype(BF16)
    k_ref[...] = proj[:, o + FOX_WIDTH:o + 2 * FOX_WIDTH].astype(BF16)
    v_ref[...] = proj[:, o + 2 * FOX_WIDTH:].astype(BF16)

    z = _dot_nt(wf_ref[...], hb) + fb_ref[...]
    logf = jnp.minimum(z, 0.0) - jnp.log1p(jnp.exp(-jnp.abs(z)))
    p1 = logf.astype(BF16)
    r1 = logf - p1.astype(F32)
    p2 = r1.astype(BF16)
    p3 = (r1 - p2.astype(F32)).astype(BF16)
    tri = tri_ref[...]
    seg = tri.shape[0]
    run = carry_ref[...][:, :1]
    parts = []
    for c0 in range(0, logf.shape[1], seg):
        sl = slice(c0, c0 + seg)
        parts.append(_dot(p1[:, sl], tri) + _dot(p2[:, sl], tri) + _dot(p3[:, sl], tri) + run)
        run = parts[-1][:, -1:]
    f = jnp.concatenate(parts, axis=1)
    f_ref[...] = f * LOG2E
    carry_ref[...] = jnp.broadcast_to(f[:, -1:], carry_ref.shape)


def _inproj(x, mix_pre_g, w_main, wf_t, f_bias, tm):
    b, s, d = x.shape
    nt = s // tm
    seg = min(tm, CUMSUM_SEGMENT)
    tri = jnp.triu(jnp.ones((seg, seg), F32)).astype(BF16)
    row = lambda i, j: (i, j, 0)
    const2 = lambda i, j: (0, 0)
    return pl.pallas_call(
        _inproj_kernel,
        grid=(b, nt),
        in_specs=[
            pl.BlockSpec((None, tm, d), row),
            pl.BlockSpec((1, d), const2),
            pl.BlockSpec(w_main.shape, const2),
            pl.BlockSpec(wf_t.shape, const2),
            pl.BlockSpec((FOX_HEADS, 1), const2),
            pl.BlockSpec((seg, seg), const2),
        ],
        out_specs=[
            pl.BlockSpec((None, tm, SSM_WIDTH), row),
            pl.BlockSpec((None, tm, FOX_WIDTH), row),
            pl.BlockSpec((None, tm, FOX_WIDTH), row),
            pl.BlockSpec((None, tm, FOX_WIDTH), row),
            pl.BlockSpec((None, FOX_HEADS, tm), lambda i, j: (i, 0, j)),
        ],
        out_shape=[
            jax.ShapeDtypeStruct((b, s, SSM_WIDTH), BF16),
            jax.ShapeDtypeStruct((b, s, FOX_WIDTH), BF16),
            jax.ShapeDtypeStruct((b, s, FOX_WIDTH), BF16),
            jax.ShapeDtypeStruct((b, s, FOX_WIDTH), BF16),
            jax.ShapeDtypeStruct((b, FOX_HEADS, s), F32),
        ],
        scratch_shapes=[pltpu.VMEM((FOX_HEADS, LANES), F32)],
        compiler_params=pltpu.CompilerParams(
            dimension_semantics=("arbitrary", "arbitrary"), vmem_limit_bytes=VMEM_LIMIT_BYTES),
        name="inproj",
    )(x, mix_pre_g.reshape(1, d), w_main, wf_t, f_bias.reshape(FOX_HEADS, 1), tri)


def _gelu_tanh(x):
    c = math.sqrt(2.0 / math.pi)
    return 0.5 * x * (1.0 + jnp.tanh(c * (x + 0.044715 * (x * x * x))))


TILE_PITCH = 20
TIME_PITCH = 324
SCAN_UNROLL = 8


def _ssm_kernel(u_ref, bblk_ref, cblk_ref, are_ref, aim_ref, d_ref, gw_ref, gb_ref, g_ref,
                o_ref, s_ref, xc_ref, *, nb, tm):
    @pl.when(pl.program_id(0) == 0)
    def _():
        xc_ref[...] = jnp.zeros_like(xc_ref)

    def rows_of_tile(b, part, k):
        return pl.ds(TILE_PITCH * k + 2 * b + part, tm, stride=TIME_PITCH)

    for ut in range(U_TILES):
        ub = u_ref[:, :, ut * LANES:(ut + 1) * LANES].reshape(nb * tm, LANES)
        res = _dot(ub, bblk_ref[ut])
        for b in range(nb):
            for j in range(2 * U_TILES):
                part, k = divmod(j, U_TILES)
                s_ref[rows_of_tile(b, part, U_TILES * ut + k), :] = (
                    res[b * tm:(b + 1) * tm, j * LANES:(j + 1) * LANES])

    ar = are_ref[...]
    ai = aim_ref[...]

    def step(t, carry):
        new = []
        for b in range(nb):
            xr, xi = carry[2 * b], carry[2 * b + 1]
            ire = pl.ds(TIME_PITCH * t + 2 * b, RE_TILES, stride=TILE_PITCH)
            iim = pl.ds(TIME_PITCH * t + 2 * b + 1, RE_TILES, stride=TILE_PITCH)
            nr = ar * xr - ai * xi + s_ref[ire, :]
            ni = ar * xi + ai * xr + s_ref[iim, :]
            s_ref[ire, :] = nr
            s_ref[iim, :] = ni
            new += [nr, ni]
        return tuple(new)

    init = []
    for b in range(nb):
        init.append(xc_ref[b * STATE_TILES:b * STATE_TILES + RE_TILES, :])
        init.append(xc_ref[b * STATE_TILES + RE_TILES:(b + 1) * STATE_TILES, :])
    fin = lax.fori_loop(0, tm, step, tuple(init), unroll=SCAN_UNROLL)
    for b in range(nb):
        xc_ref[b * STATE_TILES:b * STATE_TILES + RE_TILES, :] = fin[2 * b]
        xc_ref[b * STATE_TILES + RE_TILES:(b + 1) * STATE_TILES, :] = fin[2 * b + 1]

    ys = []
    for ot in range(U_TILES):
        per_seq = []
        for b in range(nb):
            tiles = [s_ref[rows_of_tile(b, part, U_TILES * ot + k), :].astype(BF16)
                     for part in range(2) for k in range(U_TILES)]
            per_seq.append(jnp.concatenate(tiles, axis=1))
        ys.append(_dot(jnp.concatenate(per_seq, axis=0), cblk_ref[ot]))
    u_all = u_ref[...].reshape(nb * tm, SSM_WIDTH).astype(F32)
    y = jnp.concatenate(ys, axis=1) + d_ref[...] * u_all
    g = _gelu_tanh(y)
    out = g * _sigmoid(_dot(g.astype(BF16), gw_ref[...]) + gb_ref[...])
    o_ref[...] = _rms(out, g_ref[...]).astype(BF16).reshape(nb, tm, SSM_WIDTH)


def _ssm(u, bblk, cblk, a_re, a_im, d_skip, glu_w, glu_b, out_g, tm):
    b, s, w = u.shape
    assert 2 * b <= TILE_PITCH - 4 and TIME_PITCH >= TILE_PITCH * (RE_TILES - 1) + 2 * b
    kern = functools.partial(_ssm_kernel, nb=b, tm=tm)
    c2 = lambda t: (0, 0)
    c3 = lambda t: (0, 0, 0)
    return pl.pallas_call(
        kern,
        grid=(s // tm,),
        in_specs=[
            pl.BlockSpec((b, tm, w), lambda t: (0, t, 0)),
            pl.BlockSpec(bblk.shape, c3),
            pl.BlockSpec(cblk.shape, c3),
            pl.BlockSpec(a_re.shape, c2),
            pl.BlockSpec(a_im.shape, c2),
            pl.BlockSpec((1, w), c2),
            pl.BlockSpec(glu_w.shape, c2),
            pl.BlockSpec((1, w), c2),
            pl.BlockSpec((1, w), c2),
        ],
        out_specs=pl.BlockSpec((b, tm, w), lambda t: (0, t, 0)),
        out_shape=jax.ShapeDtypeStruct((b, s, w), BF16),
        scratch_shapes=[
            pltpu.VMEM((tm * TIME_PITCH, LANES), F32),
            pltpu.VMEM((b * STATE_TILES, LANES), F32),
        ],
        compiler_params=pltpu.CompilerParams(
            dimension_semantics=("arbitrary",), vmem_limit_bytes=VMEM_LIMIT_BYTES),
        name="ssm",
    )(u, bblk, cblk, a_re, a_im, d_skip.reshape(1, w), glu_w, glu_b.reshape(1, w),
      out_g.reshape(1, w))


def _ssm_params(a_re, a_im, log_dt, b_re, b_im, c_re, c_im):
    ar, ai = a_re.astype(F32), a_im.astype(F32)
    dt = jnp.exp(log_dt.astype(F32))[:, None]
    mag = jnp.exp(ar * dt)
    abr, abi = mag * jnp.cos(ai * dt), mag * jnp.sin(ai * dt)
    den = ar * ar + ai * ai
    cr = (((abr - 1.0) * ar + abi * ai) / den)[..., None]
    ci = ((abi * ar - (abr - 1.0) * ai) / den)[..., None]
    br, bi = b_re.astype(F32), b_im.astype(F32)
    bbr, bbi = cr * br - ci * bi, cr * bi + ci * br
    gpt = LANES // SSM_GROUP_CH
    eye = jnp.eye(gpt, dtype=F32)[None, :, None, :, None]

    def bmat(part):
        p = jnp.swapaxes(part.reshape(U_TILES, gpt, SSM_STATE, SSM_GROUP_CH), 2, 3)
        return (p[:, :, :, None, :] * eye).reshape(U_TILES, LANES, gpt * SSM_STATE)

    def cmat(part):
        p = jnp.swapaxes(part.reshape(U_TILES, gpt, SSM_GROUP_CH, SSM_STATE), 2, 3)
        return (p[:, :, :, None, :] * eye).reshape(U_TILES, gpt * SSM_STATE, LANES)

    bblk = jnp.concatenate([bmat(bbr), bmat(bbi)], axis=2).astype(BF16)
    cblk = jnp.concatenate([cmat(c_re.astype(F32)), cmat(-c_im.astype(F32))], axis=1).astype(BF16)
    return bblk, cblk, abr.reshape(RE_TILES, LANES), abi.reshape(RE_TILES, LANES)


BIAS_LANES = 8


def _split3(f):
    p1 = f.astype(BF16).astype(F32)
    r = f - p1
    p2 = r.astype(BF16).astype(F32)
    p3 = (r - p2).astype(BF16).astype(F32)
    return p1, p2, p3


def _bias_cols(f):
    n = f.shape[1]
    one = jnp.ones((1, n), F32)
    zero = jnp.zeros((2, n), F32)
    pieces = _split3(f)
    rows = []
    for key_side in (True, False):
        for h in range(2):
            ph = [p[h:h + 1, :] for p in pieces]
            rows += [one] * 3 + [-p for p in ph] if key_side else ph + [one] * 3
            rows.append(zero)
    pt = jnp.concatenate(rows, axis=0).astype(BF16)
    r = lax.broadcasted_iota(jnp.int32, (4 * BIAS_LANES, 2 * LANES), 0)
    c = lax.broadcasted_iota(jnp.int32, (4 * BIAS_LANES, 2 * LANES), 1)
    place = (c == jnp.where(r < 2 * BIAS_LANES, r, r + LANES - 2 * BIAS_LANES)).astype(BF16)
    cols = lax.dot_general(pt, place, (((0,), (0,)), ((), ())), preferred_element_type=F32)
    return cols[:, :LANES].astype(BF16), cols[:, LANES:].astype(BF16)


V_ROWS = 144
ONES_ROW = 2 * FOX_HEAD_DIM
QUERY_SLAB = 256
SKIP_EXPONENT = 152.0


def _fox_kernel(q_ref, k_ref, v_ref, frow_ref, o_ref, kaug_ref, qaug_ref, vt_ref, j0_ref,
                lhs_ref, s_ref, m_ref, acc_ref, *, t):
    i = pl.program_id(2)
    nq = pl.num_programs(2)
    seq = k_ref.shape[0]
    lane = lax.broadcasted_iota(jnp.int32, (t, LANES), 1)
    first = lane < FOX_HEAD_DIM

    def eye(rows, cols):
        return (lax.broadcasted_iota(jnp.int32, (rows, cols), 0) ==
                lax.broadcasted_iota(jnp.int32, (rows, cols), 1)).astype(BF16)

    def store_lhs(qi):
        r0 = pl.multiple_of(qi * t, t)
        q = q_ref[pl.ds(r0, t), :]
        bias = qaug_ref[pl.ds(r0, t), :]
        zero = jnp.zeros_like(q)
        fzero = jnp.zeros_like(bias)
        slot = qi & 1
        lhs_ref[slot, :t, :] = jnp.concatenate(
            [jnp.where(first, q, zero), jnp.where(lane < BIAS_LANES, bias, fzero)], axis=1)
        lhs_ref[slot, t:, :] = jnp.concatenate(
            [jnp.where(first, zero, q), jnp.where(lane < BIAS_LANES, fzero, bias)], axis=1)

    def reset_stats():
        m_ref[...] = jnp.full_like(m_ref, NEG_INF)
        acc_ref[...] = jnp.zeros_like(acc_ref)

    def head_stats():
        qk = jnp.concatenate([q_ref[...], k_ref[...]], axis=1)
        qkt = _dot_nt(eye(2 * LANES, 2 * LANES), qk)
        qt, kt = qkt[:LANES], qkt[LANES:]

        def per_head(x):
            return jnp.concatenate([jnp.sum(x[:FOX_HEAD_DIM], axis=0, keepdims=True),
                                    jnp.sum(x[FOX_HEAD_DIM:], axis=0, keepdims=True)], axis=0)
        return per_head(qt * qt), per_head(kt * kt), per_head(qt * kt)

    def first_block(qi, qn2, kmax2, diag):
        r0 = qi * t
        f = frow_ref[...]
        rows = jnp.sqrt(qn2[:, r0:r0 + t] * kmax2) + f[:, r0:r0 + t] - diag[:, r0:r0 + t]
        tau = jnp.max(rows, axis=1, keepdims=True) + SKIP_EXPONENT
        pos = lax.broadcasted_iota(jnp.int32, (1, seq), 1)
        far = (f[0:1, :] > tau[0:1, :]) & (f[1:2, :] > tau[1:2, :])
        ends = ((pos & (t - 1)) == t - 1) & (pos < r0)
        return jnp.sum((far & ends).astype(jnp.int32))

    def step(jc, jp, causal, qp=None, diag_jc=False):
        slot = (i if qp is None else qp) & 1

        def keys_seen(c0, diagonal):
            return min(t, c0 % t + QUERY_SLAB) if diagonal else t
        if jc is not None:
            vt = vt_ref[jc]
        if jp is not None:
            k0 = pl.multiple_of(jp * t, t)
            rk = jnp.concatenate([k_ref[pl.ds(k0, t), :], kaug_ref[pl.ds(k0, t), :]], axis=1)
        for c0 in range(0, 2 * t, QUERY_SLAB):
            sl = slice(c0, c0 + QUERY_SLAB)
            if jc is not None:
                nk = keys_seen(c0, diag_jc)
                s = s_ref[:nk, sl]
                m_old = m_ref[:, sl]
                m_new = jnp.maximum(m_old, jnp.max(s, axis=0, keepdims=True))
                alpha = jnp.exp2(m_old - m_new)
                p = jnp.exp2(s - m_new).astype(BF16)
                acc_ref[:, sl] = alpha * acc_ref[:, sl] + _dot(vt[:, :nk], p)
                m_ref[:, sl] = m_new
            if jp is not None:
                nk = keys_seen(c0, causal is True)
                s = _dot_nt(rk[:nk], lhs_ref[slot, sl, :])
                if causal is not False:
                    ahead = 0 if causal is True else causal
                    key = lax.broadcasted_iota(jnp.int32, s.shape, 0)
                    qry = (lax.broadcasted_iota(jnp.int32, s.shape, 1) + c0) & (t - 1)
                    s = jnp.where(qry + ahead >= key, s, NEG_INF)
                s_ref[:nk, sl] = s

    @pl.when(i == 0)
    def _():
        kaug_ref[...], qaug_ref[...] = _bias_cols(frow_ref[...])
        qn2, kn2, diag = head_stats()
        kmax2 = jnp.max(kn2, axis=1, keepdims=True)
        for qi in range(seq // t):
            j0_ref[qi] = first_block(qi, qn2, kmax2, diag)
        vt = _dot_nt(eye(V_ROWS, LANES), v_ref[...])
        row = lax.broadcasted_iota(jnp.int32, vt.shape, 0)
        vt = jnp.where(row == ONES_ROW, 1.0, vt).astype(BF16)
        for jb in range(seq // t):
            vt_ref[jb] = vt[:, jb * t:(jb + 1) * t]
        store_lhs(0)
        reset_stats()
        step(None, 0, True)

    j0 = j0_ref[i]
    n_steady = jnp.maximum(i - 1 - j0, 0)

    def steady_pair(jj, _):
        j = j0 + 2 * jj
        step(j, j + 1, False)
        step(j + 1, j + 2, False)
        return 0
    lax.fori_loop(0, n_steady >> 1, steady_pair, 0)

    def finish():
        nxt = jnp.minimum(i + 1, nq - 1)
        j0_nxt = j0_ref[nxt]
        store_lhs(nxt)
        step(i, j0_nxt, (nxt - j0_nxt) * t, nxt, diag_jc=True)

        acc = acc_ref[...]
        inv_l = 1.0 / acc[ONES_ROW:ONES_ROW + 1, :]
        o_ref[...] = jnp.concatenate([acc[:FOX_HEAD_DIM, :t] * inv_l[:, :t],
                                      acc[FOX_HEAD_DIM:ONES_ROW, t:] * inv_l[:, t:]], axis=0).astype(BF16)
        reset_stats()

    odd = (n_steady & 1) == 1

    @pl.when(odd)
    def _():
        step(i - 2, i - 1, False)
        step(i - 1, i, True)
        finish()

    @pl.when((i > j0) & jnp.logical_not(odd))
    def _():
        step(i - 1, i, True)
        finish()

    @pl.when(i <= j0)
    def _():
        finish()


def _fox(q, k, v, frow, t):
    b, s, w = q.shape
    assert s % t == 0 and t & (t - 1) == 0
    kern = functools.partial(_fox_kernel, t=t)
    whole = lambda bi, p, i: (bi, 0, p)
    return pl.pallas_call(
        kern,
        grid=(b, FOX_PAIRS, s // t),
        in_specs=[
            pl.BlockSpec((None, s, LANES), whole),
            pl.BlockSpec((None, s, LANES), whole),
            pl.BlockSpec((None, s, LANES), whole),
            pl.BlockSpec((None, None, 2, s), lambda bi, p, i: (bi, p, 0, 0)),
        ],
        out_specs=pl.BlockSpec((None, LANES, t), lambda bi, p, i: (bi, p, i)),
        out_shape=jax.ShapeDtypeStruct((b, w, s), BF16),
        scratch_shapes=[
            pltpu.VMEM((s, LANES), BF16),
            pltpu.VMEM((s, LANES), BF16),
            pltpu.VMEM((s // t, V_ROWS, t), BF16),
            pltpu.SMEM((s // t,), jnp.int32),
            pltpu.VMEM((2, 2 * t, 2 * LANES), BF16),
            pltpu.VMEM((t, 2 * t), F32),
            pltpu.VMEM((1, 2 * t), F32),
            pltpu.VMEM((V_ROWS, 2 * t), F32),
        ],
        compiler_params=pltpu.CompilerParams(
            dimension_semantics=("arbitrary", "arbitrary", "arbitrary"),
            vmem_limit_bytes=VMEM_LIMIT_BYTES),
        name="fox",
    )(q, k, v, frow)


TAIL_SUB_ROWS = 256
FFN_CHUNK = 1024


def _tail_kernel(x_ref, ys_ref, yf_ref, wqk_ref, vo_ref,
                 fox_g_ref, w_out_ref, mix_post_ref, xa_pre_ref, xa_post_ref,
                 ffn_pre_ref, wg_ref, wu_ref, wd_ref, ffn_post_ref, o_ref, *, n_sub):
    sub = x_ref.shape[0] // n_sub
    rows = [slice(r * sub, (r + 1) * sub) for r in range(n_sub)]
    each = lambda f, *lists: [f(*a) for a in zip(*lists)]

    x = [x_ref[r, :] for r in rows]
    yft = [yf_ref[:, r].astype(F32) for r in rows]
    yft = [(y * lax.rsqrt(jnp.mean(y * y, axis=0, keepdims=True) + RMS_EPS) * fox_g_ref[...]).astype(BF16)
           for y in yft]
    mix = [_dot(ys_ref[r, :], w_out_ref[:SSM_WIDTH, :]) +
           lax.dot_general(f, w_out_ref[SSM_WIDTH:, :], (((0,), (0,)), ((), ())), preferred_element_type=F32)
           for r, f in zip(rows, yft)]
    x = each(lambda xi, mi: xi + _rms(mi, mix_post_ref[...]), x, mix)

    h = [_rms(xi, xa_pre_ref[...]).astype(BF16) for xi in x]
    scores = [_dot(hi, wqk_ref[...]) for hi in h]
    m_tok = wqk_ref.shape[1] // XA_HEADS
    probs = [[] for _ in rows]
    for hd in range(XA_HEADS):
        for si, out in zip(scores, probs):
            s = si[:, hd * m_tok:(hd + 1) * m_tok]
            p = jnp.exp(s - jnp.max(s, axis=-1, keepdims=True))
            out.append((p / jnp.sum(p, axis=-1, keepdims=True)).astype(BF16))
    p_all = [jnp.concatenate(ps, axis=1) for ps in probs]
    x = each(lambda xi, pi: xi + _rms(_dot(pi, vo_ref[...]), xa_post_ref[...]), x, p_all)

    h = [_rms(xi, ffn_pre_ref[...]).astype(BF16) for xi in x]
    hidden = wg_ref.shape[1]
    down = [None] * n_sub
    for c0 in range(0, hidden, FFN_CHUNK):
        c1 = min(c0 + FFN_CHUNK, hidden)
        for r, hi in enumerate(h):
            gate = _dot(hi, wg_ref[:, c0:c1])
            up = _dot(hi, wu_ref[:, c0:c1])
            act = (gate * _sigmoid(gate) * up).astype(BF16)
            part = _dot(act, wd_ref[c0:c1, :])
            down[r] = part if down[r] is None else down[r] + part
    for r, xi, di in zip(rows, x, down):
        o_ref[r, :] = xi + _rms(di, ffn_post_ref[...])


def _tail(x, ys, yf, wqk, vo, fox_g, w_out, mix_post_g, xa_pre_g, xa_post_g,
          ffn_pre_g, wg, wu, wd, ffn_post_g, tm):
    b, s, d = x.shape
    row = lambda i, j: (i, j, 0)
    per_b = lambda i, j: (i, 0, 0)
    const = lambda i, j: (0, 0)

    def resident(a):
        return pl.BlockSpec(a.shape, const, pipeline_mode=pl.Buffered(1))

    gains = [g.reshape(1, -1) for g in (mix_post_g, xa_pre_g, xa_post_g, ffn_pre_g, ffn_post_g)]
    mix_post_g, xa_pre_g, xa_post_g, ffn_pre_g, ffn_post_g = gains
    fox_g = fox_g.reshape(-1, 1)
    args = (x, ys, yf, wqk, vo, fox_g, w_out, mix_post_g, xa_pre_g, xa_post_g,
            ffn_pre_g, wg, wu, wd, ffn_post_g)
    in_specs = [
        pl.BlockSpec((None, tm, d), row),
        pl.BlockSpec((None, tm, SSM_WIDTH), row),
        pl.BlockSpec((None, FOX_WIDTH, tm), lambda i, j: (i, 0, j)),
        pl.BlockSpec((None,) + wqk.shape[1:], per_b),
        pl.BlockSpec((None,) + vo.shape[1:], per_b),
    ] + [resident(a) for a in args[5:]]
    return pl.pallas_call(
        functools.partial(_tail_kernel, n_sub=tm // TAIL_SUB_ROWS),
        grid=(b, s // tm),
        in_specs=in_specs,
        out_specs=pl.BlockSpec((None, tm, d), row),
        out_shape=jax.ShapeDtypeStruct((b, s, d), F32),
        compiler_params=pltpu.CompilerParams(
            dimension_semantics=("arbitrary", "arbitrary"), vmem_limit_bytes=VMEM_LIMIT_BYTES),
        name="tail",
    )(*args)


def _pick(n, pref):
    t = min(n, pref)
    assert n % t == 0, (n, t)
    return t


def kernel(x, mem, mix_pre_g, w_in, ssm_a_re, ssm_a_im, ssm_log_dt, ssm_b_re, ssm_b_im, ssm_c_re, ssm_c_im, ssm_d, ssm_glu_w, ssm_glu_b, fox_f_bias, ssm_out_g, fox_out_g, w_out, mix_post_g, xa_pre_g, mem_g, xa_wq, xa_wkv, xa_wo, xa_post_g, ffn_pre_g, w_gate, w_up, w_down, ffn_post_g):
    b, s, d = x.shape
    assert d == D_MODEL and s % LANES == 0
    n_main = SSM_WIDTH + 3 * FOX_WIDTH

    wqk, vo = _memkv(mem, mem_g, xa_wkv.astype(BF16), xa_wq.astype(BF16), xa_wo.astype(BF16))

    u, q, k, v, fcum = _inproj(
        x, mix_pre_g, w_in[:, :n_main].astype(BF16), w_in[:, n_main:].T.astype(BF16),
        fox_f_bias, tm=_pick(s, 1024))

    bblk, cblk, are, aim = _ssm_params(ssm_a_re, ssm_a_im, ssm_log_dt, ssm_b_re, ssm_b_im,
                                       ssm_c_re, ssm_c_im)
    y_ssm = _ssm(u, bblk, cblk, are, aim, ssm_d, ssm_glu_w.astype(BF16), ssm_glu_b, ssm_out_g,
                 tm=_pick(s, 128))

    y_fox = _fox(q, k, v, fcum.reshape(b, FOX_PAIRS, 2, s), t=_pick(s, 512))

    return _tail(x, y_ssm, y_fox, wqk, vo, fox_out_g, w_out.astype(BF16), mix_post_g, xa_pre_g,
                 xa_post_g, ffn_pre_g,
                 w_gate.astype(BF16), w_up.astype(BF16), w_down.astype(BF16), ffn_post_g,
                 tm=_pick(s, 2 * TAIL_SUB_ROWS))
```

```python
import functools
import math

import jax
import jax.numpy as jnp
from jax import lax
from jax.experimental import pallas as pl
from jax.experimental.pallas import tpu as pltpu

F32 = jnp.float32
BF16 = jnp.bfloat16

LANES = 128
VMEM_LIMIT_BYTES = 56 * 1024 * 1024

D_MODEL = 1024
SSM_WIDTH = 512
SSM_GROUP_CH = 16
SSM_GROUPS = 32
SSM_STATE = 64
FOX_WIDTH = 512
FOX_HEAD_DIM = 64
FOX_HEADS = 8
FOX_PAIRS = FOX_HEADS * FOX_HEAD_DIM // LANES
XA_HEADS = 4
XA_HEAD_DIM = 256
RMS_EPS = 1e-6
NEG_INF = -1e30
LOG2E = math.log2(math.e)

STATE_TILES = 2 * SSM_GROUPS * SSM_STATE // LANES
RE_TILES = STATE_TILES // 2
U_TILES = SSM_WIDTH // LANES


def _rms(x, g):
    return x * lax.rsqrt(jnp.mean(x * x, axis=-1, keepdims=True) + RMS_EPS) * g


def _dot(a, b):
    return jnp.dot(a, b, preferred_element_type=F32)


def _dot_nt(a, b):
    return lax.dot_general(a, b, (((1,), (1,)), ((), ())), preferred_element_type=F32)


def _sigmoid(z):
    return 1.0 / (1.0 + jnp.exp(-z))


def _memkv_kernel(mem_ref, g_ref, wkv_ref, wq_ref, wo_ref, wqk_ref, vo_ref):
    m = mem_ref.shape[0]
    mn = _rms(mem_ref[...], g_ref[...]).astype(BF16)
    kv = _dot(mn, wkv_ref[...])
    for hd in range(XA_HEADS):
        sl = slice(hd * XA_HEAD_DIM, (hd + 1) * XA_HEAD_DIM)
        k_h = (kv[:, sl] * (1.0 / math.sqrt(XA_HEAD_DIM))).astype(BF16)
        v_h = kv[:, D_MODEL + hd * XA_HEAD_DIM:D_MODEL + (hd + 1) * XA_HEAD_DIM].astype(BF16)
        wqk_ref[:, hd * m:(hd + 1) * m] = _dot_nt(wq_ref[:, sl], k_h).astype(BF16)
        vo_ref[hd * m:(hd + 1) * m, :] = _dot(v_h, wo_ref[sl, :]).astype(BF16)


def _memkv(mem, mem_g, wkv, wq, wo):
    b, m, d = mem.shape
    c2 = lambda i: (0, 0)
    return pl.pallas_call(
        _memkv_kernel,
        grid=(b,),
        in_specs=[
            pl.BlockSpec((None, m, d), lambda i: (i, 0, 0)),
            pl.BlockSpec((1, d), c2),
            pl.BlockSpec((d, 2 * d), c2),
            pl.BlockSpec((d, d), c2),
            pl.BlockSpec((d, d), c2),
        ],
        out_specs=[
            pl.BlockSpec((None, d, XA_HEADS * m), lambda i: (i, 0, 0)),
            pl.BlockSpec((None, XA_HEADS * m, d), lambda i: (i, 0, 0)),
        ],
        out_shape=[jax.ShapeDtypeStruct((b, d, XA_HEADS * m), BF16),
                   jax.ShapeDtypeStruct((b, XA_HEADS * m, d), BF16)],
        compiler_params=pltpu.CompilerParams(
            dimension_semantics=("arbitrary",), vmem_limit_bytes=VMEM_LIMIT_BYTES),
        name="memkv",
    )(mem, mem_g.reshape(1, d), wkv, wq, wo)


CUMSUM_SEGMENT = 512


def _inproj_kernel(x_ref, g_ref, w_ref, wf_ref, fb_ref, tri_ref,
                   u_ref, q_ref, k_ref, v_ref, f_ref, carry_ref):
    @pl.when(pl.program_id(1) == 0)
    def _():
        carry_ref[...] = jnp.zeros_like(carry_ref)

    hb = _rms(x_ref[...], g_ref[...]).astype(BF16)
    proj = _dot(hb, w_ref[...])
    u_ref[...] = proj[:, :SSM_WIDTH].astype(BF16)
    o = SSM_WIDTH
    q_ref[...] = (proj[:, o:o + FOX_WIDTH] * (LOG2E / math.sqrt(FOX_HEAD_DIM))).astype(BF16)
    k_ref[...] = proj[:, o + FOX_WIDTH:o + 2 * FOX_WIDTH].astype(BF16)
    v_ref[...] = proj[:, o + 2 * FOX_WIDTH:].astype(BF16)

    z = _dot_nt(wf_ref[...], hb) + fb_ref[...]
    logf = jnp.minimum(z, 0.0) - jnp.log1p(jnp.exp(-jnp.abs(z)))
    p1 = logf.astype(BF16)
    r1 = logf - p1.astype(F32)
    p2 = r1.astype(BF16)
    p3 = (r1 - p2.astype(F32)).astype(BF16)
    tri = tri_ref[...]
    seg = tri.shape[0]
    run = carry_ref[...][:, :1]
    parts = []
    for c0 in range(0, logf.shape[1], seg):
        sl = slice(c0, c0 + seg)
        parts.append(_dot(p1[:, sl], tri) + _dot(p2[:, sl], tri) + _dot(p3[:, sl], tri) + run)
        run = parts[-1][:, -1:]
    f = jnp.concatenate(parts, axis=1)
    f_ref[...] = f * LOG2E
    carry_ref[...] = jnp.broadcast_to(f[:, -1:], carry_ref.shape)


def _inproj(x, mix_pre_g, w_main, wf_t, f_bias, tm):
    b, s, d = x.shape
    nt = s // tm
    seg = min(tm, CUMSUM_SEGMENT)
    tri = jnp.triu(jnp.ones((seg, seg), F32)).astype(BF16)
    row = lambda i, j: (i, j, 0)
    const2 = lambda i, j: (0, 0)
    return pl.pallas_call(
        _inproj_kernel,
        grid=(b, nt),
        in_specs=[
            pl.BlockSpec((None, tm, d), row),
            pl.BlockSpec((1, d), const2),
            pl.BlockSpec(w_main.shape, const2),
            pl.BlockSpec(wf_t.shape, const2),
            pl.BlockSpec((FOX_HEADS, 1), const2),
            pl.BlockSpec((seg, seg), const2),
        ],
        out_specs=[
            pl.BlockSpec((None, tm, SSM_WIDTH), row),
            pl.BlockSpec((None, tm, FOX_WIDTH), row),
            pl.BlockSpec((None, tm, FOX_WIDTH), row),
            pl.BlockSpec((None, tm, FOX_WIDTH), row),
            pl.BlockSpec((None, FOX_HEADS, tm), lambda i, j: (i, 0, j)),
        ],
        out_shape=[
            jax.ShapeDtypeStruct((b, s, SSM_WIDTH), BF16),
            jax.ShapeDtypeStruct((b, s, FOX_WIDTH), BF16),
            jax.ShapeDtypeStruct((b, s, FOX_WIDTH), BF16),
            jax.ShapeDtypeStruct((b, s, FOX_WIDTH), BF16),
            jax.ShapeDtypeStruct((b, FOX_HEADS, s), F32),
        ],
        scratch_shapes=[pltpu.VMEM((FOX_HEADS, LANES), F32)],
        compiler_params=pltpu.CompilerParams(
            dimension_semantics=("arbitrary", "arbitrary"), vmem_limit_bytes=VMEM_LIMIT_BYTES),
        name="inproj",
    )(x, mix_pre_g.reshape(1, d), w_main, wf_t, f_bias.reshape(FOX_HEADS, 1), tri)


def _gelu_tanh(x):
    c = math.sqrt(2.0 / math.pi)
    return 0.5 * x * (1.0 + jnp.tanh(c * (x + 0.044715 * (x * x * x))))


TILE_PITCH = 20
TIME_PITCH = 324
SCAN_UNROLL = 8


def _ssm_kernel(u_ref, bblk_ref, cblk_ref, are_ref, aim_ref, d_ref, gw_ref, gb_ref, g_ref,
                o_ref, s_ref, xc_ref, *, nb, tm):
    @pl.when(pl.program_id(0) == 0)
    def _():
        xc_ref[...] = jnp.zeros_like(xc_ref)

    def rows_of_tile(b, part, k):
        return pl.ds(TILE_PITCH * k + 2 * b + part, tm, stride=TIME_PITCH)

    for ut in range(U_TILES):
        ub = u_ref[:, :, ut * LANES:(ut + 1) * LANES].reshape(nb * tm, LANES)
        res = _dot(ub, bblk_ref[ut])
        for b in range(nb):
            for j in range(2 * U_TILES):
                part, k = divmod(j, U_TILES)
                s_ref[rows_of_tile(b, part, U_TILES * ut + k), :] = (
                    res[b * tm:(b + 1) * tm, j * LANES:(j + 1) * LANES])

    ar = are_ref[...]
    ai = aim_ref[...]

    def step(t, carry):
        new = []
        for b in range(nb):
            xr, xi = carry[2 * b], carry[2 * b + 1]
            ire = pl.ds(TIME_PITCH * t + 2 * b, RE_TILES, stride=TILE_PITCH)
            iim = pl.ds(TIME_PITCH * t + 2 * b + 1, RE_TILES, stride=TILE_PITCH)
            nr = ar * xr - ai * xi + s_ref[ire, :]
            ni = ar * xi + ai * xr + s_ref[iim, :]
            s_ref[ire, :] = nr
            s_ref[iim, :] = ni
            new += [nr, ni]
        return tuple(new)

    init = []
    for b in range(nb):
        init.append(xc_ref[b * STATE_TILES:b * STATE_TILES + RE_TILES, :])
        init.append(xc_ref[b * STATE_TILES + RE_TILES:(b + 1) * STATE_TILES, :])
    fin = lax.fori_loop(0, tm, step, tuple(init), unroll=SCAN_UNROLL)
    for b in range(nb):
        xc_ref[b * STATE_TILES:b * STATE_TILES + RE_TILES, :] = fin[2 * b]
        xc_ref[b * STATE_TILES + RE_TILES:(b + 1) * STATE_TILES, :] = fin[2 * b + 1]

    ys = []
    for ot in range(U_TILES):
        per_seq = []
        for b in range(nb):
            tiles = [s_ref[rows_of_tile(b, part, U_TILES * ot + k), :].astype(BF16)
                     for part in range(2) for k in range(U_TILES)]
            per_seq.append(jnp.concatenate(tiles, axis=1))
        ys.append(_dot(jnp.concatenate(per_seq, axis=0), cblk_ref[ot]))
    u_all = u_ref[...].reshape(nb * tm, SSM_WIDTH).astype(F32)
    y = jnp.concatenate(ys, axis=1) + d_ref[...] * u_all
    g = _gelu_tanh(y)
    out = g * _sigmoid(_dot(g.astype(BF16), gw_ref[...]) + gb_ref[...])
    o_ref[...] = _rms(out, g_ref[...]).astype(BF16).reshape(nb, tm, SSM_WIDTH)


def _ssm(u, bblk, cblk, a_re, a_im, d_skip, glu_w, glu_b, out_g, tm):
    b, s, w = u.shape
    assert 2 * b <= TILE_PITCH - 4 and TIME_PITCH >= TILE_PITCH * (RE_TILES - 1) + 2 * b
    kern = functools.partial(_ssm_kernel, nb=b, tm=tm)
    c2 = lambda t: (0, 0)
    c3 = lambda t: (0, 0, 0)
    return pl.pallas_call(
        kern,
        grid=(s // tm,),
        in_specs=[
            pl.BlockSpec((b, tm, w), lambda t: (0, t, 0)),
            pl.BlockSpec(bblk.shape, c3),
            pl.BlockSpec(cblk.shape, c3),
            pl.BlockSpec(a_re.shape, c2),
            pl.BlockSpec(a_im.shape, c2),
            pl.BlockSpec((1, w), c2),
            pl.BlockSpec(glu_w.shape, c2),
            pl.BlockSpec((1, w), c2),
            pl.BlockSpec((1, w), c2),
        ],
        out_specs=pl.BlockSpec((b, tm, w), lambda t: (0, t, 0)),
        out_shape=jax.ShapeDtypeStruct((b, s, w), BF16),
        scratch_shapes=[
            pltpu.VMEM((tm * TIME_PITCH, LANES), F32),
            pltpu.VMEM((b * STATE_TILES, LANES), F32),
        ],
        compiler_params=pltpu.CompilerParams(
            dimension_semantics=("arbitrary",), vmem_limit_bytes=VMEM_LIMIT_BYTES),
        name="ssm",
    )(u, bblk, cblk, a_re, a_im, d_skip.reshape(1, w), glu_w, glu_b.reshape(1, w),
      out_g.reshape(1, w))


def _ssm_params(a_re, a_im, log_dt, b_re, b_im, c_re, c_im):
    ar, ai = a_re.astype(F32), a_im.astype(F32)
    dt = jnp.exp(log_dt.astype(F32))[:, None]
    mag = jnp.exp(ar * dt)
    abr, abi = mag * jnp.cos(ai * dt), mag * jnp.sin(ai * dt)
    den = ar * ar + ai * ai
    cr = (((abr - 1.0) * ar + abi * ai) / den)[..., None]
    ci = ((abi * ar - (abr - 1.0) * ai) / den)[..., None]
    br, bi = b_re.astype(F32), b_im.astype(F32)
    bbr, bbi = cr * br - ci * bi, cr * bi + ci * br
    gpt = LANES // SSM_GROUP_CH
    eye = jnp.eye(gpt, dtype=F32)[None, :, None, :, None]

    def bmat(part):
        p = jnp.swapaxes(part.reshape(U_TILES, gpt, SSM_STATE, SSM_GROUP_CH), 2, 3)
        return (p[:, :, :, None, :] * eye).reshape(U_TILES, LANES, gpt * SSM_STATE)

    def cmat(part):
        p = jnp.swapaxes(part.reshape(U_TILES, gpt, SSM_GROUP_CH, SSM_STATE), 2, 3)
        return (p[:, :, :, None, :] * eye).reshape(U_TILES, gpt * SSM_STATE, LANES)

    bblk = jnp.concatenate([bmat(bbr), bmat(bbi)], axis=2).astype(BF16)
    cblk = jnp.concatenate([cmat(c_re.astype(F32)), cmat(-c_im.astype(F32))], axis=1).astype(BF16)
    return bblk, cblk, abr.reshape(RE_TILES, LANES), abi.reshape(RE_TILES, LANES)


BIAS_LANES = 8


def _split3(f):
    p1 = f.astype(BF16).astype(F32)
    r = f - p1
    p2 = r.astype(BF16).astype(F32)
    p3 = (r - p2).astype(BF16).astype(F32)
    return p1, p2, p3


def _bias_cols(f):
    n = f.shape[1]
    one = jnp.ones((1, n), F32)
    zero = jnp.zeros((2, n), F32)
    pieces = _split3(f)
    rows = []
    for key_side in (True, False):
        for h in range(2):
            ph = [p[h:h + 1, :] for p in pieces]
            rows += [one] * 3 + [-p for p in ph] if key_side else ph + [one] * 3
            rows.append(zero)
    pt = jnp.concatenate(rows, axis=0).astype(BF16)
    r = lax.broadcasted_iota(jnp.int32, (4 * BIAS_LANES, 2 * LANES), 0)
    c = lax.broadcasted_iota(jnp.int32, (4 * BIAS_LANES, 2 * LANES), 1)
    place = (c == jnp.where(r < 2 * BIAS_LANES, r, r + LANES - 2 * BIAS_LANES)).astype(BF16)
    cols = lax.dot_general(pt, place, (((0,), (0,)), ((), ())), preferred_element_type=F32)
    return cols[:, :LANES].astype(BF16), cols[:, LANES:].astype(BF16)


V_ROWS = 144
ONES_ROW = 2 * FOX_HEAD_DIM
QUERY_SLAB = 256
SKIP_EXPONENT = 1e30


def _fox_kernel(q_ref, k_ref, v_ref, frow_ref, o_ref, kaug_ref, qaug_ref, vt_ref, j0_ref,
                lhs_ref, s_ref, m_ref, acc_ref, *, t):
    i = pl.program_id(2)
    nq = pl.num_programs(2)
    seq = k_ref.shape[0]
    lane = lax.broadcasted_iota(jnp.int32, (t, LANES), 1)
    first = lane < FOX_HEAD_DIM

    def eye(rows, cols):
        return (lax.broadcasted_iota(jnp.int32, (rows, cols), 0) ==
                lax.broadcasted_iota(jnp.int32, (rows, cols), 1)).astype(BF16)

    def store_lhs(qi):
        r0 = pl.multiple_of(qi * t, t)
        q = q_ref[pl.ds(r0, t), :]
        bias = qaug_ref[pl.ds(r0, t), :]
        zero = jnp.zeros_like(q)
        fzero = jnp.zeros_like(bias)
        slot = qi & 1
        lhs_ref[slot, :t, :] = jnp.concatenate(
            [jnp.where(first, q, zero), jnp.where(lane < BIAS_LANES, bias, fzero)], axis=1)
        lhs_ref[slot, t:, :] = jnp.concatenate(
            [jnp.where(first, zero, q), jnp.where(lane < BIAS_LANES, fzero, bias)], axis=1)

    def reset_stats():
        m_ref[...] = jnp.full_like(m_ref, NEG_INF)
        acc_ref[...] = jnp.zeros_like(acc_ref)

    def head_stats():
        qk = jnp.concatenate([q_ref[...], k_ref[...]], axis=1)
        qkt = _dot_nt(eye(2 * LANES, 2 * LANES), qk)
        qt, kt = qkt[:LANES], qkt[LANES:]

        def per_head(x):
            return jnp.concatenate([jnp.sum(x[:FOX_HEAD_DIM], axis=0, keepdims=True),
                                    jnp.sum(x[FOX_HEAD_DIM:], axis=0, keepdims=True)], axis=0)
        return per_head(qt * qt), per_head(kt * kt), per_head(qt * kt)

    def first_block(qi, qn2, kmax2, diag):
        r0 = qi * t
        f = frow_ref[...]
        rows = jnp.sqrt(qn2[:, r0:r0 + t] * kmax2) + f[:, r0:r0 + t] - diag[:, r0:r0 + t]
        tau = jnp.max(rows, axis=1, keepdims=True) + SKIP_EXPONENT
        pos = lax.broadcasted_iota(jnp.int32, (1, seq), 1)
        far = (f[0:1, :] > tau[0:1, :]) & (f[1:2, :] > tau[1:2, :])
        ends = ((pos & (t - 1)) == t - 1) & (pos < r0)
        return jnp.sum((far & ends).astype(jnp.int32))

    def step(jc, jp, causal, qp=None, diag_jc=False):
        slot = (i if qp is None else qp) & 1

        def keys_seen(c0, diagonal):
            return min(t, c0 % t + QUERY_SLAB) if diagonal else t
        if jc is not None:
            vt = vt_ref[jc]
        if jp is not None:
            k0 = pl.multiple_of(jp * t, t)
            rk = jnp.concatenate([k_ref[pl.ds(k0, t), :], kaug_ref[pl.ds(k0, t), :]], axis=1)
        for c0 in range(0, 2 * t, QUERY_SLAB):
            sl = slice(c0, c0 + QUERY_SLAB)
            if jc is not None:
                nk = keys_seen(c0, diag_jc)
                s = s_ref[:nk, sl]
                m_old = m_ref[:, sl]
                m_new = jnp.maximum(m_old, jnp.max(s, axis=0, keepdims=True))
                alpha = jnp.exp2(m_old - m_new)
                p = jnp.exp2(s - m_new).astype(BF16)
                acc_ref[:, sl] = alpha * acc_ref[:, sl] + _dot(vt[:, :nk], p)
                m_ref[:, sl] = m_new
            if jp is not None:
                nk = keys_seen(c0, causal is True)
                s = _dot_nt(rk[:nk], lhs_ref[slot, sl, :])
                if causal is not False:
                    ahead = 0 if causal is True else causal
                    key = lax.broadcasted_iota(jnp.int32, s.shape, 0)
                    qry = (lax.broadcasted_iota(jnp.int32, s.shape, 1) + c0) & (t - 1)
                    s = jnp.where(qry + ahead >= key, s, NEG_INF)
                s_ref[:nk, sl] = s

    @pl.when(i == 0)
    def _():
        kaug_ref[...], qaug_ref[...] = _bias_cols(frow_ref[...])
        qn2, kn2, diag = head_stats()
        kmax2 = jnp.max(kn2, axis=1, keepdims=True)
        for qi in range(seq // t):
            j0_ref[qi] = first_block(qi, qn2, kmax2, diag)
        vt = _dot_nt(eye(V_ROWS, LANES), v_ref[...])
        row = lax.broadcasted_iota(jnp.int32, vt.shape, 0)
        vt = jnp.where(row == ONES_ROW, 1.0, vt).astype(BF16)
        for jb in range(seq // t):
            vt_ref[jb] = vt[:, jb * t:(jb + 1) * t]
        store_lhs(0)
        reset_stats()
        step(None, 0, True)

    j0 = j0_ref[i]
    n_steady = jnp.maximum(i - 1 - j0, 0)

    def steady_pair(jj, _):
        j = j0 + 2 * jj
        step(j, j + 1, False)
        step(j + 1, j + 2, False)
        return 0
    lax.fori_loop(0, n_steady >> 1, steady_pair, 0)

    def finish():
        nxt = jnp.minimum(i + 1, nq - 1)
        j0_nxt = j0_ref[nxt]
        store_lhs(nxt)
        step(i, j0_nxt, (nxt - j0_nxt) * t, nxt, diag_jc=True)

        acc = acc_ref[...]
        inv_l = 1.0 / acc[ONES_ROW:ONES_ROW + 1, :]
        o_ref[...] = jnp.concatenate([acc[:FOX_HEAD_DIM, :t] * inv_l[:, :t],
                                      acc[FOX_HEAD_DIM:ONES_ROW, t:] * inv_l[:, t:]], axis=0).astype(BF16)
        reset_stats()

    odd = (n_steady & 1) == 1

    @pl.when(odd)
    def _():
        step(i - 2, i - 1, False)
        step(i - 1, i, True)
        finish()

    @pl.when((i > j0) & jnp.logical_not(odd))
    def _():
        step(i - 1, i, True)
        finish()

    @pl.when(i <= j0)
    def _():
        finish()


def _fox(q, k, v, frow, t):
    b, s, w = q.shape
    assert s % t == 0 and t & (t - 1) == 0
    kern = functools.partial(_fox_kernel, t=t)
    whole = lambda bi, p, i: (bi, 0, p)
    return pl.pallas_call(
        kern,
        grid=(b, FOX_PAIRS, s // t),
        in_specs=[
            pl.BlockSpec((None, s, LANES), whole),
            pl.BlockSpec((None, s, LANES), whole),
            pl.BlockSpec((None, s, LANES), whole),
            pl.BlockSpec((None, None, 2, s), lambda bi, p, i: (bi, p, 0, 0)),
        ],
        out_specs=pl.BlockSpec((None, LANES, t), lambda bi, p, i: (bi, p, i)),
        out_shape=jax.ShapeDtypeStruct((b, w, s), BF16),
        scratch_shapes=[
            pltpu.VMEM((s, LANES), BF16),
            pltpu.VMEM((s, LANES), BF16),
            pltpu.VMEM((s // t, V_ROWS, t), BF16),
            pltpu.SMEM((s // t,), jnp.int32),
            pltpu.VMEM((2, 2 * t, 2 * LANES), BF16),
            pltpu.VMEM((t, 2 * t), F32),
            pltpu.VMEM((1, 2 * t), F32),
            pltpu.VMEM((V_ROWS, 2 * t), F32),
        ],
        compiler_params=pltpu.CompilerParams(
            dimension_semantics=("arbitrary", "arbitrary", "arbitrary"),
            vmem_limit_bytes=VMEM_LIMIT_BYTES),
        name="fox",
    )(q, k, v, frow)


TAIL_SUB_ROWS = 256
FFN_CHUNK = 1024


def _tail_kernel(x_ref, ys_ref, yf_ref, wqk_ref, vo_ref,
                 fox_g_ref, w_out_ref, mix_post_ref, xa_pre_ref, xa_post_ref,
                 ffn_pre_ref, wg_ref, wu_ref, wd_ref, ffn_post_ref, o_ref, *, n_sub):
    sub = x_ref.shape[0] // n_sub
    rows = [slice(r * sub, (r + 1) * sub) for r in range(n_sub)]
    each = lambda f, *lists: [f(*a) for a in zip(*lists)]

    x = [x_ref[r, :] for r in rows]
    yft = [yf_ref[:, r].astype(F32) for r in rows]
    yft = [(y * lax.rsqrt(jnp.mean(y * y, axis=0, keepdims=True) + RMS_EPS) * fox_g_ref[...]).astype(BF16)
           for y in yft]
    mix = [_dot(ys_ref[r, :], w_out_ref[:SSM_WIDTH, :]) +
           lax.dot_general(f, w_out_ref[SSM_WIDTH:, :], (((0,), (0,)), ((), ())), preferred_element_type=F32)
           for r, f in zip(rows, yft)]
    x = each(lambda xi, mi: xi + _rms(mi, mix_post_ref[...]), x, mix)

    h = [_rms(xi, xa_pre_ref[...]).astype(BF16) for xi in x]
    scores = [_dot(hi, wqk_ref[...]) for hi in h]
    m_tok = wqk_ref.shape[1] // XA_HEADS
    probs = [[] for _ in rows]
    for hd in range(XA_HEADS):
        for si, out in zip(scores, probs):
            s = si[:, hd * m_tok:(hd + 1) * m_tok]
            p = jnp.exp(s - jnp.max(s, axis=-1, keepdims=True))
            out.append((p / jnp.sum(p, axis=-1, keepdims=True)).astype(BF16))
    p_all = [jnp.concatenate(ps, axis=1) for ps in probs]
    x = each(lambda xi, pi: xi + _rms(_dot(pi, vo_ref[...]), xa_post_ref[...]), x, p_all)

    h = [_rms(xi, ffn_pre_ref[...]).astype(BF16) for xi in x]
    hidden = wg_ref.shape[1]
    down = [None] * n_sub
    for c0 in range(0, hidden, FFN_CHUNK):
        c1 = min(c0 + FFN_CHUNK, hidden)
        for r, hi in enumerate(h):
            gate = _dot(hi, wg_ref[:, c0:c1])
            up = _dot(hi, wu_ref[:, c0:c1])
            act = (gate * _sigmoid(gate) * up).astype(BF16)
            part = _dot(act, wd_ref[c0:c1, :])
            down[r] = part if down[r] is None else down[r] + part
    for r, xi, di in zip(rows, x, down):
        o_ref[r, :] = xi + _rms(di, ffn_post_ref[...])


def _tail(x, ys, yf, wqk, vo, fox_g, w_out, mix_post_g, xa_pre_g, xa_post_g,
          ffn_pre_g, wg, wu, wd, ffn_post_g, tm):
    b, s, d = x.shape
    row = lambda i, j: (i, j, 0)
    per_b = lambda i, j: (i, 0, 0)
    const = lambda i, j: (0, 0)

    def resident(a):
        return pl.BlockSpec(a.shape, const, pipeline_mode=pl.Buffered(1))

    gains = [g.reshape(1, -1) for g in (mix_post_g, xa_pre_g, xa_post_g, ffn_pre_g, ffn_post_g)]
    mix_post_g, xa_pre_g, xa_post_g, ffn_pre_g, ffn_post_g = gains
    fox_g = fox_g.reshape(-1, 1)
    args = (x, ys, yf, wqk, vo, fox_g, w_out, mix_post_g, xa_pre_g, xa_post_g,
            ffn_pre_g, wg, wu, wd, ffn_post_g)
    in_specs = [
        pl.BlockSpec((None, tm, d), row),
        pl.BlockSpec((None, tm, SSM_WIDTH), row),
        pl.BlockSpec((None, FOX_WIDTH, tm), lambda i, j: (i, 0, j)),
        pl.BlockSpec((None,) + wqk.shape[1:], per_b),
        pl.BlockSpec((None,) + vo.shape[1:], per_b),
    ] + [resident(a) for a in args[5:]]
    return pl.pallas_call(
        functools.partial(_tail_kernel, n_sub=tm // TAIL_SUB_ROWS),
        grid=(b, s // tm),
        in_specs=in_specs,
        out_specs=pl.BlockSpec((None, tm, d), row),
        out_shape=jax.ShapeDtypeStruct((b, s, d), F32),
        compiler_params=pltpu.CompilerParams(
            dimension_semantics=("arbitrary", "arbitrary"), vmem_limit_bytes=VMEM_LIMIT_BYTES),
        name="tail",
    )(*args)


def _pick(n, pref):
    t = min(n, pref)
    assert n % t == 0, (n, t)
    return t


def kernel(x, mem, mix_pre_g, w_in, ssm_a_re, ssm_a_im, ssm_log_dt, ssm_b_re, ssm_b_im, ssm_c_re, ssm_c_im, ssm_d, ssm_glu_w, ssm_glu_b, fox_f_bias, ssm_out_g, fox_out_g, w_out, mix_post_g, xa_pre_g, mem_g, xa_wq, xa_wkv, xa_wo, xa_post_g, ffn_pre_g, w_gate, w_up, w_down, ffn_post_g):
    b, s, d = x.shape
    assert d == D_MODEL and s % LANES == 0
    n_main = SSM_WIDTH + 3 * FOX_WIDTH

    wqk, vo = _memkv(mem, mem_g, xa_wkv.astype(BF16), xa_wq.astype(BF16), xa_wo.astype(BF16))

    u, q, k, v, fcum = _inproj(
        x, mix_pre_g, w_in[:, :n_main].astype(BF16), w_in[:, n_main:].T.astype(BF16),
        fox_f_bias, tm=_pick(s, 1024))

    bblk, cblk, are, aim = _ssm_params(ssm_a_re, ssm_a_im, ssm_log_dt, ssm_b_re, ssm_b_im,
                                       ssm_c_re, ssm_c_im)
    y_ssm = _ssm(u, bblk, cblk, are, aim, ssm_d, ssm_glu_w.astype(BF16), ssm_glu_b, ssm_out_g,
                 tm=_pick(s, 128))

    y_fox = _fox(q, k, v, fcum.reshape(b, FOX_PAIRS, 2, s), t=_pick(s, 512))

    return _tail(x, y_ssm, y_fox, wqk, vo, fox_out_g, w_out.astype(BF16), mix_post_g, xa_pre_g,
                 xa_post_g, ffn_pre_g,
                 w_gate.astype(BF16), w_up.astype(BF16), w_down.astype(BF16), ffn_post_g,
                 tm=_pick(s, 2 * TAIL_SUB_ROWS))
```
